```python
import math
import jax, jax.numpy as jnp
from jax import lax
import numpy as np

D_MODEL = 1024
BATCH = 4
SEQ = 4096
DEPTH = 4

CTX_LEN = 256
GRID_W = 64
HG_WIDTH = D_MODEL // 2
HG_DIM = 128
HG_HEADS = HG_WIDTH // HG_DIM
CHUNK = 64
DA_WIDTH = D_MODEL // 4
DA_HEADS = 4
DA_V = DA_WIDTH // DA_HEADS
DA_QK = DA_V // 2
DA_QK_WIDTH = DA_HEADS * 2 * DA_QK
Q_BLOCK = 128
ROPE_THETA = 10000.0
ROPE_AXIS_DIM = DA_QK // 2
FT_WIDTH = D_MODEL // 4
FT_GROUPS = 4
MIX_WIDTH = HG_WIDTH + DA_WIDTH + FT_WIDTH
IN_SPLITS = (HG_WIDTH, HG_WIDTH, HG_WIDTH, HG_WIDTH, HG_WIDTH, DA_QK_WIDTH, DA_QK_WIDTH, DA_WIDTH, FT_WIDTH)
IN_WIDTH = sum(IN_SPLITS)
FF_HIDDEN = ((8 * D_MODEL // 3 + 255) // 256) * 256
N_MOD = 6
EPS = 1e-6

kernel_name = "hymba_style_hgrn2_diffattn_fnet_dit"


def rms_norm(x, g):
    xf = x.astype(jnp.float32)
    y = xf * lax.rsqrt(jnp.mean(xf * xf, axis=-1, keepdims=True) + EPS)
    return (y * g.astype(jnp.float32)).astype(x.dtype)


def adaln(x, g, shift, scale):
    return rms_norm(x, g) * (1 + scale) + shift


def modulation(cond, w, b):
    return jnp.split(jax.nn.silu(cond) @ w + b, N_MOD, axis=-1)


def split_heads(a, h):
    return a.reshape(a.shape[:-1] + (h, a.shape[-1] // h))


def split_projection(p):
    offsets = [int(o) for o in np.cumsum(IN_SPLITS)[:-1]]
    return jnp.split(p, offsets, axis=-1)


def axial_rope(t_len):
    n_rows = t_len // GRID_W
    rows = jnp.broadcast_to(jnp.arange(n_rows)[:, None], (n_rows, GRID_W)).reshape(-1)
    cols = jnp.broadcast_to(jnp.arange(GRID_W)[None, :], (n_rows, GRID_W)).reshape(-1)
    inv_freq = 1.0 / (ROPE_THETA ** (jnp.arange(0, ROPE_AXIS_DIM, 2, dtype=jnp.float32) / ROPE_AXIS_DIM))
    ang = jnp.stack([rows, cols], axis=-1).astype(jnp.float32)[:, :, None] * inv_freq
    return jnp.cos(ang), jnp.sin(ang)


def apply_rope(x, cos, sin):
    xs = x.reshape(x.shape[:-1] + (2, 2, ROPE_AXIS_DIM // 2)).astype(jnp.float32)
    cs, sn = cos[None, :, None, None], sin[None, :, None, None]
    x1, x2 = xs[..., 0, :], xs[..., 1, :]
    out = jnp.stack([x1 * cs - x2 * sn, x2 * cs + x1 * sn], axis=-2)
    return out.reshape(x.shape).astype(x.dtype)


def gla_chunkwise(q, k, v, log_f, s0):
    b_, t_, h_, _ = q.shape
    dv = v.shape[-1]
    n = t_ // CHUNK

    def chunks(a):
        return a.astype(jnp.float32).reshape(b_, n, CHUNK, h_, a.shape[-1]).transpose(1, 0, 3, 2, 4)

    incl = jnp.tril(jnp.ones((CHUNK, CHUNK), dtype=bool))[:, :, None]

    def step(state, inp):
        qi, ki, vi, gi = inp
        bcum = jnp.cumsum(gi, axis=2)
        o_inter = jnp.einsum('bhtk,bhkv->bhtv', qi * jnp.exp(bcum), state)
        rel = bcum[:, :, :, None, :] - bcum[:, :, None, :, :]
        decay = jnp.exp(jnp.where(incl, rel, -jnp.inf))
        scores = jnp.einsum('bhtk,bhsk,bhtsk->bhts', qi, ki, decay)
        o_intra = jnp.einsum('bhts,bhsv->bhtv', scores, vi)
        blast = bcum[:, :, -1:, :]
        state = state * jnp.exp(blast[:, :, 0, :, None]) + jnp.einsum(
            'bhsk,bhsv->bhkv', ki * jnp.exp(blast - bcum), vi)
        return state, o_inter + o_intra

    s_fin, o = lax.scan(step, s0, (chunks(q), chunks(k), chunks(v), chunks(log_f)))
    return o.transpose(1, 0, 3, 2, 4).reshape(b_, t_, h_, dv), s_fin


def hgrn2_scan(q, v, z, lb, s0, reverse):
    f = lb + (1.0 - lb) * jax.nn.sigmoid(z.astype(jnp.float32))
    k = split_heads(1.0 - f, HG_HEADS)
    log_f = split_heads(jnp.log(f), HG_HEADS)
    if reverse:
        q, k, v, log_f = jnp.flip(q, 1), jnp.flip(k, 1), jnp.flip(v, 1), jnp.flip(log_f, 1)
    o, s = gla_chunkwise(q, k, v, log_f, s0)
    if reverse:
        o = jnp.flip(o, 1)
    return o, s


def diff_softmax_attend(q, keys, vals, lam):
    s = jnp.einsum('bqhmd,bkhmd->bhmqk', q, keys).astype(jnp.float32) * (DA_QK ** -0.5)
    p = jax.nn.softmax(s, axis=-1)
    w = p[:, :, 0] - lam * p[:, :, 1]
    return jnp.einsum('bhqk,bkhv->bqhv', w.astype(vals.dtype), vals)


def fourier_mix(u):
    uh = split_heads(u, FT_GROUPS).astype(jnp.float32)
    y = jnp.fft.fft2(uh, axes=(1, 3), norm="ortho").real
    return y.reshape(u.shape).astype(u.dtype)


def mixer(hl, hc, w_in, w_out, lb_f, lb_b, hg_onorm, da_lam, da_subln, lam_init, cos, sin, with_ctx):
    pl = split_projection(hl @ w_in)
    pc = split_projection(hc @ w_in)
    bsz, t_lat = hl.shape[0], hl.shape[1]

    s0 = jnp.zeros((bsz, HG_HEADS, HG_DIM, HG_DIM), jnp.float32)
    q_c, v_c = split_heads(jax.nn.silu(pc[0]), HG_HEADS), split_heads(pc[1], HG_HEADS)
    q_l, v_l = split_heads(jax.nn.silu(pl[0]), HG_HEADS), split_heads(pl[1], HG_HEADS)
    oc_f, st_f = hgrn2_scan(q_c, v_c, pc[3], lb_f, s0, False)
    oc_b, st_b = hgrn2_scan(q_c, v_c, pc[4], lb_b, s0, True)
    ol_f, _ = hgrn2_scan(q_l, v_l, pl[3], lb_f, st_f, False)
    ol_b, _ = hgrn2_scan(q_l, v_l, pl[4], lb_b, st_b, True)

    def hgrn_out(o, g):
        return (rms_norm(o, hg_onorm).reshape(g.shape) * jax.nn.silu(g)).astype(g.dtype)

    def qk_heads(a):
        return a.reshape(a.shape[:-1] + (DA_HEADS, 2, DA_QK))

    lam_f = da_lam.astype(jnp.float32)
    lam = jnp.exp(jnp.sum(lam_f[0] * lam_f[1])) - jnp.exp(jnp.sum(lam_f[2] * lam_f[3])) + lam_init
    dq_l, dk_l = apply_rope(qk_heads(pl[5]), cos, sin), apply_rope(qk_heads(pl[6]), cos, sin)
    dv_l = split_heads(pl[7], DA_HEADS)
    dq_c, dk_c, dv_c = qk_heads(pc[5]), qk_heads(pc[6]), split_heads(pc[7], DA_HEADS)
    keys = jnp.concatenate([dk_l, dk_c], axis=1)
    vals = jnp.concatenate([dv_l, dv_c], axis=1)
    nb = t_lat // Q_BLOCK
    qb = dq_l.reshape((bsz, nb, Q_BLOCK) + dq_l.shape[2:]).transpose(1, 0, 2, 3, 4, 5)
    od_l = lax.map(lambda qq: diff_softmax_attend(qq, keys, vals, lam), qb)
    od_l = od_l.transpose(1, 0, 2, 3, 4).reshape(bsz, t_lat, DA_HEADS, DA_V)

    def diff_out(o):
        return (rms_norm(o, da_subln) * (1.0 - lam_init)).reshape(o.shape[:2] + (DA_WIDTH,))

    yl = jnp.concatenate([hgrn_out(ol_f + ol_b, pl[2]), diff_out(od_l).astype(hl.dtype),
                          fourier_mix(pl[8])], axis=-1) @ w_out
    if not with_ctx:
        return yl, None
    od_c = diff_softmax_attend(dq_c, dk_c, dv_c, lam)
    yc = jnp.concatenate([hgrn_out(oc_f + oc_b, pc[2]), diff_out(od_c).astype(hc.dtype),
                          fourier_mix(pc[8])], axis=-1) @ w_out
    return yl, yc


def swiglu(h, w_i, w_o):
    g, u = jnp.split(h @ w_i, 2, axis=-1)
    return (jax.nn.silu(g) * u) @ w_o


def setup_inputs(seed: int = 0) -> dict:
    key = jax.random.key(seed)
    ks = jax.random.split(key, 15)
    f32 = jnp.float32
    nrm = lambda k, s: jax.random.normal(k, s, f32)
    return {
        "x": nrm(ks[0], (BATCH, SEQ, D_MODEL)),
        "c": nrm(ks[1], (BATCH, D_MODEL)),
        "ctx": nrm(ks[2], (BATCH, CTX_LEN, D_MODEL)),
        "c_ctx": nrm(ks[3], (D_MODEL,)),
        "w_mod": nrm(ks[4], (DEPTH, D_MODEL, N_MOD * D_MODEL)) * (0.5 * D_MODEL ** -0.5),
        "b_mod": nrm(ks[5], (DEPTH, N_MOD * D_MODEL)) * 0.01,
        "norm_g": 1.0 + 0.02 * nrm(ks[6], (DEPTH, 4, D_MODEL)),
        "w_in": nrm(ks[7], (DEPTH, D_MODEL, IN_WIDTH)) * D_MODEL ** -0.5,
        "w_out": nrm(ks[8], (DEPTH, MIX_WIDTH, D_MODEL)) * MIX_WIDTH ** -0.5,
        "hg_lb_logits": 0.1 * nrm(ks[9], (2, DEPTH, HG_WIDTH)),
        "hg_onorm": 1.0 + 0.02 * nrm(ks[10], (DEPTH, HG_DIM)),
        "da_lambda": 0.1 * nrm(ks[11], (DEPTH, 4, DA_QK)),
        "da_subln": 1.0 + 0.02 * nrm(ks[12], (DEPTH, DA_V)),
        "w_ffn_in": nrm(ks[13], (DEPTH, D_MODEL, 2 * FF_HIDDEN)) * D_MODEL ** -0.5,
        "w_ffn_out": nrm(ks[14], (DEPTH, FF_HIDDEN, D_MODEL)) * FF_HIDDEN ** -0.5,
    }


def reference(x, c, ctx, c_ctx, w_mod, b_mod, norm_g, w_in, w_out, hg_lb_logits, hg_onorm,
              da_lambda, da_subln, w_ffn_in, w_ffn_out):
    cos, sin = axial_rope(x.shape[1])
    lb_p = jax.nn.softmax(hg_lb_logits.astype(jnp.float32), axis=1)
    lower_bounds = jnp.cumsum(lb_p, axis=1) - lb_p[:, :1]
    xl, xc = x, ctx
    for l in range(DEPTH):
        with_ctx = l < DEPTH - 1
        lam_init = 0.8 - 0.6 * math.exp(-0.3 * l)
        ml = modulation(c[:, None, :], w_mod[l], b_mod[l])
        mc = modulation(c_ctx[None, None, :], w_mod[l], b_mod[l])
        g = norm_g[l]
        hl = adaln(xl, g[0], ml[0], ml[1])
        hc = adaln(xc, g[0], mc[0], mc[1])
        yl, yc = mixer(hl, hc, w_in[l], w_out[l], lower_bounds[0, l], lower_bounds[1, l], hg_onorm[l],
                       da_lambda[l], da_subln[l], lam_init, cos, sin, with_ctx)
        xl = xl + ml[2] * rms_norm(yl, g[1])
        xl = xl + ml[5] * rms_norm(swiglu(adaln(xl, g[2], ml[3], ml[4]), w_ffn_in[l], w_ffn_out[l]), g[3])
        if with_ctx:
            xc = xc + mc[2] * rms_norm(yc, g[1])
            xc = xc + mc[5] * rms_norm(swiglu(adaln(xc, g[2], mc[3], mc[4]), w_ffn_in[l], w_ffn_out[l]), g[3])
    return xl
```

```python
import functools
import math

import numpy as np
import jax
import jax.numpy as jnp
from jax import lax
from jax.experimental import pallas as pl
from jax.experimental.pallas import tpu as pltpu

F32 = jnp.float32
BF16 = jnp.bfloat16

D_MODEL = 1024
GRID_W = 64
HG_WIDTH = 512
HG_DIM = 128
HG_HEADS = 4
DA_WIDTH = 256
DA_HEADS = 4
DA_V = 64
DA_QK = 32
FT_WIDTH = 256
FT_GROUPS = 4
FT_GDIM = FT_WIDTH // FT_GROUPS
FF_HIDDEN = 2816
N_MOD = 6
EPS = 1e-6
ROPE_THETA = 10000.0
ROPE_AXIS_DIM = DA_QK // 2

_OFF_Q, _OFF_I, _OFF_G, _OFF_F, _OFF_DQ, _OFF_DK, _OFF_DV, _OFF_FT, IN_WIDTH = (
    0, 512, 1024, 1536, 2560, 2816, 3072, 3328, 3584)

TM = 256
TQ = 256
KC = 256
CH = 64
SUB = 8
TK = 256
VT_ROWS = 80
MOD_ROWS = 8
MOD_TN = 1536
VMEM_LIMIT = 52 * 1024 * 1024

Q_SCALE = (DA_QK ** -0.5) * math.log2(math.e)


def _silu(x):
    return x * jax.nn.sigmoid(x)


def _rms(x):
    return x * lax.rsqrt(jnp.mean(x * x, axis=-1, keepdims=True) + EPS)


def _cparams(sem):
    return pltpu.CompilerParams(dimension_semantics=sem, vmem_limit_bytes=VMEM_LIMIT)


def _mod_kernel(c_ref, w_ref, b_ref, o_ref):
    a = _silu(c_ref[...]).astype(BF16)
    w = w_ref[0].astype(BF16)
    o_ref[0] = jnp.dot(a, w, preferred_element_type=F32) + b_ref[0]


def _modulation(cond, w_mod, b_mod):
    depth, d, n = w_mod.shape
    return pl.pallas_call(
        _mod_kernel,
        grid=(depth, n // MOD_TN),
        in_specs=[
            pl.BlockSpec((MOD_ROWS, d), lambda l, j: (0, 0)),
            pl.BlockSpec((1, d, MOD_TN), lambda l, j: (l, 0, j)),
            pl.BlockSpec((1, 1, MOD_TN), lambda l, j: (l, 0, j)),
        ],
        out_specs=pl.BlockSpec((1, MOD_ROWS, MOD_TN), lambda l, j: (l, 0, j)),
        out_shape=jax.ShapeDtypeStruct((depth, MOD_ROWS, n), F32),
        compiler_params=_cparams(("arbitrary", "arbitrary")),
        name="modulation",
    )(cond, w_mod, b_mod.reshape(depth, 1, n))


def _prep_kernel(lb_ref, lam_ref, lam_init_ref, lbo_ref, lamo_ref, *, depth):
    rows = [lb_ref[l:l + 1, :] for l in range(depth)]
    m = rows[0]
    for r in rows[1:]:
        m = jnp.maximum(m, r)
    e = [jnp.exp(r - m) for r in rows]
    tot = e[0]
    for r in e[1:]:
        tot = tot + r
    p = [r / tot for r in e]
    acc = p[0]
    lbo_ref[0:1, :] = acc - p[0]
    for l in range(1, depth):
        acc = acc + p[l]
        lbo_ref[l:l + 1, :] = acc - p[0]
    x = lam_ref[...]
    a = jnp.sum(x[:, 0:DA_QK] * x[:, DA_QK:2 * DA_QK], axis=-1, keepdims=True)
    b = jnp.sum(x[:, 2 * DA_QK:3 * DA_QK] * x[:, 3 * DA_QK:4 * DA_QK], axis=-1, keepdims=True)
    lamo_ref[...] = jnp.exp(a) - jnp.exp(b) + lam_init_ref[...]


def _prep(hg_lb_logits, da_lambda, lam_init):
    depth = hg_lb_logits.shape[1]
    lb_in = jnp.transpose(hg_lb_logits.astype(F32), (1, 0, 2)).reshape(depth, 2 * HG_WIDTH)
    lam_in = da_lambda.astype(F32).reshape(depth, 4 * DA_QK)
    lam_init_arr = jnp.asarray(np.broadcast_to(np.asarray(lam_init, np.float32)[:, None], (depth, TQ)))
    lb, lam = pl.pallas_call(
        functools.partial(_prep_kernel, depth=depth),
        out_shape=(jax.ShapeDtypeStruct((depth, 2 * HG_WIDTH), F32),
                   jax.ShapeDtypeStruct((depth, TQ), F32)),
        name="param_prep",
    )(lb_in, lam_in, lam_init_arr)
    return lb.reshape(depth, 2, HG_WIDTH), lam


def _inproj_kernel(x_ref, mod_ref, g_ref, w_ref, cos_ref, sin_ref, lb_ref, cs_ref,
                   q_ref, i_ref, gate_ref, lf_ref, dq_ref, dk_ref, dv_ref, f_ref):
    x = x_ref[0]
    mod = mod_ref[0]
    h = (_rms(x) * g_ref[0:1, :]) * (1.0 + mod[1:2, :]) + mod[0:1, :]
    hb = h.astype(BF16)

    def proj(a, b):
        return jnp.dot(hb, w_ref[:, a:b], preferred_element_type=F32)

    q_ref[0] = _silu(proj(_OFF_Q, _OFF_I)).astype(BF16)
    i_ref[0] = proj(_OFF_I, _OFF_G).astype(BF16)
    gate_ref[0] = _silu(proj(_OFF_G, _OFF_F)).astype(BF16)
    for d in range(2):
        z = proj(_OFF_F + d * HG_WIDTH, _OFF_F + (d + 1) * HG_WIDTH)
        lb = lb_ref[d:d + 1, :]
        lf_ref[d, 0] = jnp.log(lb + (1.0 - lb) * jax.nn.sigmoid(z))

    cos = cos_ref[...]
    sin = sin_ref[...]
    lane = lax.broadcasted_iota(jnp.int32, cos.shape, 1)
    upper_half = (lane & (ROPE_AXIS_DIM // 2)) != 0

    def rope(t):
        partner = jnp.where(upper_half,
                            pltpu.roll(t, ROPE_AXIS_DIM // 2, 1),
                            pltpu.roll(t, 2 * DA_HEADS * DA_QK - ROPE_AXIS_DIM // 2, 1))
        return t * cos + partner * sin

    dq_ref[0] = (rope(proj(_OFF_DQ, _OFF_DK)) * Q_SCALE).astype(BF16)
    dk_ref[0] = rope(proj(_OFF_DK, _OFF_DV)).astype(BF16)
    dv_ref[0] = proj(_OFF_DV, _OFF_FT).astype(BF16)
    ft = proj(_OFF_FT, IN_WIDTH).astype(BF16)
    cs = jnp.dot(ft, cs_ref[...], preferred_element_type=F32)
    f_ref[0] = cs[:, :FT_WIDTH].astype(BF16)
    f_ref[1] = cs[:, FT_WIDTH:].astype(BF16)


def _inproj(X, mod_l, g_l, w_in_l, cos_t, sin_t, lb_l, cs_tab, n_lat_tiles):
    B, NT, D = X.shape
    nt = NT // TM
    tok = lambda w: pl.BlockSpec((1, TM, w), lambda b, i: (b, i, 0))
    full = lambda a: pl.BlockSpec(a.shape, lambda b, i: (0,) * a.ndim)
    out_shapes = (
        jax.ShapeDtypeStruct((B, NT, HG_WIDTH), BF16),
        jax.ShapeDtypeStruct((B, NT, HG_WIDTH), BF16),
        jax.ShapeDtypeStruct((B, NT, HG_WIDTH), BF16),
        jax.ShapeDtypeStruct((2, B, NT, HG_WIDTH), F32),
        jax.ShapeDtypeStruct((B, NT, DA_WIDTH), BF16),
        jax.ShapeDtypeStruct((B, NT, DA_WIDTH), BF16),
        jax.ShapeDtypeStruct((B, NT, DA_WIDTH), BF16),
        jax.ShapeDtypeStruct((2, NT, B * FT_WIDTH), BF16),
    )
    out_specs = (
        tok(HG_WIDTH), tok(HG_WIDTH), tok(HG_WIDTH),
        pl.BlockSpec((2, 1, TM, HG_WIDTH), lambda b, i: (0, b, i, 0)),
        tok(DA_WIDTH), tok(DA_WIDTH), tok(DA_WIDTH),
        pl.BlockSpec((2, TM, FT_WIDTH), lambda b, i: (0, i, b)),
    )
    return pl.pallas_call(
        _inproj_kernel,
        grid=(B, nt),
        in_specs=[
            tok(D),
            pl.BlockSpec((1, N_MOD, D), lambda b, i: (jnp.where(i < n_lat_tiles, b, B), 0, 0)),
            full(g_l), full(w_in_l),
            pl.BlockSpec((TM, DA_WIDTH), lambda b, i: (i, 0)),
            pl.BlockSpec((TM, DA_WIDTH), lambda b, i: (i, 0)),
            full(lb_l), full(cs_tab),
        ],
        out_specs=out_specs,
        out_shape=out_shapes,
        compiler_params=_cparams(("arbitrary", "arbitrary")),
        name="adaln_inproj",
    )(X, mod_l, g_l, w_in_l, cos_t, sin_t, lb_l, cs_tab)


def _hgrn_direction(q, k, v, g, st, cmat, consts, rev):
    C = q.shape[0]
    row8, sel_masks, block_masks = consts
    g_hi = g.astype(BF16)
    g_lo = (g - g_hi.astype(F32)).astype(BF16)
    bc = (jnp.dot(cmat, g_hi, preferred_element_type=F32)
          + jnp.dot(cmat, g_lo, preferred_element_type=F32))
    tot = bc[0:1, :] if rev else bc[C - 1:C, :]

    nt_dims = (((1,), (1,)), ((), ()))
    tn_dims = (((0,), (0,)), ((), ()))
    qd = (q * jnp.exp(bc)).astype(BF16)
    o = lax.dot_general(qd, st.astype(BF16), nt_dims, preferred_element_type=F32)
    kdec = (k * jnp.exp(tot - bc)).astype(BF16)
    st_new = st * jnp.exp(tot) + lax.dot_general(v, kdec, tn_dims, preferred_element_type=F32)

    a = None
    for d8 in range(SUB):
        if d8 == 0:
            term = q * k
        else:
            shift = (C - d8) if rev else d8
            ks = pltpu.roll(k, shift, 0)
            bs = pltpu.roll(bc, shift, 0)
            valid = (row8 <= SUB - 1 - d8) if rev else (row8 >= d8)
            term = jnp.where(valid, q * ks * jnp.exp(bc - bs), 0.0)
        red = jnp.sum(term, axis=-1, keepdims=True)
        contrib = jnp.where(sel_masks[d8], red, 0.0)
        a = contrib if a is None else a + contrib

    zeros8 = jnp.zeros((SUB, q.shape[1]), F32)
    b = SUB
    li = 0
    while b < C:
        ql, kl = [], []
        for r in range(0, C, SUB):
            base = (r // (2 * b)) * 2 * b
            upper = (r - base) >= b
            ref_row = base + b if rev else base + b - 1
            q_side = (not upper) if rev else upper
            bref = bc[ref_row:ref_row + 1, :]
            bg = bc[r:r + SUB, :]
            if q_side:
                ql.append(q[r:r + SUB, :] * jnp.exp(bg - bref))
                kl.append(zeros8)
            else:
                kl.append(k[r:r + SUB, :] * jnp.exp(bref - bg))
                ql.append(zeros8)
        qlb = jnp.concatenate(ql, axis=0).astype(BF16)
        klb = jnp.concatenate(kl, axis=0).astype(BF16)
        p = lax.dot_general(qlb, klb, nt_dims, preferred_element_type=F32)
        a = a + jnp.where(block_masks[li], p, 0.0)
        b *= 2
        li += 1

    o = o + jnp.dot(a.astype(BF16), v, preferred_element_type=F32)
    return o, st_new


def _hgrn_consts(C, rev):
    row = lax.broadcasted_iota(jnp.int32, (C, C), 0)
    col = lax.broadcasted_iota(jnp.int32, (C, C), 1)
    cmat = jnp.where((col >= row) if rev else (col <= row), 1.0, 0.0).astype(BF16)
    row8 = lax.broadcasted_iota(jnp.int32, (C, HG_DIM), 0) & (SUB - 1)
    sel_masks = [(col == row + d8) if rev else (col == row - d8) for d8 in range(SUB)]
    block_masks = []
    b = SUB
    while b < C:
        shift = int(math.log2(2 * b))
        block_masks.append((row >> shift) == (col >> shift))
        b *= 2
    return cmat, (row8, sel_masks, block_masks)


def _hgrn_kernel(q_ref, v_ref, lf_ref, o_ref, st_ref, *, n_lat, n_ctx):
    C = CH
    n_chunks = n_lat + n_ctx
    st_ref[...] = jnp.zeros(st_ref.shape, F32)
    dir_consts = [_hgrn_consts(C, rev) for rev in (False, True)]

    def body(j, carry):
        for d, rev in enumerate((False, True)):
            c = (n_chunks - 1 - j) if rev else jnp.where(j < n_ctx, n_lat + j, j - n_ctx)
            r0 = pl.multiple_of(c * C, C)
            q = q_ref[0, pl.ds(r0, C), :].astype(F32)
            v = v_ref[0, pl.ds(r0, C), :]
            g = lf_ref[d, 0, pl.ds(r0, C), :]
            k = 1.0 - jnp.exp(g)
            cmat, consts = dir_consts[d]
            o, st_new = _hgrn_direction(q, k, v, g, st_ref[d], cmat, consts, rev)
            st_ref[d] = st_new
            o_ref[d, 0, pl.ds(r0, C), :] = o
        return carry

    lax.fori_loop(0, n_chunks, body, 0)


def _hgrn(q, v, lf, n_lat_rows):
    B, NT, _ = q.shape
    n_lat = n_lat_rows // CH
    n_ctx = (NT - n_lat_rows) // CH
    head = pl.BlockSpec((1, NT, HG_DIM), lambda b, h: (b, 0, h))
    both = pl.BlockSpec((2, 1, NT, HG_DIM), lambda b, h: (0, b, 0, h))
    return pl.pallas_call(
        functools.partial(_hgrn_kernel, n_lat=n_lat, n_ctx=n_ctx),
        grid=(B, HG_HEADS),
        in_specs=[head, head, both],
        out_specs=both,
        out_shape=jax.ShapeDtypeStruct((2, B, NT, HG_WIDTH), F32),
        scratch_shapes=[pltpu.VMEM((2, HG_DIM, HG_DIM), F32)],
        compiler_params=_cparams(("arbitrary", "arbitrary")),
        name="hgrn_scan",
    )(q, v, lf)


def _attn_kernel(qT_ref, k_ref, vT_ref, lam_ref, sub_ref, o_ref, *, n_lat_tiles, n_lat_chunks,
                 n_chunks, out_scale):
    qi = pl.program_id(2)
    lo = jnp.where(qi < n_lat_tiles, 0, n_lat_chunks)
    neg = jnp.full((1, TQ), -1e30, F32)
    zero = jnp.zeros((VT_ROWS, TQ), F32)

    def body(c, carry):
        r0 = pl.multiple_of(c * KC, KC)
        vt = vT_ref[0, 0, :, pl.ds(r0, KC)]
        out = []
        for mp in range(2):
            m, acc = carry[mp]
            s = jnp.dot(k_ref[0, 0, mp, pl.ds(r0, KC), :], qT_ref[0, 0, mp],
                        preferred_element_type=F32)
            mn = jnp.maximum(m, jnp.max(s, axis=0, keepdims=True))
            p = jnp.exp2(s - mn).astype(BF16)
            acc = jnp.exp2(m - mn) * acc + jnp.dot(vt, p, preferred_element_type=F32)
            out.append((mn, acc))
        return tuple(out)

    (_, acc0), (_, acc1) = lax.fori_loop(lo, n_chunks, body, ((neg, zero), (neg, zero)))
    o = (acc0[0:DA_V] / acc0[DA_V:DA_V + 1]
         - lam_ref[...] * (acc1[0:DA_V] / acc1[DA_V:DA_V + 1]))
    y = o * lax.rsqrt(jnp.mean(o * o, axis=0, keepdims=True) + EPS)
    o_ref[0, 0] = (y * sub_ref[...] * out_scale).astype(BF16)


def _attention(dq, dk, dv, lam_l, subln_l, n_lat_rows, lam_init):
    B, NT, _ = dq.shape
    qT = dq.reshape(B, NT, DA_HEADS, 2, DA_QK).transpose(0, 2, 3, 4, 1)
    k4 = dk.reshape(B, NT, DA_HEADS, 2, DA_QK).transpose(0, 2, 3, 1, 4)
    vT = dv.reshape(B, NT, DA_HEADS, DA_V).transpose(0, 2, 3, 1)
    pad = np.zeros((VT_ROWS - DA_V, 1), np.float32)
    pad[0, 0] = 1.0
    vT = jnp.concatenate(
        [vT, jnp.broadcast_to(jnp.asarray(pad, BF16)[None, None], (B, DA_HEADS, VT_ROWS - DA_V, NT))],
        axis=2)
    kern = functools.partial(
        _attn_kernel, n_lat_tiles=n_lat_rows // TQ, n_lat_chunks=n_lat_rows // KC,
        n_chunks=NT // KC, out_scale=1.0 - lam_init)
    oT = pl.pallas_call(
        kern,
        grid=(B, DA_HEADS, NT // TQ),
        in_specs=[
            pl.BlockSpec((1, 1, 2, DA_QK, TQ), lambda b, h, i: (b, h, 0, 0, i)),
            pl.BlockSpec((1, 1, 2, NT, DA_QK), lambda b, h, i: (b, h, 0, 0, 0)),
            pl.BlockSpec((1, 1, VT_ROWS, NT), lambda b, h, i: (b, h, 0, 0)),
            pl.BlockSpec((1, TQ), lambda b, h, i: (0, 0)),
            pl.BlockSpec((DA_V, 1), lambda b, h, i: (0, 0)),
        ],
        out_specs=pl.BlockSpec((1, 1, DA_V, TQ), lambda b, h, i: (b, h, 0, i)),
        out_shape=jax.ShapeDtypeStruct((B, DA_HEADS, DA_V, NT), BF16),
        compiler_params=_cparams(("arbitrary", "arbitrary", "arbitrary")),
        name="diff_attention",
    )(qT, k4, vT, lam_l, subln_l)
    return oT.transpose(0, 3, 1, 2).reshape(B, NT, DA_WIDTH)


def _fourier_kernel(ct_ref, st_ref, f_ref, o_ref):
    o_ref[...] = (jnp.dot(ct_ref[...], f_ref[0], preferred_element_type=F32)
                  + jnp.dot(st_ref[...], f_ref[1], preferred_element_type=F32)).astype(BF16)


def _fourier(ct, st_neg, f, row_block, n_rows):
    W = f.shape[2]
    tk = min(TK, n_rows)
    return pl.pallas_call(
        _fourier_kernel,
        grid=(n_rows // tk,),
        in_specs=[
            pl.BlockSpec((tk, n_rows), lambda i: (i, 0)),
            pl.BlockSpec((tk, n_rows), lambda i: (i, 0)),
            pl.BlockSpec((2, n_rows, W), lambda i: (0, row_block, 0)),
        ],
        out_specs=pl.BlockSpec((tk, W), lambda i: (i, 0)),
        out_shape=jax.ShapeDtypeStruct((n_rows, W), BF16),
        compiler_params=_cparams(("arbitrary",)),
        name="fourier_mix",
    )(ct, st_neg, f)


def _post_kernel(x_ref, mod_ref, g_ref, of_ref, gate_ref, da_ref, ftl_ref, ftc_ref, onorm_ref,
                 wo_ref, wfi_ref, wfo_ref, o_ref, *, n_lat_tiles):
    x = x_ref[0]
    mod = mod_ref[0]
    ft = jnp.where(pl.program_id(1) < n_lat_tiles, ftl_ref[...], ftc_ref[...])
    o = of_ref[0, 0] + of_ref[1, 0]
    onorm = onorm_ref[...]
    heads = [_rms(o[:, h * HG_DIM:(h + 1) * HG_DIM]) * onorm for h in range(HG_HEADS)]
    hg = (jnp.concatenate(heads, axis=-1) * gate_ref[0].astype(F32)).astype(BF16)
    y = (jnp.dot(hg, wo_ref[0:HG_WIDTH, :], preferred_element_type=F32)
         + jnp.dot(da_ref[0], wo_ref[HG_WIDTH:HG_WIDTH + DA_WIDTH, :], preferred_element_type=F32)
         + jnp.dot(ft, wo_ref[HG_WIDTH + DA_WIDTH:, :], preferred_element_type=F32))
    x1 = x + mod[2:3, :] * (_rms(y) * g_ref[1:2, :])
    h2 = ((_rms(x1) * g_ref[2:3, :]) * (1.0 + mod[4:5, :]) + mod[3:4, :]).astype(BF16)
    gu = jnp.dot(h2, wfi_ref[...], preferred_element_type=F32)
    act = (_silu(gu[:, :FF_HIDDEN]) * gu[:, FF_HIDDEN:]).astype(BF16)
    y2 = jnp.dot(act, wfo_ref[...], preferred_element_type=F32)
    o_ref[0] = x1 + mod[5:6, :] * (_rms(y2) * g_ref[3:4, :])


def _post(X, mod_l, g_l, of, gate, da, ft_lat, ft_ctx, onorm_l, wo, wfi, wfo, n_lat_tiles):
    B, NT, D = X.shape
    tok = lambda w: pl.BlockSpec((1, TM, w), lambda b, i: (b, i, 0))
    const = lambda a: pl.BlockSpec(a.shape, lambda b, i: (0,) * a.ndim, pipeline_mode=pl.Buffered(1))
    return pl.pallas_call(
        functools.partial(_post_kernel, n_lat_tiles=n_lat_tiles),
        grid=(B, NT // TM),
        in_specs=[
            tok(D),
            pl.BlockSpec((1, N_MOD, D), lambda b, i: (jnp.where(i < n_lat_tiles, b, B), 0, 0)),
            const(g_l),
            pl.BlockSpec((2, 1, TM, HG_WIDTH), lambda b, i: (0, b, i, 0)),
            tok(HG_WIDTH), tok(DA_WIDTH),
            pl.BlockSpec((TM, FT_WIDTH), lambda b, i: (jnp.minimum(i, n_lat_tiles - 1), b)),
            pl.BlockSpec((TM, FT_WIDTH), lambda b, i: (jnp.maximum(i - n_lat_tiles, 0), b)),
            const(onorm_l), const(wo), const(wfi), const(wfo),
        ],
        out_specs=tok(D),
        out_shape=jax.ShapeDtypeStruct((B, NT, D), F32),
        compiler_params=_cparams(("arbitrary", "arbitrary")),
        name="outproj_ffn",
    )(X, mod_l, g_l, of, gate, da, ft_lat, ft_ctx, onorm_l, wo, wfi, wfo)


def _rope_tables(t_lat, t_ctx):
    pos = jnp.arange(t_lat)
    inv_freq = 1.0 / (ROPE_THETA ** (jnp.arange(0, ROPE_AXIS_DIM, 2, dtype=F32) / ROPE_AXIS_DIM))
    ang = jnp.stack([pos // GRID_W, pos % GRID_W], axis=-1).astype(F32)[:, :, None] * inv_freq
    cos, sin = jnp.cos(ang), jnp.sin(ang)
    cos32 = jnp.stack([cos, cos], axis=2).reshape(t_lat, DA_QK)
    sin32 = jnp.stack([-sin, sin], axis=2).reshape(t_lat, DA_QK)
    reps = DA_WIDTH // DA_QK
    cos_t = jnp.concatenate([jnp.tile(cos32, (1, reps)), jnp.ones((t_ctx, DA_WIDTH), F32)], axis=0)
    sin_t = jnp.concatenate([jnp.tile(sin32, (1, reps)), jnp.zeros((t_ctx, DA_WIDTH), F32)], axis=0)
    return cos_t, sin_t


def _channel_table():
    idx = np.arange(FT_GDIM)
    ang = 2.0 * np.pi * ((idx[:, None] * idx[None, :]) % FT_GDIM) / FT_GDIM
    eye = np.eye(FT_GROUPS)
    return jnp.asarray(np.concatenate([np.kron(eye, np.cos(ang)), np.kron(eye, np.sin(ang))], axis=1), BF16)


def _position_tables(n):
    lo = 1
    while lo * lo < n:
        lo *= 2
    hi = n // lo
    k = jnp.arange(n)
    a1 = 2.0 * np.pi * ((k[:, None] * jnp.arange(hi)[None, :]) % hi).astype(F32) / hi
    a2 = 2.0 * np.pi * ((k[:, None] * jnp.arange(lo)[None, :]) % n).astype(F32) / n
    c1, s1 = jnp.cos(a1)[:, :, None], jnp.sin(a1)[:, :, None]
    c2, s2 = jnp.cos(a2)[:, None, :], jnp.sin(a2)[:, None, :]
    scale = 1.0 / math.sqrt(n * FT_GDIM)
    ct = ((c1 * c2 - s1 * s2) * scale).reshape(n, n).astype(BF16)
    st_neg = ((s1 * c2 + c1 * s2) * (-scale)).reshape(n, n).astype(BF16)
    return ct, st_neg


def kernel(x, c, ctx, c_ctx, w_mod, b_mod, norm_g, w_in, w_out, hg_lb_logits, hg_onorm,
           da_lambda, da_subln, w_ffn_in, w_ffn_out):
    B, T, D = x.shape
    Tc = ctx.shape[1]
    depth = w_mod.shape[0]
    assert D == D_MODEL and B + 1 <= MOD_ROWS
    assert T % TM == 0 and Tc % TM == 0 and T % KC == 0 and Tc % KC == 0 and T % CH == 0 and Tc % CH == 0
    n_lat_tiles = T // TM

    lam_init = [0.8 - 0.6 * math.exp(-0.3 * l) for l in range(depth)]
    lb, lam = _prep(hg_lb_logits, da_lambda, lam_init)

    cond = jnp.concatenate([c.astype(F32), c_ctx.astype(F32)[None, :],
                            jnp.zeros((MOD_ROWS - B - 1, D), F32)], axis=0)
    mods = _modulation(cond, w_mod, b_mod).reshape(depth, MOD_ROWS, N_MOD, D)

    cos_t, sin_t = _rope_tables(T, Tc)
    cs_tab = _channel_table()
    ct_lat, st_lat = _position_tables(T)
    ct_ctx, st_ctx = _position_tables(Tc)

    X = jnp.concatenate([x, ctx], axis=1).astype(F32)
    for l in range(depth):
        mod_l = mods[l]
        g_l = norm_g[l].astype(F32)
        q, vi, gate, lf, dq, dk, dv, f = _inproj(
            X, mod_l, g_l, w_in[l].astype(BF16), cos_t, sin_t, lb[l], cs_tab, n_lat_tiles)
        of = _hgrn(q, vi, lf, T)
        da = _attention(dq, dk, dv, lam[l:l + 1], da_subln[l].astype(F32).reshape(DA_V, 1), T, lam_init[l])
        ft_lat = _fourier(ct_lat, st_lat, f, 0, T)
        ft_ctx = _fourier(ct_ctx, st_ctx, f, T // Tc, Tc)
        X = _post(X, mod_l, g_l, of, gate, da, ft_lat, ft_ctx, hg_onorm[l].astype(F32).reshape(1, HG_DIM),
                  w_out[l].astype(BF16), w_ffn_in[l].astype(BF16), w_ffn_out[l].astype(BF16), n_lat_tiles)
    return X[:, :T].astype(x.dtype)
```

```python
import functools
import math

import numpy as np
import jax
import jax.numpy as jnp
from jax import lax
from jax.experimental import pallas as pl
from jax.experimental.pallas import tpu as pltpu

F32 = jnp.float32
BF16 = jnp.bfloat16

D_MODEL = 1024
GRID_W = 64
HG_WIDTH = 512
HG_DIM = 128
HG_HEADS = 4
DA_WIDTH = 256
DA_HEADS = 4
DA_V = 64
DA_QK = 32
FT_WIDTH = 256
FT_GROUPS = 4
FT_GDIM = FT_WIDTH // FT_GROUPS
FF_HIDDEN = 2816
N_MOD = 6
EPS = 1e-6
ROPE_THETA = 10000.0
ROPE_AXIS_DIM = DA_QK // 2

_OFF_Q, _OFF_I, _OFF_G, _OFF_F, _OFF_DQ, _OFF_DK, _OFF_DV, _OFF_FT, IN_WIDTH = (
    0, 512, 1024, 1536, 2560, 2816, 3072, 3328, 3584)

TM = 256
TQ = 256
KC = 256
CH = 64
SUB = 8
HGRN_UNROLL = 4
TK = 256
VT_ROWS = 80
MOD_ROWS = 8
MOD_TN = 1536
VMEM_LIMIT = 52 * 1024 * 1024

Q_SCALE = (DA_QK ** -0.5) * math.log2(math.e)


def _silu(x):
    return x * jax.nn.sigmoid(x)


def _rms(x):
    return x * lax.rsqrt(jnp.mean(x * x, axis=-1, keepdims=True) + EPS)


def _cparams(sem):
    return pltpu.CompilerParams(dimension_semantics=sem, vmem_limit_bytes=VMEM_LIMIT)


def _mod_kernel(c_ref, w_ref, b_ref, o_ref):
    a = _silu(c_ref[...]).astype(BF16)
    w = w_ref[0].astype(BF16)
    o_ref[0] = jnp.dot(a, w, preferred_element_type=F32) + b_ref[0]


def _modulation(cond, w_mod, b_mod):
    depth, d, n = w_mod.shape
    return pl.pallas_call(
        _mod_kernel,
        grid=(depth, n // MOD_TN),
        in_specs=[
            pl.BlockSpec((MOD_ROWS, d), lambda l, j: (0, 0)),
            pl.BlockSpec((1, d, MOD_TN), lambda l, j: (l, 0, j)),
            pl.BlockSpec((1, 1, MOD_TN), lambda l, j: (l, 0, j)),
        ],
        out_specs=pl.BlockSpec((1, MOD_ROWS, MOD_TN), lambda l, j: (l, 0, j)),
        out_shape=jax.ShapeDtypeStruct((depth, MOD_ROWS, n), F32),
        compiler_params=_cparams(("arbitrary", "arbitrary")),
        name="modulation",
    )(cond, w_mod, b_mod.reshape(depth, 1, n))


def _prep_kernel(lb_ref, lam_ref, lam_init_ref, lbo_ref, lamo_ref, *, depth):
    rows = [lb_ref[l:l + 1, :] for l in range(depth)]
    m = rows[0]
    for r in rows[1:]:
        m = jnp.maximum(m, r)
    e = [jnp.exp(r - m) for r in rows]
    tot = e[0]
    for r in e[1:]:
        tot = tot + r
    p = [r / tot for r in e]
    acc = p[0]
    lbo_ref[0:1, :] = acc - p[0]
    for l in range(1, depth):
        acc = acc + p[l]
        lbo_ref[l:l + 1, :] = acc - p[0]
    x = lam_ref[...]
    a = jnp.sum(x[:, 0:DA_QK] * x[:, DA_QK:2 * DA_QK], axis=-1, keepdims=True)
    b = jnp.sum(x[:, 2 * DA_QK:3 * DA_QK] * x[:, 3 * DA_QK:4 * DA_QK], axis=-1, keepdims=True)
    lamo_ref[...] = jnp.exp(a) - jnp.exp(b) + lam_init_ref[...]


def _prep(hg_lb_logits, da_lambda, lam_init):
    depth = hg_lb_logits.shape[1]
    lb_in = jnp.transpose(hg_lb_logits.astype(F32), (1, 0, 2)).reshape(depth, 2 * HG_WIDTH)
    lam_in = da_lambda.astype(F32).reshape(depth, 4 * DA_QK)
    lam_init_arr = jnp.asarray(np.broadcast_to(np.asarray(lam_init, np.float32)[:, None], (depth, TQ)))
    lb, lam = pl.pallas_call(
        functools.partial(_prep_kernel, depth=depth),
        out_shape=(jax.ShapeDtypeStruct((depth, 2 * HG_WIDTH), F32),
                   jax.ShapeDtypeStruct((depth, TQ), F32)),
        name="param_prep",
    )(lb_in, lam_in, lam_init_arr)
    return lb.reshape(depth, 2, HG_WIDTH), lam


def _inproj_kernel(x_ref, mod_ref, g_ref, w_ref, cos_ref, sin_ref, lb_ref, cs_ref,
                   q_ref, i_ref, gate_ref, lf_ref, dq_ref, dk_ref, dv_ref, f_ref):
    x = x_ref[0]
    mod = mod_ref[0]
    h = (_rms(x) * g_ref[0:1, :]) * (1.0 + mod[1:2, :]) + mod[0:1, :]
    hb = h.astype(BF16)

    def proj(a, b):
        return jnp.dot(hb, w_ref[:, a:b], preferred_element_type=F32)

    q_ref[0] = _silu(proj(_OFF_Q, _OFF_I)).astype(BF16)
    i_ref[0] = proj(_OFF_I, _OFF_G).astype(BF16)
    gate_ref[0] = _silu(proj(_OFF_G, _OFF_F)).astype(BF16)
    for d in range(2):
        z = proj(_OFF_F + d * HG_WIDTH, _OFF_F + (d + 1) * HG_WIDTH)
        lb = lb_ref[d:d + 1, :]
        lf_ref[d, 0] = jnp.log(lb + (1.0 - lb) * jax.nn.sigmoid(z))

    cos = cos_ref[...]
    sin = sin_ref[...]
    lane = lax.broadcasted_iota(jnp.int32, cos.shape, 1)
    upper_half = (lane & (ROPE_AXIS_DIM // 2)) != 0

    def rope(t):
        partner = jnp.where(upper_half,
                            pltpu.roll(t, ROPE_AXIS_DIM // 2, 1),
                            pltpu.roll(t, 2 * DA_HEADS * DA_QK - ROPE_AXIS_DIM // 2, 1))
        return t * cos + partner * sin

    dq_ref[0] = (rope(proj(_OFF_DQ, _OFF_DK)) * Q_SCALE).astype(BF16)
    dk_ref[0] = rope(proj(_OFF_DK, _OFF_DV)).astype(BF16)
    dv_ref[0] = proj(_OFF_DV, _OFF_FT).astype(BF16)
    ft = proj(_OFF_FT, IN_WIDTH).astype(BF16)
    cs = jnp.dot(ft, cs_ref[...], preferred_element_type=F32)
    f_ref[0] = cs[:, :FT_WIDTH].astype(BF16)
    f_ref[1] = cs[:, FT_WIDTH:].astype(BF16)


def _inproj(X, mod_l, g_l, w_in_l, cos_t, sin_t, lb_l, cs_tab, n_lat_tiles):
    B, NT, D = X.shape
    nt = NT // TM
    tok = lambda w: pl.BlockSpec((1, TM, w), lambda b, i: (b, i, 0))
    full = lambda a: pl.BlockSpec(a.shape, lambda b, i: (0,) * a.ndim)
    out_shapes = (
        jax.ShapeDtypeStruct((B, NT, HG_WIDTH), BF16),
        jax.ShapeDtypeStruct((B, NT, HG_WIDTH), BF16),
        jax.ShapeDtypeStruct((B, NT, HG_WIDTH), BF16),
        jax.ShapeDtypeStruct((2, B, NT, HG_WIDTH), F32),
        jax.ShapeDtypeStruct((B, NT, DA_WIDTH), BF16),
        jax.ShapeDtypeStruct((B, NT, DA_WIDTH), BF16),
        jax.ShapeDtypeStruct((B, NT, DA_WIDTH), BF16),
        jax.ShapeDtypeStruct((2, NT, B * FT_WIDTH), BF16),
    )
    out_specs = (
        tok(HG_WIDTH), tok(HG_WIDTH), tok(HG_WIDTH),
        pl.BlockSpec((2, 1, TM, HG_WIDTH), lambda b, i: (0, b, i, 0)),
        tok(DA_WIDTH), tok(DA_WIDTH), tok(DA_WIDTH),
        pl.BlockSpec((2, TM, FT_WIDTH), lambda b, i: (0, i, b)),
    )
    return pl.pallas_call(
        _inproj_kernel,
        grid=(B, nt),
        in_specs=[
            tok(D),
            pl.BlockSpec((1, N_MOD, D), lambda b, i: (jnp.where(i < n_lat_tiles, b, B), 0, 0)),
            full(g_l), full(w_in_l),
            pl.BlockSpec((TM, DA_WIDTH), lambda b, i: (i, 0)),
            pl.BlockSpec((TM, DA_WIDTH), lambda b, i: (i, 0)),
            full(lb_l), full(cs_tab),
        ],
        out_specs=out_specs,
        out_shape=out_shapes,
        compiler_params=_cparams(("arbitrary", "arbitrary")),
        name="adaln_inproj",
    )(X, mod_l, g_l, w_in_l, cos_t, sin_t, lb_l, cs_tab)


def _hgrn_direction(q, f, v, g, st, cmat, codes, rev):
    C, K = q.shape
    k = 1.0 - f
    g_hi = g.astype(BF16)
    g_lo = (g - g_hi.astype(F32)).astype(BF16)
    cs = jnp.dot(cmat, jnp.concatenate([g_hi, g_lo], axis=1), preferred_element_type=F32)
    bc = cs[:, :K] + cs[:, K:]
    tot = bc[0:1, :] if rev else bc[C - 1:C, :]

    nt_dims = (((1,), (1,)), ((), ()))
    tn_dims = (((0,), (0,)), ((), ()))
    qd = (q * jnp.exp(bc)).astype(BF16)
    o = lax.dot_general(qd, st.astype(BF16), nt_dims, preferred_element_type=F32)
    kdec = (k * jnp.exp(tot - bc)).astype(BF16)
    st_new = st * jnp.exp(tot) + lax.dot_general(v, kdec, tn_dims, preferred_element_type=F32)

    a = jnp.zeros((C, C), F32)
    f3 = f.reshape(C // SUB, SUB, K)
    f_shift = f
    dec = None
    for d8 in range(SUB):
        if d8 == 0:
            term = q * k
        else:
            dec = f_shift if dec is None else dec * f_shift
            f_shift = pltpu.roll(f3, (SUB - d8) if rev else d8, 1).reshape(C, K)
            term = q * (1.0 - f_shift) * dec
        red = jnp.sum(term, axis=-1, keepdims=True)
        a = jnp.where(codes == d8, red, a)

    zeros8 = jnp.zeros((SUB, K), F32)
    b = SUB
    level = 0
    while b < C:
        ql, kl = [], []
        for r in range(0, C, SUB):
            base = (r // (2 * b)) * 2 * b
            upper = (r - base) >= b
            ref_row = base + b if rev else base + b - 1
            q_side = (not upper) if rev else upper
            bref = bc[ref_row:ref_row + 1, :]
            bg = bc[r:r + SUB, :]
            if q_side:
                ql.append(q[r:r + SUB, :] * jnp.exp(bg - bref))
                kl.append(zeros8)
            else:
                kl.append(k[r:r + SUB, :] * jnp.exp(bref - bg))
                ql.append(zeros8)
        qlb = jnp.concatenate(ql, axis=0).astype(BF16)
        klb = jnp.concatenate(kl, axis=0).astype(BF16)
        p = lax.dot_general(qlb, klb, nt_dims, preferred_element_type=F32)
        a = jnp.where(codes == SUB + level, p, a)
        b *= 2
        level += 1

    o = o + jnp.dot(a.astype(BF16), v, preferred_element_type=F32)
    return o, st_new


def _hgrn_tables(C):
    t = np.arange(C)[:, None]
    s = np.arange(C)[None, :]
    cmats, codes = [], []
    for rev in (False, True):
        d = (s - t) if rev else (t - s)
        cmats.append((d >= 0).astype(np.float32))
        code = np.full((C, C), -1, np.int32)
        inside = (d >= 0) & ((t // SUB) == (s // SUB))
        code[inside] = d[inside]
        b, level = SUB, 0
        while b < C:
            split = (d > 0) & ((t // (2 * b)) == (s // (2 * b))) & ((t // b) != (s // b))
            code[split] = SUB + level
            b *= 2
            level += 1
        codes.append(code)
    return jnp.asarray(np.stack(cmats), BF16), jnp.asarray(np.stack(codes), jnp.int32)


def _hgrn_kernel(q_ref, v_ref, lf_ref, cmat_ref, code_ref, o_ref, st_ref, *, n_lat, n_ctx):
    C = CH
    n_chunks = n_lat + n_ctx
    st_ref[...] = jnp.zeros(st_ref.shape, F32)

    def body(j, carry):
        for d, rev in enumerate((False, True)):
            c = (n_chunks - 1 - j) if rev else jnp.where(j < n_ctx, n_lat + j, j - n_ctx)
            r0 = pl.multiple_of(c * C, C)
            q = q_ref[0, pl.ds(r0, C), :].astype(F32)
            v = v_ref[0, pl.ds(r0, C), :]
            g = lf_ref[d, 0, pl.ds(r0, C), :]
            o, st_new = _hgrn_direction(q, jnp.exp(g), v, g, st_ref[d], cmat_ref[d], code_ref[d], rev)
            st_ref[d] = st_new
            o_ref[d, 0, pl.ds(r0, C), :] = o
        return carry

    lax.fori_loop(0, n_chunks, body, 0, unroll=HGRN_UNROLL)


def _hgrn(q, v, lf, n_lat_rows):
    B, NT, _ = q.shape
    n_lat = n_lat_rows // CH
    n_ctx = (NT - n_lat_rows) // CH
    assert (n_lat + n_ctx) % HGRN_UNROLL == 0
    cmat, codes = _hgrn_tables(CH)
    head = pl.BlockSpec((1, NT, HG_DIM), lambda b, h: (b, 0, h))
    both = pl.BlockSpec((2, 1, NT, HG_DIM), lambda b, h: (0, b, 0, h))
    table = pl.BlockSpec((2, CH, CH), lambda b, h: (0, 0, 0))
    return pl.pallas_call(
        functools.partial(_hgrn_kernel, n_lat=n_lat, n_ctx=n_ctx),
        grid=(B, HG_HEADS),
        in_specs=[head, head, both, table, table],
        out_specs=both,
        out_shape=jax.ShapeDtypeStruct((2, B, NT, HG_WIDTH), F32),
        scratch_shapes=[pltpu.VMEM((2, HG_DIM, HG_DIM), F32)],
        compiler_params=_cparams(("arbitrary", "arbitrary")),
        name="hgrn_scan",
    )(q, v, lf, cmat, codes)


def _attn_kernel(qT_ref, k_ref, vT_ref, lam_ref, sub_ref, o_ref, s_ref, *, n_lat_tiles, n_lat_chunks,
                 n_chunks, out_scale):
    qi = pl.program_id(2)

    def attend(chunks):
        n = len(chunks)
        accs = []
        for mp in range(2):
            m8 = None
            for i, c in enumerate(chunks):
                s = jnp.dot(k_ref[0, 0, mp, c * KC:(c + 1) * KC, :], qT_ref[0, 0, mp],
                            preferred_element_type=F32)
                s_ref[mp, i * KC:(i + 1) * KC, :] = s
                parts = [s[r:r + 8, :] for r in range(0, KC, 8)]
                while len(parts) > 1:
                    parts = [jnp.maximum(parts[j], parts[j + 1]) for j in range(0, len(parts), 2)]
                m8 = parts[0] if m8 is None else jnp.maximum(m8, parts[0])
            m = jnp.max(m8, axis=0, keepdims=True)
            p = jnp.exp2(s_ref[mp, 0:n * KC, :] - m).astype(BF16)
            vt = vT_ref[0, 0, :, chunks[0] * KC:(chunks[-1] + 1) * KC]
            accs.append(jnp.dot(vt, p, preferred_element_type=F32))
        acc0, acc1 = accs
        o = (acc0[0:DA_V] / acc0[DA_V:DA_V + 1]
             - lam_ref[...] * (acc1[0:DA_V] / acc1[DA_V:DA_V + 1]))
        y = o * lax.rsqrt(jnp.mean(o * o, axis=0, keepdims=True) + EPS)
        o_ref[0, 0] = (y * sub_ref[...] * out_scale).astype(BF16)

    @pl.when(qi < n_lat_tiles)
    def _():
        attend(range(n_chunks))

    @pl.when(qi >= n_lat_tiles)
    def _():
        attend(range(n_lat_chunks, n_chunks))


def _attention(dq, dk, dv, lam_l, subln_l, n_lat_rows, lam_init):
    B, NT, _ = dq.shape
    qT = dq.reshape(B, NT, DA_HEADS, 2, DA_QK).transpose(0, 2, 3, 4, 1)
    k4 = dk.reshape(B, NT, DA_HEADS, 2, DA_QK).transpose(0, 2, 3, 1, 4)
    vT = dv.reshape(B, NT, DA_HEADS, DA_V).transpose(0, 2, 3, 1)
    pad = np.zeros((VT_ROWS - DA_V, 1), np.float32)
    pad[0, 0] = 1.0
    vT = jnp.concatenate(
        [vT, jnp.broadcast_to(jnp.asarray(pad, BF16)[None, None], (B, DA_HEADS, VT_ROWS - DA_V, NT))],
        axis=2)
    kern = functools.partial(
        _attn_kernel, n_lat_tiles=n_lat_rows // TQ, n_lat_chunks=n_lat_rows // KC,
        n_chunks=NT // KC, out_scale=1.0 - lam_init)
    oT = pl.pallas_call(
        kern,
        grid=(B, DA_HEADS, NT // TQ),
        in_specs=[
            pl.BlockSpec((1, 1, 2, DA_QK, TQ), lambda b, h, i: (b, h, 0, 0, i)),
            pl.BlockSpec((1, 1, 2, NT, DA_QK), lambda b, h, i: (b, h, 0, 0, 0)),
            pl.BlockSpec((1, 1, VT_ROWS, NT), lambda b, h, i: (b, h, 0, 0)),
            pl.BlockSpec((1, TQ), lambda b, h, i: (0, 0)),
            pl.BlockSpec((DA_V, 1), lambda b, h, i: (0, 0)),
        ],
        out_specs=pl.BlockSpec((1, 1, DA_V, TQ), lambda b, h, i: (b, h, 0, i)),
        out_shape=jax.ShapeDtypeStruct((B, DA_HEADS, DA_V, NT), BF16),
        scratch_shapes=[pltpu.VMEM((2, NT, TQ), F32)],
        compiler_params=_cparams(("arbitrary", "arbitrary", "arbitrary")),
        name="diff_attention",
    )(qT, k4, vT, lam_l, subln_l)
    return oT.transpose(0, 3, 1, 2).reshape(B, NT, DA_WIDTH)


def _fourier_kernel(ct_ref, st_ref, f_ref, o_ref):
    o_ref[...] = (jnp.dot(ct_ref[...], f_ref[0], preferred_element_type=F32)
                  + jnp.dot(st_ref[...], f_ref[1], preferred_element_type=F32)).astype(BF16)


def _fourier(ct, st_neg, f, row_block, n_rows):
    W = f.shape[2]
    tk = min(TK, n_rows)
    return pl.pallas_call(
        _fourier_kernel,
        grid=(n_rows // tk,),
        in_specs=[
            pl.BlockSpec((tk, n_rows), lambda i: (i, 0)),
            pl.BlockSpec((tk, n_rows), lambda i: (i, 0)),
            pl.BlockSpec((2, n_rows, W), lambda i: (0, row_block, 0)),
        ],
        out_specs=pl.BlockSpec((tk, W), lambda i: (i, 0)),
        out_shape=jax.ShapeDtypeStruct((n_rows, W), BF16),
        compiler_params=_cparams(("arbitrary",)),
        name="fourier_mix",
    )(ct, st_neg, f)


def _post_kernel(x_ref, mod_ref, g_ref, of_ref, gate_ref, da_ref, ftl_ref, ftc_ref, onorm_ref,
                 wo_ref, wfi_ref, wfo_ref, o_ref, *, n_lat_tiles):
    x = x_ref[0]
    mod = mod_ref[0]
    ft = jnp.where(pl.program_id(1) < n_lat_tiles, ftl_ref[...], ftc_ref[...])
    o = of_ref[0, 0] + of_ref[1, 0]
    onorm = onorm_ref[...]
    heads = [_rms(o[:, h * HG_DIM:(h + 1) * HG_DIM]) * onorm for h in range(HG_HEADS)]
    hg = (jnp.concatenate(heads, axis=-1) * gate_ref[0].astype(F32)).astype(BF16)
    y = (jnp.dot(hg, wo_ref[0:HG_WIDTH, :], preferred_element_type=F32)
         + jnp.dot(da_ref[0], wo_ref[HG_WIDTH:HG_WIDTH + DA_WIDTH, :], preferred_element_type=F32)
         + jnp.dot(ft, wo_ref[HG_WIDTH + DA_WIDTH:, :], preferred_element_type=F32))
    x1 = x + mod[2:3, :] * (_rms(y) * g_ref[1:2, :])
    h2 = ((_rms(x1) * g_ref[2:3, :]) * (1.0 + mod[4:5, :]) + mod[3:4, :]).astype(BF16)
    gu = jnp.dot(h2, wfi_ref[...], preferred_element_type=F32)
    act = (_silu(gu[:, :FF_HIDDEN]) * gu[:, FF_HIDDEN:]).astype(BF16)
    y2 = jnp.dot(act, wfo_ref[...], preferred_element_type=F32)
    o_ref[0] = x1 + mod[5:6, :] * (_rms(y2) * g_ref[3:4, :])


def _post(X, mod_l, g_l, of, gate, da, ft_lat, ft_ctx, onorm_l, wo, wfi, wfo, n_lat_tiles):
    B, NT, D = X.shape
    tok = lambda w: pl.BlockSpec((1, TM, w), lambda b, i: (b, i, 0))
    const = lambda a: pl.BlockSpec(a.shape, lambda b, i: (0,) * a.ndim, pipeline_mode=pl.Buffered(1))
    return pl.pallas_call(
        functools.partial(_post_kernel, n_lat_tiles=n_lat_tiles),
        grid=(B, NT // TM),
        in_specs=[
            tok(D),
            pl.BlockSpec((1, N_MOD, D), lambda b, i: (jnp.where(i < n_lat_tiles, b, B), 0, 0)),
            const(g_l),
            pl.BlockSpec((2, 1, TM, HG_WIDTH), lambda b, i: (0, b, i, 0)),
            tok(HG_WIDTH), tok(DA_WIDTH),
            pl.BlockSpec((TM, FT_WIDTH), lambda b, i: (jnp.minimum(i, n_lat_tiles - 1), b)),
            pl.BlockSpec((TM, FT_WIDTH), lambda b, i: (jnp.maximum(i - n_lat_tiles, 0), b)),
            const(onorm_l), const(wo), const(wfi), const(wfo),
        ],
        out_specs=tok(D),
        out_shape=jax.ShapeDtypeStruct((B, NT, D), F32),
        compiler_params=_cparams(("arbitrary", "arbitrary")),
        name="outproj_ffn",
    )(X, mod_l, g_l, of, gate, da, ft_lat, ft_ctx, onorm_l, wo, wfi, wfo)


def _rope_tables(t_lat, t_ctx):
    pos = jnp.arange(t_lat)
    inv_freq = 1.0 / (ROPE_THETA ** (jnp.arange(0, ROPE_AXIS_DIM, 2, dtype=F32) / ROPE_AXIS_DIM))
    ang = jnp.stack([pos // GRID_W, pos % GRID_W], axis=-1).astype(F32)[:, :, None] * inv_freq
    cos, sin = jnp.cos(ang), jnp.sin(ang)
    cos32 = jnp.stack([cos, cos], axis=2).reshape(t_lat, DA_QK)
    sin32 = jnp.stack([-sin, sin], axis=2).reshape(t_lat, DA_QK)
    reps = DA_WIDTH // DA_QK
    cos_t = jnp.concatenate([jnp.tile(cos32, (1, reps)), jnp.ones((t_ctx, DA_WIDTH), F32)], axis=0)
    sin_t = jnp.concatenate([jnp.tile(sin32, (1, reps)), jnp.zeros((t_ctx, DA_WIDTH), F32)], axis=0)
    return cos_t, sin_t


def _channel_table():
    idx = np.arange(FT_GDIM)
    ang = 2.0 * np.pi * ((idx[:, None] * idx[None, :]) % FT_GDIM) / FT_GDIM
    eye = np.eye(FT_GROUPS)
    return jnp.asarray(np.concatenate([np.kron(eye, np.cos(ang)), np.kron(eye, np.sin(ang))], axis=1), BF16)


def _position_tables(n):
    lo = 1
    while lo * lo < n:
        lo *= 2
    hi = n // lo
    k = jnp.arange(n)
    a1 = 2.0 * np.pi * ((k[:, None] * jnp.arange(hi)[None, :]) % hi).astype(F32) / hi
    a2 = 2.0 * np.pi * ((k[:, None] * jnp.arange(lo)[None, :]) % n).astype(F32) / n
    c1, s1 = jnp.cos(a1)[:, :, None], jnp.sin(a1)[:, :, None]
    c2, s2 = jnp.cos(a2)[:, None, :], jnp.sin(a2)[:, None, :]
    scale = 1.0 / math.sqrt(n * FT_GDIM)
    ct = ((c1 * c2 - s1 * s2) * scale).reshape(n, n).astype(BF16)
    st_neg = ((s1 * c2 + c1 * s2) * (-scale)).reshape(n, n).astype(BF16)
    return ct, st_neg


def kernel(x, c, ctx, c_ctx, w_mod, b_mod, norm_g, w_in, w_out, hg_lb_logits, hg_onorm,
           da_lambda, da_subln, w_ffn_in, w_ffn_out):
    B, T, D = x.shape
    Tc = ctx.shape[1]
    depth = w_mod.shape[0]
    assert D == D_MODEL and B + 1 <= MOD_ROWS
    assert T % TM == 0 and Tc % TM == 0 and T % KC == 0 and Tc % KC == 0 and T % CH == 0 and Tc % CH == 0
    n_lat_tiles = T // TM

    lam_init = [0.8 - 0.6 * math.exp(-0.3 * l) for l in range(depth)]
    lb, lam = _prep(hg_lb_logits, da_lambda, lam_init)

    cond = jnp.concatenate([c.astype(F32), c_ctx.astype(F32)[None, :],
                            jnp.zeros((MOD_ROWS - B - 1, D), F32)], axis=0)
    mods = _modulation(cond, w_mod, b_mod).reshape(depth, MOD_ROWS, N_MOD, D)

    cos_t, sin_t = _rope_tables(T, Tc)
    cs_tab = _channel_table()
    ct_lat, st_lat = _position_tables(T)
    ct_ctx, st_ctx = _position_tables(Tc)

    X = jnp.concatenate([x, ctx], axis=1).astype(F32)
    for l in range(depth):
        mod_l = mods[l]
        g_l = norm_g[l].astype(F32)
        q, vi, gate, lf, dq, dk, dv, f = _inproj(
            X, mod_l, g_l, w_in[l].astype(BF16), cos_t, sin_t, lb[l], cs_tab, n_lat_tiles)
        of = _hgrn(q, vi, lf, T)
        da = _attention(dq, dk, dv, lam[l:l + 1], da_subln[l].astype(F32).reshape(DA_V, 1), T, lam_init[l])
        ft_lat = _fourier(ct_lat, st_lat, f, 0, T)
        ft_ctx = _fourier(ct_ctx, st_ctx, f, T // Tc, Tc)
        X = _post(X, mod_l, g_l, of, gate, da, ft_lat, ft_ctx, hg_onorm[l].astype(F32).reshape(1, HG_DIM),
                  w_out[l].astype(BF16), w_ffn_in[l].astype(BF16), w_ffn_out[l].astype(BF16), n_lat_tiles)
    return X[:, :T].astype(x.dtype)
```

```python
import functools
import math

import numpy as np
import jax
import jax.numpy as jnp
from jax import lax
from jax.experimental import pallas as pl
from jax.experimental.pallas import tpu as pltpu

F32 = jnp.float32
BF16 = jnp.bfloat16

D_MODEL = 1024
GRID_W = 64
HG_WIDTH = 512
HG_DIM = 128
HG_HEADS = 4
DA_WIDTH = 256
DA_HEADS = 4
DA_V = 64
DA_QK = 32
FT_WIDTH = 256
FT_GROUPS = 4
FT_GDIM = FT_WIDTH // FT_GROUPS
FF_HIDDEN = 2816
N_MOD = 6
EPS = 1e-6
ROPE_THETA = 10000.0
ROPE_AXIS_DIM = DA_QK // 2

_OFF_Q, _OFF_I, _OFF_G, _OFF_F, _OFF_DQ, _OFF_DK, _OFF_DV, _OFF_FT, IN_WIDTH = (
    0, 512, 1024, 1536, 2560, 2816, 3072, 3328, 3584)

TM = 256
TQ = 256
ATT_TILES = 2
KC = 256
CH = 64
SUB = 8
HGRN_UNROLL = 4
TK = 256
VT_ROWS = 80
MOD_ROWS = 8
MOD_TN = 1536
VMEM_LIMIT = 52 * 1024 * 1024

Q_SCALE = (DA_QK ** -0.5) * math.log2(math.e)


def _silu(x):
    return x * jax.nn.sigmoid(x)


def _rms(x):
    return x * lax.rsqrt(jnp.mean(x * x, axis=-1, keepdims=True) + EPS)


def _cparams(sem):
    return pltpu.CompilerParams(dimension_semantics=sem, vmem_limit_bytes=VMEM_LIMIT)


def _mod_kernel(c_ref, w_ref, b_ref, o_ref):
    a = _silu(c_ref[...]).astype(BF16)
    w = w_ref[0].astype(BF16)
    o_ref[0] = jnp.dot(a, w, preferred_element_type=F32) + b_ref[0]


def _modulation(cond, w_mod, b_mod):
    depth, d, n = w_mod.shape
    return pl.pallas_call(
        _mod_kernel,
        grid=(depth, n // MOD_TN),
        in_specs=[
            pl.BlockSpec((MOD_ROWS, d), lambda l, j: (0, 0)),
            pl.BlockSpec((1, d, MOD_TN), lambda l, j: (l, 0, j)),
            pl.BlockSpec((1, 1, MOD_TN), lambda l, j: (l, 0, j)),
        ],
        out_specs=pl.BlockSpec((1, MOD_ROWS, MOD_TN), lambda l, j: (l, 0, j)),
        out_shape=jax.ShapeDtypeStruct((depth, MOD_ROWS, n), F32),
        compiler_params=_cparams(("arbitrary", "arbitrary")),
        name="modulation",
    )(cond, w_mod, b_mod.reshape(depth, 1, n))


def _prep_kernel(lb_ref, lam_ref, lam_init_ref, lbo_ref, lamo_ref, *, depth):
    rows = [lb_ref[l:l + 1, :] for l in range(depth)]
    m = rows[0]
    for r in rows[1:]:
        m = jnp.maximum(m, r)
    e = [jnp.exp(r - m) for r in rows]
    tot = e[0]
    for r in e[1:]:
        tot = tot + r
    p = [r / tot for r in e]
    acc = p[0]
    lbo_ref[0:1, :] = acc - p[0]
    for l in range(1, depth):
        acc = acc + p[l]
        lbo_ref[l:l + 1, :] = acc - p[0]
    x = lam_ref[...]
    a = jnp.sum(x[:, 0:DA_QK] * x[:, DA_QK:2 * DA_QK], axis=-1, keepdims=True)
    b = jnp.sum(x[:, 2 * DA_QK:3 * DA_QK] * x[:, 3 * DA_QK:4 * DA_QK], axis=-1, keepdims=True)
    lamo_ref[...] = jnp.exp(a) - jnp.exp(b) + lam_init_ref[...]


def _prep(hg_lb_logits, da_lambda, lam_init):
    depth = hg_lb_logits.shape[1]
    lb_in = jnp.transpose(hg_lb_logits.astype(F32), (1, 0, 2)).reshape(depth, 2 * HG_WIDTH)
    lam_in = da_lambda.astype(F32).reshape(depth, 4 * DA_QK)
    lam_init_arr = jnp.asarray(np.broadcast_to(np.asarray(lam_init, np.float32)[:, None], (depth, TQ)))
    lb, lam = pl.pallas_call(
        functools.partial(_prep_kernel, depth=depth),
        out_shape=(jax.ShapeDtypeStruct((depth, 2 * HG_WIDTH), F32),
                   jax.ShapeDtypeStruct((depth, TQ), F32)),
        name="param_prep",
    )(lb_in, lam_in, lam_init_arr)
    return lb.reshape(depth, 2, HG_WIDTH), lam


def _inproj_kernel(x_ref, mod_ref, g_ref, w_ref, cos_ref, sin_ref, lb_ref, cs_ref,
                   q_ref, i_ref, gate_ref, lf_ref, dq_ref, dk_ref, dv_ref, f_ref):
    x = x_ref[0]
    mod = mod_ref[0]
    h = (_rms(x) * g_ref[0:1, :]) * (1.0 + mod[1:2, :]) + mod[0:1, :]
    hb = h.astype(BF16)

    def proj(a, b):
        return jnp.dot(hb, w_ref[:, a:b], preferred_element_type=F32)

    q_ref[0] = _silu(proj(_OFF_Q, _OFF_I)).astype(BF16)
    i_ref[0] = proj(_OFF_I, _OFF_G).astype(BF16)
    gate_ref[0] = _silu(proj(_OFF_G, _OFF_F)).astype(BF16)
    for d in range(2):
        z = proj(_OFF_F + d * HG_WIDTH, _OFF_F + (d + 1) * HG_WIDTH)
        lb = lb_ref[d:d + 1, :]
        lf_ref[d, 0] = jnp.log(lb + (1.0 - lb) * jax.nn.sigmoid(z))

    cos = cos_ref[...]
    sin = sin_ref[...]
    lane = lax.broadcasted_iota(jnp.int32, cos.shape, 1)
    upper_half = (lane & (ROPE_AXIS_DIM // 2)) != 0

    def rope(t):
        partner = jnp.where(upper_half,
                            pltpu.roll(t, ROPE_AXIS_DIM // 2, 1),
                            pltpu.roll(t, 2 * DA_HEADS * DA_QK - ROPE_AXIS_DIM // 2, 1))
        return t * cos + partner * sin

    dq_ref[0] = (rope(proj(_OFF_DQ, _OFF_DK)) * Q_SCALE).T.astype(BF16)
    dk_ref[0] = rope(proj(_OFF_DK, _OFF_DV)).astype(BF16)
    dv_ref[0] = proj(_OFF_DV, _OFF_FT).T.astype(BF16)
    ft = proj(_OFF_FT, IN_WIDTH).astype(BF16)
    cs = jnp.dot(ft, cs_ref[...], preferred_element_type=F32)
    f_ref[0] = cs[:, :FT_WIDTH].astype(BF16)
    f_ref[1] = cs[:, FT_WIDTH:].astype(BF16)


def _inproj(X, mod_l, g_l, w_in_l, cos_t, sin_t, lb_l, cs_tab, n_lat_tiles):
    B, NT, D = X.shape
    nt = NT // TM
    tok = lambda w: pl.BlockSpec((1, TM, w), lambda b, i: (b, i, 0))
    full = lambda a: pl.BlockSpec(a.shape, lambda b, i: (0,) * a.ndim)
    tok_t = pl.BlockSpec((1, DA_WIDTH, TM), lambda b, i: (b, 0, i))
    out_shapes = (
        jax.ShapeDtypeStruct((B, NT, HG_WIDTH), BF16),
        jax.ShapeDtypeStruct((B, NT, HG_WIDTH), BF16),
        jax.ShapeDtypeStruct((B, NT, HG_WIDTH), BF16),
        jax.ShapeDtypeStruct((2, B, NT, HG_WIDTH), F32),
        jax.ShapeDtypeStruct((B, DA_WIDTH, NT), BF16),
        jax.ShapeDtypeStruct((B, NT, DA_WIDTH), BF16),
        jax.ShapeDtypeStruct((B, DA_WIDTH, NT), BF16),
        jax.ShapeDtypeStruct((2, NT, B * FT_WIDTH), BF16),
    )
    out_specs = (
        tok(HG_WIDTH), tok(HG_WIDTH), tok(HG_WIDTH),
        pl.BlockSpec((2, 1, TM, HG_WIDTH), lambda b, i: (0, b, i, 0)),
        tok_t, tok(DA_WIDTH), tok_t,
        pl.BlockSpec((2, TM, FT_WIDTH), lambda b, i: (0, i, b)),
    )
    return pl.pallas_call(
        _inproj_kernel,
        grid=(B, nt),
        in_specs=[
            tok(D),
            pl.BlockSpec((1, N_MOD, D), lambda b, i: (jnp.where(i < n_lat_tiles, b, B), 0, 0)),
            full(g_l), full(w_in_l),
            pl.BlockSpec((TM, DA_WIDTH), lambda b, i: (i, 0)),
            pl.BlockSpec((TM, DA_WIDTH), lambda b, i: (i, 0)),
            full(lb_l), full(cs_tab),
        ],
        out_specs=out_specs,
        out_shape=out_shapes,
        compiler_params=_cparams(("arbitrary", "arbitrary")),
        name="adaln_inproj",
    )(X, mod_l, g_l, w_in_l, cos_t, sin_t, lb_l, cs_tab)


def _hgrn_direction(q, f, v, g, st, cmat, codes, rev):
    C, K = q.shape
    k = 1.0 - f
    g_hi = g.astype(BF16)
    g_lo = (g - g_hi.astype(F32)).astype(BF16)
    cs = jnp.dot(cmat, jnp.concatenate([g_hi, g_lo], axis=1), preferred_element_type=F32)
    bc = cs[:, :K] + cs[:, K:]
    tot = bc[0:1, :] if rev else bc[C - 1:C, :]

    nt_dims = (((1,), (1,)), ((), ()))
    tn_dims = (((0,), (0,)), ((), ()))
    qd = (q * jnp.exp(bc)).astype(BF16)
    o = lax.dot_general(qd, st.astype(BF16), nt_dims, preferred_element_type=F32)
    kdec = (k * jnp.exp(tot - bc)).astype(BF16)
    st_new = st * jnp.exp(tot) + lax.dot_general(v, kdec, tn_dims, preferred_element_type=F32)

    a = jnp.zeros((C, C), F32)
    f3 = f.reshape(C // SUB, SUB, K)
    f_shift = f
    dec = None
    for d8 in range(SUB):
        if d8 == 0:
            term = q * k
        else:
            dec = f_shift if dec is None else dec * f_shift
            f_shift = pltpu.roll(f3, (SUB - d8) if rev else d8, 1).reshape(C, K)
            term = q * (1.0 - f_shift) * dec
        red = jnp.sum(term, axis=-1, keepdims=True)
        a = jnp.where(codes == d8, red, a)

    zeros8 = jnp.zeros((SUB, K), F32)
    b = SUB
    level = 0
    while b < C:
        ql, kl = [], []
        for r in range(0, C, SUB):
            base = (r // (2 * b)) * 2 * b
            upper = (r - base) >= b
            ref_row = base + b if rev else base + b - 1
            q_side = (not upper) if rev else upper
            bref = bc[ref_row:ref_row + 1, :]
            bg = bc[r:r + SUB, :]
            if q_side:
                ql.append(q[r:r + SUB, :] * jnp.exp(bg - bref))
                kl.append(zeros8)
            else:
                kl.append(k[r:r + SUB, :] * jnp.exp(bref - bg))
                ql.append(zeros8)
        qlb = jnp.concatenate(ql, axis=0).astype(BF16)
        klb = jnp.concatenate(kl, axis=0).astype(BF16)
        p = lax.dot_general(qlb, klb, nt_dims, preferred_element_type=F32)
        a = jnp.where(codes == SUB + level, p, a)
        b *= 2
        level += 1

    o = o + jnp.dot(a.astype(BF16), v, preferred_element_type=F32)
    return o, st_new


def _hgrn_tables(C):
    t = np.arange(C)[:, None]
    s = np.arange(C)[None, :]
    cmats, codes = [], []
    for rev in (False, True):
        d = (s - t) if rev else (t - s)
        cmats.append((d >= 0).astype(np.float32))
        code = np.full((C, C), -1, np.int32)
        inside = (d >= 0) & ((t // SUB) == (s // SUB))
        code[inside] = d[inside]
        b, level = SUB, 0
        while b < C:
            split = (d > 0) & ((t // (2 * b)) == (s // (2 * b))) & ((t // b) != (s // b))
            code[split] = SUB + level
            b *= 2
            level += 1
        codes.append(code)
    return jnp.asarray(np.stack(cmats), BF16), jnp.asarray(np.stack(codes), jnp.int32)


def _hgrn_kernel(q_ref, v_ref, lf_ref, cmat_ref, code_ref, o_ref, st_ref, *, n_lat, n_ctx):
    C = CH
    n_chunks = n_lat + n_ctx
    st_ref[...] = jnp.zeros(st_ref.shape, F32)

    def body(j, carry):
        for d, rev in enumerate((False, True)):
            c = (n_chunks - 1 - j) if rev else jnp.where(j < n_ctx, n_lat + j, j - n_ctx)
            r0 = pl.multiple_of(c * C, C)
            q = q_ref[0, pl.ds(r0, C), :].astype(F32)
            v = v_ref[0, pl.ds(r0, C), :]
            g = lf_ref[d, 0, pl.ds(r0, C), :]
            o, st_new = _hgrn_direction(q, jnp.exp(g), v, g, st_ref[d], cmat_ref[d], code_ref[d], rev)
            st_ref[d] = st_new
            o_ref[d, 0, pl.ds(r0, C), :] = o.astype(o_ref.dtype)
        return carry

    lax.fori_loop(0, n_chunks, body, 0, unroll=HGRN_UNROLL)


def _hgrn(q, v, lf, n_lat_rows):
    B, NT, _ = q.shape
    n_lat = n_lat_rows // CH
    n_ctx = (NT - n_lat_rows) // CH
    assert (n_lat + n_ctx) % HGRN_UNROLL == 0
    cmat, codes = _hgrn_tables(CH)
    head = pl.BlockSpec((1, NT, HG_DIM), lambda b, h: (b, 0, h))
    both = pl.BlockSpec((2, 1, NT, HG_DIM), lambda b, h: (0, b, 0, h))
    table = pl.BlockSpec((2, CH, CH), lambda b, h: (0, 0, 0))
    return pl.pallas_call(
        functools.partial(_hgrn_kernel, n_lat=n_lat, n_ctx=n_ctx),
        grid=(B, HG_HEADS),
        in_specs=[head, head, both, table, table],
        out_specs=both,
        out_shape=jax.ShapeDtypeStruct((2, B, NT, HG_WIDTH), BF16),
        scratch_shapes=[pltpu.VMEM((2, HG_DIM, HG_DIM), F32)],
        compiler_params=_cparams(("arbitrary", "arbitrary")),
        name="hgrn_scan",
    )(q, v, lf, cmat, codes)


def _attn_kernel(qT_ref, k_ref, vT_ref, lam_ref, sub_ref, o_ref, s_ref, *, chunks, n_q, out_scale):
    n = len(chunks)
    odd_head = (pl.program_id(1) % 2) == 1
    ones_row = lax.broadcasted_iota(jnp.int32, (VT_ROWS - DA_V, n * KC), 0) == 0
    vt = jnp.concatenate(
        [vT_ref[0, :, chunks[0] * KC:(chunks[-1] + 1) * KC],
         jnp.where(ones_row, 1.0, 0.0).astype(BF16)], axis=0)
    streams = [(t, mp) for t in range(n_q) for mp in range(2)]
    zeros = jnp.zeros((DA_QK, TQ), BF16)

    def query_operand(t, mp):
        q = qT_ref[0, mp * DA_QK:(mp + 1) * DA_QK, t * TQ:(t + 1) * TQ]
        even = jnp.concatenate([q if j == mp else zeros for j in range(4)], axis=0)
        odd = jnp.concatenate([q if j == 2 + mp else zeros for j in range(4)], axis=0)
        return jnp.where(odd_head, odd, even)

    slot0 = jnp.minimum(pl.program_id(2), 0)

    def pass1(slot):
        rhs = query_operand(*streams[slot])
        m8 = None
        for i, c in enumerate(chunks):
            s = jnp.dot(k_ref[0, c * KC:(c + 1) * KC, :], rhs, preferred_element_type=F32)
            s_ref[slot, i * KC:(i + 1) * KC, :] = s
            parts = [s[r:r + 8, :] for r in range(0, KC, 8)]
            while len(parts) > 1:
                parts = [jnp.maximum(parts[j], parts[j + 1]) for j in range(0, len(parts), 2)]
            m8 = parts[0] if m8 is None else jnp.maximum(m8, parts[0])
        return jnp.max(m8, axis=0, keepdims=True)

    def pass2(slot, m):
        p = jnp.exp2((s_ref[slot0 + slot] - m).astype(BF16))
        return jnp.dot(vt, p, preferred_element_type=F32)

    maxes = [pass1(slot) for slot in range(len(streams))]
    accs = {st: pass2(slot, maxes[slot]) for slot, st in enumerate(streams)}

    for t in range(n_q):
        acc0, acc1 = accs[(t, 0)], accs[(t, 1)]
        o = (acc0[0:DA_V] / acc0[DA_V:DA_V + 1]
             - lam_ref[...] * (acc1[0:DA_V] / acc1[DA_V:DA_V + 1]))
        y = o * lax.rsqrt(jnp.mean(o * o, axis=0, keepdims=True) + EPS)
        o_ref[0, :, t * TQ:(t + 1) * TQ] = (y * sub_ref[...] * out_scale).astype(BF16)


def _attention_call(dqT, dk, dvT, lam_l, subln_l, chunks, n_q, q_block0, n_steps, lam_init):
    B, _, NT = dqT.shape
    w = n_q * TQ
    kern = functools.partial(_attn_kernel, chunks=tuple(chunks), n_q=n_q, out_scale=1.0 - lam_init)
    return pl.pallas_call(
        kern,
        grid=(B, DA_HEADS, n_steps),
        in_specs=[
            pl.BlockSpec((1, 2 * DA_QK, w), lambda b, h, i: (b, h, q_block0 + i)),
            pl.BlockSpec((1, NT, 2 * DA_V), lambda b, h, i: (b, 0, h // 2)),
            pl.BlockSpec((1, DA_V, NT), lambda b, h, i: (b, h, 0)),
            pl.BlockSpec((1, TQ), lambda b, h, i: (0, 0)),
            pl.BlockSpec((DA_V, 1), lambda b, h, i: (0, 0)),
        ],
        out_specs=pl.BlockSpec((1, DA_V, w), lambda b, h, i: (b, h, i)),
        out_shape=jax.ShapeDtypeStruct((B, DA_WIDTH, n_steps * w), BF16),
        scratch_shapes=[pltpu.VMEM((2 * n_q, len(chunks) * KC, TQ), F32)],
        compiler_params=_cparams(("arbitrary", "arbitrary", "arbitrary")),
        name="diff_attention",
    )(dqT, dk, dvT, lam_l, subln_l)


def _attention(dqT, dk, dvT, lam_l, subln_l, n_lat_rows, lam_init, with_ctx):
    NT = dqT.shape[2]
    n_chunks, n_lat_chunks = NT // KC, n_lat_rows // KC
    n_ctx_rows = NT - n_lat_rows
    lat = _attention_call(dqT, dk, dvT, lam_l, subln_l, range(n_chunks), ATT_TILES, 0,
                          n_lat_rows // (ATT_TILES * TQ), lam_init)
    if not with_ctx:
        return lat, lat
    ctx = _attention_call(dqT, dk, dvT, lam_l, subln_l, range(n_lat_chunks, n_chunks), 1,
                          n_lat_rows // TQ, n_ctx_rows // TQ, lam_init)
    return lat, ctx


def _fourier_kernel(ct_ref, st_ref, f_ref, o_ref):
    o_ref[...] = (jnp.dot(ct_ref[...], f_ref[0], preferred_element_type=F32)
                  + jnp.dot(st_ref[...], f_ref[1], preferred_element_type=F32)).astype(BF16)


def _fourier(ct, st_neg, f, row_block, n_rows):
    W = f.shape[2]
    tk = min(TK, n_rows)
    return pl.pallas_call(
        _fourier_kernel,
        grid=(n_rows // tk,),
        in_specs=[
            pl.BlockSpec((tk, n_rows), lambda i: (i, 0)),
            pl.BlockSpec((tk, n_rows), lambda i: (i, 0)),
            pl.BlockSpec((2, n_rows, W), lambda i: (0, row_block, 0)),
        ],
        out_specs=pl.BlockSpec((tk, W), lambda i: (i, 0)),
        out_shape=jax.ShapeDtypeStruct((n_rows, W), BF16),
        compiler_params=_cparams(("arbitrary",)),
        name="fourier_mix",
    )(ct, st_neg, f)


def _post_kernel(x_ref, mod_ref, g_ref, of_ref, gate_ref, dal_ref, dac_ref, ftl_ref, ftc_ref, onorm_ref,
                 wo_ref, wfi_ref, wfo_ref, o_ref, *, n_lat_tiles):
    x = x_ref[0]
    mod = mod_ref[0]
    is_lat = pl.program_id(1) < n_lat_tiles
    ft = jnp.where(is_lat, ftl_ref[...], ftc_ref[...])
    da_t = jnp.where(is_lat, dal_ref[0], dac_ref[0])
    o = of_ref[0, 0].astype(F32) + of_ref[1, 0].astype(F32)
    onorm = onorm_ref[...]
    heads = [_rms(o[:, h * HG_DIM:(h + 1) * HG_DIM]) * onorm for h in range(HG_HEADS)]
    hg = (jnp.concatenate(heads, axis=-1) * gate_ref[0].astype(F32)).astype(BF16)
    y = (jnp.dot(hg, wo_ref[0:HG_WIDTH, :], preferred_element_type=F32)
         + lax.dot_general(da_t, wo_ref[HG_WIDTH:HG_WIDTH + DA_WIDTH, :], (((0,), (0,)), ((), ())),
                           preferred_element_type=F32)
         + jnp.dot(ft, wo_ref[HG_WIDTH + DA_WIDTH:, :], preferred_element_type=F32))
    x1 = x + mod[2:3, :] * (_rms(y) * g_ref[1:2, :])
    h2 = ((_rms(x1) * g_ref[2:3, :]) * (1.0 + mod[4:5, :]) + mod[3:4, :]).astype(BF16)
    gu = jnp.dot(h2, wfi_ref[...], preferred_element_type=F32)
    act = (_silu(gu[:, :FF_HIDDEN]) * gu[:, FF_HIDDEN:]).astype(BF16)
    y2 = jnp.dot(act, wfo_ref[...], preferred_element_type=F32)
    o_ref[0] = x1 + mod[5:6, :] * (_rms(y2) * g_ref[3:4, :])


def _post(X, mod_l, g_l, of, gate, da_lat, da_ctx, ft_lat, ft_ctx, onorm_l, wo, wfi, wfo, n_lat_tiles, n_tiles):
    B, NT, D = X.shape
    tok = lambda w: pl.BlockSpec((1, TM, w), lambda b, i: (b, i, 0))
    const = lambda a: pl.BlockSpec(a.shape, lambda b, i: (0,) * a.ndim, pipeline_mode=pl.Buffered(1))
    return pl.pallas_call(
        functools.partial(_post_kernel, n_lat_tiles=n_lat_tiles),
        grid=(B, n_tiles),
        in_specs=[
            tok(D),
            pl.BlockSpec((1, N_MOD, D), lambda b, i: (jnp.where(i < n_lat_tiles, b, B), 0, 0)),
            const(g_l),
            pl.BlockSpec((2, 1, TM, HG_WIDTH), lambda b, i: (0, b, i, 0)),
            tok(HG_WIDTH),
            pl.BlockSpec((1, DA_WIDTH, TM), lambda b, i: (b, 0, jnp.minimum(i, n_lat_tiles - 1))),
            pl.BlockSpec((1, DA_WIDTH, TM), lambda b, i: (b, 0, jnp.maximum(i - n_lat_tiles, 0))),
            pl.BlockSpec((TM, FT_WIDTH), lambda b, i: (jnp.minimum(i, n_lat_tiles - 1), b)),
            pl.BlockSpec((TM, FT_WIDTH), lambda b, i: (jnp.maximum(i - n_lat_tiles, 0), b)),
            const(onorm_l), const(wo), const(wfi), const(wfo),
        ],
        out_specs=tok(D),
        out_shape=jax.ShapeDtypeStruct((B, n_tiles * TM, D), F32),
        compiler_params=_cparams(("arbitrary", "arbitrary")),
        name="outproj_ffn",
    )(X, mod_l, g_l, of, gate, da_lat, da_ctx, ft_lat, ft_ctx, onorm_l, wo, wfi, wfo)


def _rope_tables(t_lat, t_ctx):
    pos = jnp.arange(t_lat)
    inv_freq = 1.0 / (ROPE_THETA ** (jnp.arange(0, ROPE_AXIS_DIM, 2, dtype=F32) / ROPE_AXIS_DIM))
    ang = jnp.stack([pos // GRID_W, pos % GRID_W], axis=-1).astype(F32)[:, :, None] * inv_freq
    cos, sin = jnp.cos(ang), jnp.sin(ang)
    cos32 = jnp.stack([cos, cos], axis=2).reshape(t_lat, DA_QK)
    sin32 = jnp.stack([-sin, sin], axis=2).reshape(t_lat, DA_QK)
    reps = DA_WIDTH // DA_QK
    cos_t = jnp.concatenate([jnp.tile(cos32, (1, reps)), jnp.ones((t_ctx, DA_WIDTH), F32)], axis=0)
    sin_t = jnp.concatenate([jnp.tile(sin32, (1, reps)), jnp.zeros((t_ctx, DA_WIDTH), F32)], axis=0)
    return cos_t, sin_t


def _channel_table():
    idx = np.arange(FT_GDIM)
    ang = 2.0 * np.pi * ((idx[:, None] * idx[None, :]) % FT_GDIM) / FT_GDIM
    eye = np.eye(FT_GROUPS)
    return jnp.asarray(np.concatenate([np.kron(eye, np.cos(ang)), np.kron(eye, np.sin(ang))], axis=1), BF16)


def _position_tables(n):
    lo = 1
    while lo * lo < n:
        lo *= 2
    hi = n // lo
    k = jnp.arange(n)
    a1 = 2.0 * np.pi * ((k[:, None] * jnp.arange(hi)[None, :]) % hi).astype(F32) / hi
    a2 = 2.0 * np.pi * ((k[:, None] * jnp.arange(lo)[None, :]) % n).astype(F32) / n
    c1, s1 = jnp.cos(a1)[:, :, None], jnp.sin(a1)[:, :, None]
    c2, s2 = jnp.cos(a2)[:, None, :], jnp.sin(a2)[:, None, :]
    scale = 1.0 / math.sqrt(n * FT_GDIM)
    ct = ((c1 * c2 - s1 * s2) * scale).reshape(n, n).astype(BF16)
    st_neg = ((s1 * c2 + c1 * s2) * (-scale)).reshape(n, n).astype(BF16)
    return ct, st_neg


def kernel(x, c, ctx, c_ctx, w_mod, b_mod, norm_g, w_in, w_out, hg_lb_logits, hg_onorm,
           da_lambda, da_subln, w_ffn_in, w_ffn_out):
    B, T, D = x.shape
    Tc = ctx.shape[1]
    depth = w_mod.shape[0]
    assert D == D_MODEL and B + 1 <= MOD_ROWS
    assert T % TM == 0 and Tc % TM == 0 and T % KC == 0 and Tc % KC == 0 and T % CH == 0 and Tc % CH == 0
    n_lat_tiles = T // TM
    NT = T + Tc

    lam_init = [0.8 - 0.6 * math.exp(-0.3 * l) for l in range(depth)]
    lb, lam = _prep(hg_lb_logits, da_lambda, lam_init)

    cond = jnp.concatenate([c.astype(F32), c_ctx.astype(F32)[None, :],
                            jnp.zeros((MOD_ROWS - B - 1, D), F32)], axis=0)
    mods = _modulation(cond, w_mod, b_mod).reshape(depth, MOD_ROWS, N_MOD, D)

    cos_t, sin_t = _rope_tables(T, Tc)
    cs_tab = _channel_table()
    ct_lat, st_lat = _position_tables(T)
    ct_ctx, st_ctx = _position_tables(Tc)

    X = jnp.concatenate([x, ctx], axis=1).astype(F32)
    for l in range(depth):
        mod_l = mods[l]
        g_l = norm_g[l].astype(F32)
        q, vi, gate, lf, dq, dk, dv, f = _inproj(
            X, mod_l, g_l, w_in[l].astype(BF16), cos_t, sin_t, lb[l], cs_tab, n_lat_tiles)
        of = _hgrn(q, vi, lf, T)
        with_ctx = l < depth - 1
        da_lat, da_ctx = _attention(dq, dk, dv, lam[l:l + 1], da_subln[l].astype(F32).reshape(DA_V, 1), T,
                                    lam_init[l], with_ctx)
        ft_lat = _fourier(ct_lat, st_lat, f, 0, T)
        ft_ctx = _fourier(ct_ctx, st_ctx, f, T // Tc, Tc) if with_ctx else ft_lat
        X = _post(X, mod_l, g_l, of, gate, da_lat, da_ctx, ft_lat, ft_ctx, hg_onorm[l].astype(F32).reshape(1, HG_DIM),
                  w_out[l].astype(BF16), w_ffn_in[l].astype(BF16), w_ffn_out[l].astype(BF16), n_lat_tiles,
                  NT // TM if with_ctx else n_lat_tiles)
    return X.astype(x.dtype)
```

```python
import functools
import math

import numpy as np
import jax
import jax.numpy as jnp
from jax import lax
from jax.experimental import pallas as pl
from jax.experimental.pallas import tpu as pltpu

F32 = jnp.float32
BF16 = jnp.bfloat16

D_MODEL = 1024
GRID_W = 64
HG_WIDTH = 512
HG_DIM = 128
HG_HEADS = 4
DA_WIDTH = 256
DA_HEADS = 4
DA_V = 64
DA_QK = 32
FT_WIDTH = 256
FT_GROUPS = 4
FT_GDIM = FT_WIDTH // FT_GROUPS
FF_HIDDEN = 2816
N_MOD = 6
EPS = 1e-6
ROPE_THETA = 10000.0
ROPE_AXIS_DIM = DA_QK // 2

_OFF_Q, _OFF_I, _OFF_G, _OFF_F, _OFF_DQ, _OFF_DK, _OFF_DV, _OFF_FT, IN_WIDTH = (
    0, 512, 1024, 1536, 2560, 2816, 3072, 3328, 3584)

TM = 256
TQ = 256
ATT_TILES = 2
KC = 256
CH = 128
SUB = 8
HGRN_UNROLL = 17
TK = 256
VT_ROWS = 80
MOD_ROWS = 8
MOD_TN = 1536
VMEM_LIMIT = 52 * 1024 * 1024

Q_SCALE = (DA_QK ** -0.5) * math.log2(math.e)


def _silu(x):
    return x * jax.nn.sigmoid(x)


def _rms(x):
    return x * lax.rsqrt(jnp.mean(x * x, axis=-1, keepdims=True) + EPS)


def _cparams(sem):
    return pltpu.CompilerParams(dimension_semantics=sem, vmem_limit_bytes=VMEM_LIMIT)


def _mod_kernel(c_ref, w_ref, b_ref, o_ref):
    a = _silu(c_ref[...]).astype(BF16)
    w = w_ref[0].astype(BF16)
    o_ref[0] = jnp.dot(a, w, preferred_element_type=F32) + b_ref[0]


def _modulation(cond, w_mod, b_mod):
    depth, d, n = w_mod.shape
    return pl.pallas_call(
        _mod_kernel,
        grid=(depth, n // MOD_TN),
        in_specs=[
            pl.BlockSpec((MOD_ROWS, d), lambda l, j: (0, 0)),
            pl.BlockSpec((1, d, MOD_TN), lambda l, j: (l, 0, j)),
            pl.BlockSpec((1, 1, MOD_TN), lambda l, j: (l, 0, j)),
        ],
        out_specs=pl.BlockSpec((1, MOD_ROWS, MOD_TN), lambda l, j: (l, 0, j)),
        out_shape=jax.ShapeDtypeStruct((depth, MOD_ROWS, n), F32),
        compiler_params=_cparams(("arbitrary", "arbitrary")),
        name="modulation",
    )(cond, w_mod, b_mod.reshape(depth, 1, n))


def _prep_kernel(lb_ref, lam_ref, lam_init_ref, lbo_ref, lamo_ref, *, depth):
    rows = [lb_ref[l:l + 1, :] for l in range(depth)]
    m = rows[0]
    for r in rows[1:]:
        m = jnp.maximum(m, r)
    e = [jnp.exp(r - m) for r in rows]
    tot = e[0]
    for r in e[1:]:
        tot = tot + r
    p = [r / tot for r in e]
    acc = p[0]
    lbo_ref[0:1, :] = acc - p[0]
    for l in range(1, depth):
        acc = acc + p[l]
        lbo_ref[l:l + 1, :] = acc - p[0]
    x = lam_ref[...]
    a = jnp.sum(x[:, 0:DA_QK] * x[:, DA_QK:2 * DA_QK], axis=-1, keepdims=True)
    b = jnp.sum(x[:, 2 * DA_QK:3 * DA_QK] * x[:, 3 * DA_QK:4 * DA_QK], axis=-1, keepdims=True)
    lamo_ref[...] = jnp.exp(a) - jnp.exp(b) + lam_init_ref[...]


def _prep(hg_lb_logits, da_lambda, lam_init):
    depth = hg_lb_logits.shape[1]
    lb_in = jnp.transpose(hg_lb_logits.astype(F32), (1, 0, 2)).reshape(depth, 2 * HG_WIDTH)
    lam_in = da_lambda.astype(F32).reshape(depth, 4 * DA_QK)
    lam_init_arr = jnp.asarray(np.broadcast_to(np.asarray(lam_init, np.float32)[:, None], (depth, TQ)))
    lb, lam = pl.pallas_call(
        functools.partial(_prep_kernel, depth=depth),
        out_shape=(jax.ShapeDtypeStruct((depth, 2 * HG_WIDTH), F32),
                   jax.ShapeDtypeStruct((depth, TQ), F32)),
        name="param_prep",
    )(lb_in, lam_in, lam_init_arr)
    return lb.reshape(depth, 2, HG_WIDTH), lam


def _inproj_kernel(x_ref, mod_ref, g_ref, w_ref, cos_ref, sin_ref, lb_ref, cs_ref,
                   q_ref, i_ref, gate_ref, lf_ref, dq_ref, dk_ref, dv_ref, f_ref):
    x = x_ref[0]
    mod = mod_ref[0]
    h = (_rms(x) * g_ref[0:1, :]) * (1.0 + mod[1:2, :]) + mod[0:1, :]
    hb = h.astype(BF16)

    def proj(a, b):
        return jnp.dot(hb, w_ref[:, a:b], preferred_element_type=F32)

    q_ref[0] = _silu(proj(_OFF_Q, _OFF_I)).astype(BF16)
    i_ref[0] = proj(_OFF_I, _OFF_G).astype(BF16)
    gate_ref[0] = _silu(proj(_OFF_G, _OFF_F)).astype(BF16)
    for d in range(2):
        z = proj(_OFF_F + d * HG_WIDTH, _OFF_F + (d + 1) * HG_WIDTH)
        lb = lb_ref[d:d + 1, :]
        lf_ref[d, 0] = jnp.log(lb + (1.0 - lb) * jax.nn.sigmoid(z))

    cos = cos_ref[...]
    sin = sin_ref[...]
    lane = lax.broadcasted_iota(jnp.int32, cos.shape, 1)
    upper_half = (lane & (ROPE_AXIS_DIM // 2)) != 0

    def rope(t):
        partner = jnp.where(upper_half,
                            pltpu.roll(t, ROPE_AXIS_DIM // 2, 1),
                            pltpu.roll(t, 2 * DA_HEADS * DA_QK - ROPE_AXIS_DIM // 2, 1))
        return t * cos + partner * sin

    dq_ref[0] = (rope(proj(_OFF_DQ, _OFF_DK)) * Q_SCALE).T.astype(BF16)
    dk_ref[0] = rope(proj(_OFF_DK, _OFF_DV)).astype(BF16)
    dv_ref[0] = proj(_OFF_DV, _OFF_FT).T.astype(BF16)
    ft = proj(_OFF_FT, IN_WIDTH).astype(BF16)
    cs = jnp.dot(ft, cs_ref[...], preferred_element_type=F32)
    f_ref[0] = cs[:, :FT_WIDTH].astype(BF16)
    f_ref[1] = cs[:, FT_WIDTH:].astype(BF16)


def _inproj(X, mod_l, g_l, w_in_l, cos_t, sin_t, lb_l, cs_tab, n_lat_tiles):
    B, NT, D = X.shape
    nt = NT // TM
    tok = lambda w: pl.BlockSpec((1, TM, w), lambda b, i: (b, i, 0))
    full = lambda a: pl.BlockSpec(a.shape, lambda b, i: (0,) * a.ndim)
    tok_t = pl.BlockSpec((1, DA_WIDTH, TM), lambda b, i: (b, 0, i))
    out_shapes = (
        jax.ShapeDtypeStruct((B, NT, HG_WIDTH), BF16),
        jax.ShapeDtypeStruct((B, NT, HG_WIDTH), BF16),
        jax.ShapeDtypeStruct((B, NT, HG_WIDTH), BF16),
        jax.ShapeDtypeStruct((2, B, NT, HG_WIDTH), F32),
        jax.ShapeDtypeStruct((B, DA_WIDTH, NT), BF16),
        jax.ShapeDtypeStruct((B, NT, DA_WIDTH), BF16),
        jax.ShapeDtypeStruct((B, DA_WIDTH, NT), BF16),
        jax.ShapeDtypeStruct((2, NT, B * FT_WIDTH), BF16),
    )
    out_specs = (
        tok(HG_WIDTH), tok(HG_WIDTH), tok(HG_WIDTH),
        pl.BlockSpec((2, 1, TM, HG_WIDTH), lambda b, i: (0, b, i, 0)),
        tok_t, tok(DA_WIDTH), tok_t,
        pl.BlockSpec((2, TM, FT_WIDTH), lambda b, i: (0, i, b)),
    )
    return pl.pallas_call(
        _inproj_kernel,
        grid=(B, nt),
        in_specs=[
            tok(D),
            pl.BlockSpec((1, N_MOD, D), lambda b, i: (jnp.where(i < n_lat_tiles, b, B), 0, 0)),
            full(g_l), full(w_in_l),
            pl.BlockSpec((TM, DA_WIDTH), lambda b, i: (i, 0)),
            pl.BlockSpec((TM, DA_WIDTH), lambda b, i: (i, 0)),
            full(lb_l), full(cs_tab),
        ],
        out_specs=out_specs,
        out_shape=out_shapes,
        compiler_params=_cparams(("arbitrary", "arbitrary")),
        name="adaln_inproj",
    )(X, mod_l, g_l, w_in_l, cos_t, sin_t, lb_l, cs_tab)


def _hgrn_direction(q, f, v, g, st, cmat, codes, rev):
    C, K = q.shape
    k = 1.0 - f
    g_hi = g.astype(BF16)
    g_lo = (g - g_hi.astype(F32)).astype(BF16)
    cs = jnp.dot(cmat, jnp.concatenate([g_hi, g_lo], axis=1), preferred_element_type=F32)
    bc = cs[:, :K] + cs[:, K:]
    tot = bc[0:1, :] if rev else bc[C - 1:C, :]

    nt_dims = (((1,), (1,)), ((), ()))
    tn_dims = (((0,), (0,)), ((), ()))
    qd = (q * jnp.exp(bc)).astype(BF16)
    o = lax.dot_general(qd, st.astype(BF16), nt_dims, preferred_element_type=F32)
    kdec = (k * jnp.exp(tot - bc)).astype(BF16)
    st_new = st * jnp.exp(tot) + lax.dot_general(v, kdec, tn_dims, preferred_element_type=F32)

    a = jnp.zeros((C, C), F32)
    f3 = f.reshape(C // SUB, SUB, K)
    f_shift = f
    dec = None
    for d8 in range(SUB):
        if d8 == 0:
            term = q * k
        else:
            dec = f_shift if dec is None else dec * f_shift
            f_shift = pltpu.roll(f3, (SUB - d8) if rev else d8, 1).reshape(C, K)
            term = q * (1.0 - f_shift) * dec
        red = jnp.sum(term, axis=-1, keepdims=True)
        a = jnp.where(codes == d8, red, a)

    zeros8 = jnp.zeros((SUB, K), F32)
    b = SUB
    level = 0
    while b < C:
        ql, kl = [], []
        for r in range(0, C, SUB):
            base = (r // (2 * b)) * 2 * b
            upper = (r - base) >= b
            ref_row = base + b if rev else base + b - 1
            q_side = (not upper) if rev else upper
            bref = bc[ref_row:ref_row + 1, :]
            bg = bc[r:r + SUB, :]
            if q_side:
                ql.append(q[r:r + SUB, :] * jnp.exp(bg - bref))
                kl.append(zeros8)
            else:
                kl.append(k[r:r + SUB, :] * jnp.exp(bref - bg))
                ql.append(zeros8)
        qlb = jnp.concatenate(ql, axis=0).astype(BF16)
        klb = jnp.concatenate(kl, axis=0).astype(BF16)
        p = lax.dot_general(qlb, klb, nt_dims, preferred_element_type=F32)
        a = jnp.where(codes == SUB + level, p, a)
        b *= 2
        level += 1

    o = o + jnp.dot(a.astype(BF16), v, preferred_element_type=F32)
    return o, st_new


def _hgrn_tables(C):
    t = np.arange(C)[:, None]
    s = np.arange(C)[None, :]
    cmats, codes = [], []
    for rev in (False, True):
        d = (s - t) if rev else (t - s)
        cmats.append((d >= 0).astype(np.float32))
        code = np.full((C, C), -1, np.int32)
        inside = (d >= 0) & ((t // SUB) == (s // SUB))
        code[inside] = d[inside]
        b, level = SUB, 0
        while b < C:
            split = (d > 0) & ((t // (2 * b)) == (s // (2 * b))) & ((t // b) != (s // b))
            code[split] = SUB + level
            b *= 2
            level += 1
        codes.append(code)
    return jnp.asarray(np.stack(cmats), BF16), jnp.asarray(np.stack(codes), jnp.int32)


def _hgrn_kernel(q_ref, v_ref, lf_ref, cmat_ref, code_ref, o_ref, st_ref, *, n_lat, n_ctx):
    C = CH
    n_chunks = n_lat + n_ctx
    st_ref[...] = jnp.zeros(st_ref.shape, F32)

    def body(j, carry):
        for d, rev in enumerate((False, True)):
            c = (n_chunks - 1 - j) if rev else jnp.where(j < n_ctx, n_lat + j, j - n_ctx)
            r0 = pl.multiple_of(c * C, C)
            q = q_ref[0, pl.ds(r0, C), :].astype(F32)
            v = v_ref[0, pl.ds(r0, C), :]
            g = lf_ref[d, 0, pl.ds(r0, C), :]
            o, st_new = _hgrn_direction(q, jnp.exp(g), v, g, st_ref[d], cmat_ref[d], code_ref[d], rev)
            st_ref[d] = st_new
            o_ref[d, 0, pl.ds(r0, C), :] = o.astype(o_ref.dtype)
        return carry

    lax.fori_loop(0, n_chunks, body, 0, unroll=HGRN_UNROLL)


def _hgrn(q, v, lf, n_lat_rows):
    B, NT, _ = q.shape
    n_lat = n_lat_rows // CH
    n_ctx = (NT - n_lat_rows) // CH
    assert (n_lat + n_ctx) % HGRN_UNROLL == 0
    cmat, codes = _hgrn_tables(CH)
    head = pl.BlockSpec((1, NT, HG_DIM), lambda b, h: (b, 0, h))
    both = pl.BlockSpec((2, 1, NT, HG_DIM), lambda b, h: (0, b, 0, h))
    table = pl.BlockSpec((2, CH, CH), lambda b, h: (0, 0, 0))
    return pl.pallas_call(
        functools.partial(_hgrn_kernel, n_lat=n_lat, n_ctx=n_ctx),
        grid=(B, HG_HEADS),
        in_specs=[head, head, both, table, table],
        out_specs=both,
        out_shape=jax.ShapeDtypeStruct((2, B, NT, HG_WIDTH), BF16),
        scratch_shapes=[pltpu.VMEM((2, HG_DIM, HG_DIM), F32)],
        compiler_params=_cparams(("arbitrary", "arbitrary")),
        name="hgrn_scan",
    )(q, v, lf, cmat, codes)


def _attn_kernel(qT_ref, k_ref, vT_ref, lam_ref, sub_ref, o_ref, s_ref, *, chunks, n_q, out_scale):
    n = len(chunks)
    odd_head = (pl.program_id(1) % 2) == 1
    ones_row = lax.broadcasted_iota(jnp.int32, (VT_ROWS - DA_V, n * KC), 0) == 0
    vt = jnp.concatenate(
        [vT_ref[0, :, chunks[0] * KC:(chunks[-1] + 1) * KC],
         jnp.where(ones_row, 1.0, 0.0).astype(BF16)], axis=0)
    streams = [(t, mp) for t in range(n_q) for mp in range(2)]
    zeros = jnp.zeros((DA_QK, TQ), BF16)

    def query_operand(t, mp):
        q = qT_ref[0, mp * DA_QK:(mp + 1) * DA_QK, t * TQ:(t + 1) * TQ]
        even = jnp.concatenate([q if j == mp else zeros for j in range(4)], axis=0)
        odd = jnp.concatenate([q if j == 2 + mp else zeros for j in range(4)], axis=0)
        return jnp.where(odd_head, odd, even)

    slot0 = jnp.minimum(pl.program_id(2), 0)

    def pass1(slot):
        rhs = query_operand(*streams[slot])
        m8 = None
        for i, c in enumerate(chunks):
            s = jnp.dot(k_ref[0, c * KC:(c + 1) * KC, :], rhs, preferred_element_type=F32)
            s_ref[slot, i * KC:(i + 1) * KC, :] = s
            parts = [s[r:r + 8, :] for r in range(0, KC, 8)]
            while len(parts) > 1:
                parts = [jnp.maximum(parts[j], parts[j + 1]) for j in range(0, len(parts), 2)]
            m8 = parts[0] if m8 is None else jnp.maximum(m8, parts[0])
        return jnp.max(m8, axis=0, keepdims=True)

    def pass2(slot, m):
        p = jnp.exp2((s_ref[slot0 + slot] - m).astype(BF16))
        return jnp.dot(vt, p, preferred_element_type=F32)

    maxes = [pass1(slot) for slot in range(len(streams))]
    accs = {st: pass2(slot, maxes[slot]) for slot, st in enumerate(streams)}

    for t in range(n_q):
        acc0, acc1 = accs[(t, 0)], accs[(t, 1)]
        o = (acc0[0:DA_V] / acc0[DA_V:DA_V + 1]
             - lam_ref[...] * (acc1[0:DA_V] / acc1[DA_V:DA_V + 1]))
        y = o * lax.rsqrt(jnp.mean(o * o, axis=0, keepdims=True) + EPS)
        o_ref[0, :, t * TQ:(t + 1) * TQ] = (y * sub_ref[...] * out_scale).astype(BF16)


def _attention_call(dqT, dk, dvT, lam_l, subln_l, chunks, n_q, q_block0, n_steps, lam_init):
    B, _, NT = dqT.shape
    w = n_q * TQ
    kern = functools.partial(_attn_kernel, chunks=tuple(chunks), n_q=n_q, out_scale=1.0 - lam_init)
    return pl.pallas_call(
        kern,
        grid=(B, DA_HEADS, n_steps),
        in_specs=[
            pl.BlockSpec((1, 2 * DA_QK, w), lambda b, h, i: (b, h, q_block0 + i)),
            pl.BlockSpec((1, NT, 2 * DA_V), lambda b, h, i: (b, 0, h // 2)),
            pl.BlockSpec((1, DA_V, NT), lambda b, h, i: (b, h, 0)),
            pl.BlockSpec((1, TQ), lambda b, h, i: (0, 0)),
            pl.BlockSpec((DA_V, 1), lambda b, h, i: (0, 0)),
        ],
        out_specs=pl.BlockSpec((1, DA_V, w), lambda b, h, i: (b, h, i)),
        out_shape=jax.ShapeDtypeStruct((B, DA_WIDTH, n_steps * w), BF16),
        scratch_shapes=[pltpu.VMEM((2 * n_q, len(chunks) * KC, TQ), F32)],
        compiler_params=_cparams(("arbitrary", "arbitrary", "arbitrary")),
        name="diff_attention",
    )(dqT, dk, dvT, lam_l, subln_l)


def _attention(dqT, dk, dvT, lam_l, subln_l, n_lat_rows, lam_init, with_ctx):
    NT = dqT.shape[2]
    n_chunks, n_lat_chunks = NT // KC, n_lat_rows // KC
    n_ctx_rows = NT - n_lat_rows
    lat = _attention_call(dqT, dk, dvT, lam_l, subln_l, range(n_chunks), ATT_TILES, 0,
                          n_lat_rows // (ATT_TILES * TQ), lam_init)
    if not with_ctx:
        return lat, lat
    ctx = _attention_call(dqT, dk, dvT, lam_l, subln_l, range(n_lat_chunks, n_chunks), 1,
                          n_lat_rows // TQ, n_ctx_rows // TQ, lam_init)
    return lat, ctx


def _fourier_kernel(ct_ref, st_ref, f_ref, o_ref):
    o_ref[...] = (jnp.dot(ct_ref[...], f_ref[0], preferred_element_type=F32)
                  + jnp.dot(st_ref[...], f_ref[1], preferred_element_type=F32)).astype(BF16)


def _fourier(ct, st_neg, f, row_block, n_rows):
    W = f.shape[2]
    tk = min(TK, n_rows)
    return pl.pallas_call(
        _fourier_kernel,
        grid=(n_rows // tk,),
        in_specs=[
            pl.BlockSpec((tk, n_rows), lambda i: (i, 0)),
            pl.BlockSpec((tk, n_rows), lambda i: (i, 0)),
            pl.BlockSpec((2, n_rows, W), lambda i: (0, row_block, 0)),
        ],
        out_specs=pl.BlockSpec((tk, W), lambda i: (i, 0)),
        out_shape=jax.ShapeDtypeStruct((n_rows, W), BF16),
        compiler_params=_cparams(("arbitrary",)),
        name="fourier_mix",
    )(ct, st_neg, f)


def _post_kernel(x_ref, mod_ref, g_ref, of_ref, gate_ref, dal_ref, dac_ref, ftl_ref, ftc_ref, onorm_ref,
                 wo_ref, wfi_ref, wfo_ref, o_ref, *, n_lat_tiles):
    x = x_ref[0]
    mod = mod_ref[0]
    is_lat = pl.program_id(1) < n_lat_tiles
    ft = jnp.where(is_lat, ftl_ref[...], ftc_ref[...])
    da_t = jnp.where(is_lat, dal_ref[0], dac_ref[0])
    o = of_ref[0, 0].astype(F32) + of_ref[1, 0].astype(F32)
    onorm = onorm_ref[...]
    heads = [_rms(o[:, h * HG_DIM:(h + 1) * HG_DIM]) * onorm for h in range(HG_HEADS)]
    hg = (jnp.concatenate(heads, axis=-1) * gate_ref[0].astype(F32)).astype(BF16)
    y = (jnp.dot(hg, wo_ref[0:HG_WIDTH, :], preferred_element_type=F32)
         + lax.dot_general(da_t, wo_ref[HG_WIDTH:HG_WIDTH + DA_WIDTH, :], (((0,), (0,)), ((), ())),
                           preferred_element_type=F32)
         + jnp.dot(ft, wo_ref[HG_WIDTH + DA_WIDTH:, :], preferred_element_type=F32))
    x1 = x + mod[2:3, :] * (_rms(y) * g_ref[1:2, :])
    h2 = ((_rms(x1) * g_ref[2:3, :]) * (1.0 + mod[4:5, :]) + mod[3:4, :]).astype(BF16)
    gu = jnp.dot(h2, wfi_ref[...], preferred_element_type=F32)
    act = (_silu(gu[:, :FF_HIDDEN]) * gu[:, FF_HIDDEN:]).astype(BF16)
    y2 = jnp.dot(act, wfo_ref[...], preferred_element_type=F32)
    o_ref[0] = x1 + mod[5:6, :] * (_rms(y2) * g_ref[3:4, :])


def _post(X, mod_l, g_l, of, gate, da_lat, da_ctx, ft_lat, ft_ctx, onorm_l, wo, wfi, wfo, n_lat_tiles, n_tiles):
    B, NT, D = X.shape
    tok = lambda w: pl.BlockSpec((1, TM, w), lambda b, i: (b, i, 0))
    const = lambda a: pl.BlockSpec(a.shape, lambda b, i: (0,) * a.ndim, pipeline_mode=pl.Buffered(1))
    return pl.pallas_call(
        functools.partial(_post_kernel, n_lat_tiles=n_lat_tiles),
        grid=(B, n_tiles),
        in_specs=[
            tok(D),
            pl.BlockSpec((1, N_MOD, D), lambda b, i: (jnp.where(i < n_lat_tiles, b, B), 0, 0)),
            const(g_l),
            pl.BlockSpec((2, 1, TM, HG_WIDTH), lambda b, i: (0, b, i, 0)),
            tok(HG_WIDTH),
            pl.BlockSpec((1, DA_WIDTH, TM), lambda b, i: (b, 0, jnp.minimum(i, n_lat_tiles - 1))),
            pl.BlockSpec((1, DA_WIDTH, TM), lambda b, i: (b, 0, jnp.maximum(i - n_lat_tiles, 0))),
            pl.BlockSpec((TM, FT_WIDTH), lambda b, i: (jnp.minimum(i, n_lat_tiles - 1), b)),
            pl.BlockSpec((TM, FT_WIDTH), lambda b, i: (jnp.maximum(i - n_lat_tiles, 0), b)),
            const(onorm_l), const(wo), const(wfi), const(wfo),
        ],
        out_specs=tok(D),
        out_shape=jax.ShapeDtypeStruct((B, n_tiles * TM, D), F32),
        compiler_params=_cparams(("arbitrary", "arbitrary")),
        name="outproj_ffn",
    )(X, mod_l, g_l, of, gate, da_lat, da_ctx, ft_lat, ft_ctx, onorm_l, wo, wfi, wfo)


def _rope_tables(t_lat, t_ctx):
    pos = jnp.arange(t_lat)
    inv_freq = 1.0 / (ROPE_THETA ** (jnp.arange(0, ROPE_AXIS_DIM, 2, dtype=F32) / ROPE_AXIS_DIM))
    ang = jnp.stack([pos // GRID_W, pos % GRID_W], axis=-1).astype(F32)[:, :, None] * inv_freq
    cos, sin = jnp.cos(ang), jnp.sin(ang)
    cos32 = jnp.stack([cos, cos], axis=2).reshape(t_lat, DA_QK)
    sin32 = jnp.stack([-sin, sin], axis=2).reshape(t_lat, DA_QK)
    reps = DA_WIDTH // DA_QK
    cos_t = jnp.concatenate([jnp.tile(cos32, (1, reps)), jnp.ones((t_ctx, DA_WIDTH), F32)], axis=0)
    sin_t = jnp.concatenate([jnp.tile(sin32, (1, reps)), jnp.zeros((t_ctx, DA_WIDTH), F32)], axis=0)
    return cos_t, sin_t


def _channel_table():
    idx = np.arange(FT_GDIM)
    ang = 2.0 * np.pi * ((idx[:, None] * idx[None, :]) % FT_GDIM) / FT_GDIM
    eye = np.eye(FT_GROUPS)
    return jnp.asarray(np.concatenate([np.kron(eye, np.cos(ang)), np.kron(eye, np.sin(ang))], axis=1), BF16)


def _position_tables(n):
    lo = 1
    while lo * lo < n:
        lo *= 2
    hi = n // lo
    k = jnp.arange(n)
    a1 = 2.0 * np.pi * ((k[:, None] * jnp.arange(hi)[None, :]) % hi).astype(F32) / hi
    a2 = 2.0 * np.pi * ((k[:, None] * jnp.arange(lo)[None, :]) % n).astype(F32) / n
    t = np.arange(n)
    c1, s1 = jnp.cos(a1)[:, t // lo], jnp.sin(a1)[:, t // lo]
    c2, s2 = jnp.cos(a2)[:, t % lo], jnp.sin(a2)[:, t % lo]
    scale = 1.0 / math.sqrt(n * FT_GDIM)
    ct = ((c1 * c2 - s1 * s2) * scale).astype(BF16)
    st_neg = ((s1 * c2 + c1 * s2) * (-scale)).astype(BF16)
    return ct, st_neg


def kernel(x, c, ctx, c_ctx, w_mod, b_mod, norm_g, w_in, w_out, hg_lb_logits, hg_onorm,
           da_lambda, da_subln, w_ffn_in, w_ffn_out):
    B, T, D = x.shape
    Tc = ctx.shape[1]
    depth = w_mod.shape[0]
    assert D == D_MODEL and B + 1 <= MOD_ROWS
    assert T % TM == 0 and Tc % TM == 0 and T % KC == 0 and Tc % KC == 0 and T % CH == 0 and Tc % CH == 0
    n_lat_tiles = T // TM
    NT = T + Tc

    lam_init = [0.8 - 0.6 * math.exp(-0.3 * l) for l in range(depth)]
    lb, lam = _prep(hg_lb_logits, da_lambda, lam_init)

    cond = jnp.concatenate([c.astype(F32), c_ctx.astype(F32)[None, :],
                            jnp.zeros((MOD_ROWS - B - 1, D), F32)], axis=0)
    mods = _modulation(cond, w_mod, b_mod).reshape(depth, MOD_ROWS, N_MOD, D)

    cos_t, sin_t = _rope_tables(T, Tc)
    cs_tab = _channel_table()
    ct_lat, st_lat = _position_tables(T)
    ct_ctx, st_ctx = _position_tables(Tc)

    X = jnp.concatenate([x, ctx], axis=1).astype(F32)
    for l in range(depth):
        mod_l = mods[l]
        g_l = norm_g[l].astype(F32)
        q, vi, gate, lf, dq, dk, dv, f = _inproj(
            X, mod_l, g_l, w_in[l].astype(BF16), cos_t, sin_t, lb[l], cs_tab, n_lat_tiles)
        of = _hgrn(q, vi, lf, T)
        with_ctx = l < depth - 1
        da_lat, da_ctx = _attention(dq, dk, dv, lam[l:l + 1], da_subln[l].astype(F32).reshape(DA_V, 1), T,
                                    lam_init[l], with_ctx)
        ft_lat = _fourier(ct_lat, st_lat, f, 0, T)
        ft_ctx = _fourier(ct_ctx, st_ctx, f, T // Tc, Tc) if with_ctx else ft_lat
        X = _post(X, mod_l, g_l, of, gate, da_lat, da_ctx, ft_lat, ft_ctx, hg_onorm[l].astype(F32).reshape(1, HG_DIM),
                  w_out[l].astype(BF16), w_ffn_in[l].astype(BF16), w_ffn_out[l].astype(BF16), n_lat_tiles,
                  NT // TM if with_ctx else n_lat_tiles)
    return X.astype(x.dtype)
```

```python
import functools
import math

import numpy as np
import jax
import jax.numpy as jnp
from jax import lax
from jax.experimental import pallas as pl
from jax.experimental.pallas import tpu as pltpu

F32 = jnp.float32
BF16 = jnp.bfloat16

D_MODEL = 1024
GRID_W = 64
HG_WIDTH = 512
HG_DIM = 128
HG_HEADS = 4
DA_WIDTH = 256
DA_HEADS = 4
DA_V = 64
DA_QK = 32
FT_WIDTH = 256
FT_GROUPS = 4
FT_GDIM = FT_WIDTH // FT_GROUPS
FF_HIDDEN = 2816
N_MOD = 6
EPS = 1e-6
ROPE_THETA = 10000.0
ROPE_AXIS_DIM = DA_QK // 2

_OFF_Q, _OFF_I, _OFF_G, _OFF_F, _OFF_DQ, _OFF_DK, _OFF_DV, _OFF_FT, IN_WIDTH = (
    0, 512, 1024, 1536, 2560, 2816, 3072, 3328, 3584)

TM = 256
TQ = 256
ATT_TILES = 2
KC = 256
CH = 128
SUB = 8
HGRN_UNROLL = 17
TK = 256
VT_ROWS = 80
MOD_ROWS = 8
MOD_TN = 1536
VMEM_LIMIT = 52 * 1024 * 1024

Q_SCALE = (DA_QK ** -0.5) * math.log2(math.e)


def _silu(x):
    return x * jax.nn.sigmoid(x)


def _rms(x):
    return x * lax.rsqrt(jnp.mean(x * x, axis=-1, keepdims=True) + EPS)


def _cparams(sem):
    return pltpu.CompilerParams(dimension_semantics=sem, vmem_limit_bytes=VMEM_LIMIT)


def _mod_kernel(c_ref, w_ref, b_ref, o_ref):
    a = _silu(c_ref[...]).astype(BF16)
    w = w_ref[0].astype(BF16)
    o_ref[0] = jnp.dot(a, w, preferred_element_type=F32) + b_ref[0]


def _modulation(cond, w_mod, b_mod):
    depth, d, n = w_mod.shape
    return pl.pallas_call(
        _mod_kernel,
        grid=(depth, n // MOD_TN),
        in_specs=[
            pl.BlockSpec((MOD_ROWS, d), lambda l, j: (0, 0)),
            pl.BlockSpec((1, d, MOD_TN), lambda l, j: (l, 0, j)),
            pl.BlockSpec((1, 1, MOD_TN), lambda l, j: (l, 0, j)),
        ],
        out_specs=pl.BlockSpec((1, MOD_ROWS, MOD_TN), lambda l, j: (l, 0, j)),
        out_shape=jax.ShapeDtypeStruct((depth, MOD_ROWS, n), F32),
        compiler_params=_cparams(("arbitrary", "arbitrary")),
        name="modulation",
    )(cond, w_mod, b_mod.reshape(depth, 1, n))


def _prep_kernel(lb_ref, lam_ref, lam_init_ref, lbo_ref, lamo_ref, *, depth):
    rows = [lb_ref[l:l + 1, :] for l in range(depth)]
    m = rows[0]
    for r in rows[1:]:
        m = jnp.maximum(m, r)
    e = [jnp.exp(r - m) for r in rows]
    tot = e[0]
    for r in e[1:]:
        tot = tot + r
    p = [r / tot for r in e]
    acc = p[0]
    lbo_ref[0:1, :] = acc - p[0]
    for l in range(1, depth):
        acc = acc + p[l]
        lbo_ref[l:l + 1, :] = acc - p[0]
    x = lam_ref[...]
    a = jnp.sum(x[:, 0:DA_QK] * x[:, DA_QK:2 * DA_QK], axis=-1, keepdims=True)
    b = jnp.sum(x[:, 2 * DA_QK:3 * DA_QK] * x[:, 3 * DA_QK:4 * DA_QK], axis=-1, keepdims=True)
    lamo_ref[...] = jnp.exp(a) - jnp.exp(b) + lam_init_ref[...]


def _prep(hg_lb_logits, da_lambda, lam_init):
    depth = hg_lb_logits.shape[1]
    lb_in = jnp.transpose(hg_lb_logits.astype(F32), (1, 0, 2)).reshape(depth, 2 * HG_WIDTH)
    lam_in = da_lambda.astype(F32).reshape(depth, 4 * DA_QK)
    lam_init_arr = jnp.asarray(np.broadcast_to(np.asarray(lam_init, np.float32)[:, None], (depth, TQ)))
    lb, lam = pl.pallas_call(
        functools.partial(_prep_kernel, depth=depth),
        out_shape=(jax.ShapeDtypeStruct((depth, 2 * HG_WIDTH), F32),
                   jax.ShapeDtypeStruct((depth, TQ), F32)),
        name="param_prep",
    )(lb_in, lam_in, lam_init_arr)
    return lb.reshape(depth, 2, HG_WIDTH), lam


def _inproj_kernel(x_ref, mod_ref, g_ref, w_ref, cos_ref, sin_ref, lb_ref, cs_ref,
                   q_ref, i_ref, gate_ref, lf_ref, dq_ref, dk_ref, dv_ref, f_ref):
    x = x_ref[0]
    mod = mod_ref[0]
    h = (_rms(x) * g_ref[0:1, :]) * (1.0 + mod[1:2, :]) + mod[0:1, :]
    hb = h.astype(BF16)

    def proj(a, b):
        return jnp.dot(hb, w_ref[0, :, a:b], preferred_element_type=F32)

    q_ref[0] = _silu(proj(_OFF_Q, _OFF_I)).astype(BF16)
    i_ref[0] = proj(_OFF_I, _OFF_G).astype(BF16)
    gate_ref[0] = _silu(proj(_OFF_G, _OFF_F)).astype(BF16)
    for d in range(2):
        z = proj(_OFF_F + d * HG_WIDTH, _OFF_F + (d + 1) * HG_WIDTH)
        lb = lb_ref[d:d + 1, :]
        lf_ref[d, 0] = jnp.log(lb + (1.0 - lb) * jax.nn.sigmoid(z))

    cos = cos_ref[...]
    sin = sin_ref[...]
    lane = lax.broadcasted_iota(jnp.int32, cos.shape, 1)
    upper_half = (lane & (ROPE_AXIS_DIM // 2)) != 0

    def rope(t):
        partner = jnp.where(upper_half,
                            pltpu.roll(t, ROPE_AXIS_DIM // 2, 1),
                            pltpu.roll(t, 2 * DA_HEADS * DA_QK - ROPE_AXIS_DIM // 2, 1))
        return t * cos + partner * sin

    dq_ref[0] = (rope(proj(_OFF_DQ, _OFF_DK)) * Q_SCALE).T.astype(BF16)
    dk_ref[0] = rope(proj(_OFF_DK, _OFF_DV)).astype(BF16)
    dv_ref[0] = proj(_OFF_DV, _OFF_FT).T.astype(BF16)
    ft = proj(_OFF_FT, IN_WIDTH).astype(BF16)
    cs = jnp.dot(ft, cs_ref[...], preferred_element_type=F32)
    f_ref[0] = cs[:, :FT_WIDTH].astype(BF16)
    f_ref[1] = cs[:, FT_WIDTH:].astype(BF16)


def _inproj(X, mod_l, g_l, w_in, layer, cos_t, sin_t, lb_l, cs_tab, n_lat_tiles):
    B, NT, D = X.shape
    nt = NT // TM
    tok = lambda w: pl.BlockSpec((1, TM, w), lambda b, i: (b, i, 0))
    full = lambda a: pl.BlockSpec(a.shape, lambda b, i: (0,) * a.ndim)
    tok_t = pl.BlockSpec((1, DA_WIDTH, TM), lambda b, i: (b, 0, i))
    out_shapes = (
        jax.ShapeDtypeStruct((B, NT, HG_WIDTH), BF16),
        jax.ShapeDtypeStruct((B, NT, HG_WIDTH), BF16),
        jax.ShapeDtypeStruct((B, NT, HG_WIDTH), BF16),
        jax.ShapeDtypeStruct((2, B, NT, HG_WIDTH), F32),
        jax.ShapeDtypeStruct((B, DA_WIDTH, NT), BF16),
        jax.ShapeDtypeStruct((B, NT, DA_WIDTH), BF16),
        jax.ShapeDtypeStruct((B, DA_WIDTH, NT), BF16),
        jax.ShapeDtypeStruct((2, NT, B * FT_WIDTH), BF16),
    )
    out_specs = (
        tok(HG_WIDTH), tok(HG_WIDTH), tok(HG_WIDTH),
        pl.BlockSpec((2, 1, TM, HG_WIDTH), lambda b, i: (0, b, i, 0)),
        tok_t, tok(DA_WIDTH), tok_t,
        pl.BlockSpec((2, TM, FT_WIDTH), lambda b, i: (0, i, b)),
    )
    return pl.pallas_call(
        _inproj_kernel,
        grid=(B, nt),
        in_specs=[
            tok(D),
            pl.BlockSpec((1, N_MOD, D), lambda b, i: (jnp.where(i < n_lat_tiles, b, B), 0, 0)),
            full(g_l),
            pl.BlockSpec((1,) + w_in.shape[1:], lambda b, i: (layer, 0, 0)),
            pl.BlockSpec((TM, DA_WIDTH), lambda b, i: (i, 0)),
            pl.BlockSpec((TM, DA_WIDTH), lambda b, i: (i, 0)),
            full(lb_l), full(cs_tab),
        ],
        out_specs=out_specs,
        out_shape=out_shapes,
        compiler_params=_cparams(("arbitrary", "arbitrary")),
        name="adaln_inproj",
    )(X, mod_l, g_l, w_in, cos_t, sin_t, lb_l, cs_tab)


def _hgrn_direction(q, f, v, g, st, cmat, codes, rev):
    C, K = q.shape
    k = 1.0 - f
    g_hi = g.astype(BF16)
    g_lo = (g - g_hi.astype(F32)).astype(BF16)
    cs = jnp.dot(cmat, jnp.concatenate([g_hi, g_lo], axis=1), preferred_element_type=F32)
    bc = cs[:, :K] + cs[:, K:]
    tot = bc[0:1, :] if rev else bc[C - 1:C, :]

    nt_dims = (((1,), (1,)), ((), ()))
    tn_dims = (((0,), (0,)), ((), ()))
    qd = (q * jnp.exp(bc)).astype(BF16)
    o = lax.dot_general(qd, st.astype(BF16), nt_dims, preferred_element_type=F32)
    kdec = (k * jnp.exp(tot - bc)).astype(BF16)
    st_new = st * jnp.exp(tot) + lax.dot_general(v, kdec, tn_dims, preferred_element_type=F32)

    a = jnp.zeros((C, C), F32)
    f3 = f.reshape(C // SUB, SUB, K)
    qd_d = q
    for d8 in range(SUB):
        f_shift = f if d8 == 0 else pltpu.roll(f3, (SUB - d8) if rev else d8, 1).reshape(C, K)
        qd_next = qd_d * f_shift
        red = jnp.sum(qd_d - qd_next, axis=-1, keepdims=True)
        a = jnp.where(codes == d8, red, a)
        qd_d = qd_next

    zeros8 = jnp.zeros((SUB, K), F32)
    b = SUB
    level = 0
    while b < C:
        ql, kl = [], []
        for r in range(0, C, SUB):
            base = (r // (2 * b)) * 2 * b
            upper = (r - base) >= b
            ref_row = base + b if rev else base + b - 1
            q_side = (not upper) if rev else upper
            bref = bc[ref_row:ref_row + 1, :]
            bg = bc[r:r + SUB, :]
            if q_side:
                ql.append(q[r:r + SUB, :] * jnp.exp(bg - bref))
                kl.append(zeros8)
            else:
                kl.append(k[r:r + SUB, :] * jnp.exp(bref - bg))
                ql.append(zeros8)
        qlb = jnp.concatenate(ql, axis=0).astype(BF16)
        klb = jnp.concatenate(kl, axis=0).astype(BF16)
        p = lax.dot_general(qlb, klb, nt_dims, preferred_element_type=F32)
        a = jnp.where(codes == SUB + level, p, a)
        b *= 2
        level += 1

    o = o + jnp.dot(a.astype(BF16), v, preferred_element_type=F32)
    return o, st_new


def _hgrn_tables(C):
    t = np.arange(C)[:, None]
    s = np.arange(C)[None, :]
    cmats, codes = [], []
    for rev in (False, True):
        d = (s - t) if rev else (t - s)
        cmats.append((d >= 0).astype(np.float32))
        code = np.full((C, C), -1, np.int32)
        inside = (d >= 0) & ((t // SUB) == (s // SUB))
        code[inside] = d[inside]
        b, level = SUB, 0
        while b < C:
            split = (d > 0) & ((t // (2 * b)) == (s // (2 * b))) & ((t // b) != (s // b))
            code[split] = SUB + level
            b *= 2
            level += 1
        codes.append(code)
    return jnp.asarray(np.stack(cmats), BF16), jnp.asarray(np.stack(codes), jnp.int32)


def _hgrn_kernel(q_ref, v_ref, lf_ref, cmat_ref, code_ref, o_ref, st_ref, *, n_lat, n_ctx):
    C = CH
    n_chunks = n_lat + n_ctx
    st_ref[...] = jnp.zeros(st_ref.shape, F32)

    def body(j, carry):
        for d, rev in enumerate((False, True)):
            c = (n_chunks - 1 - j) if rev else jnp.where(j < n_ctx, n_lat + j, j - n_ctx)
            r0 = pl.multiple_of(c * C, C)
            q = q_ref[0, pl.ds(r0, C), :].astype(F32)
            v = v_ref[0, pl.ds(r0, C), :]
            g = lf_ref[d, 0, pl.ds(r0, C), :]
            o, st_new = _hgrn_direction(q, jnp.exp(g), v, g, st_ref[d], cmat_ref[d], code_ref[d], rev)
            st_ref[d] = st_new
            o_ref[d, 0, pl.ds(r0, C), :] = o.astype(o_ref.dtype)
        return carry

    lax.fori_loop(0, n_chunks, body, 0, unroll=HGRN_UNROLL)


def _hgrn(q, v, lf, n_lat_rows):
    B, NT, _ = q.shape
    n_lat = n_lat_rows // CH
    n_ctx = (NT - n_lat_rows) // CH
    assert (n_lat + n_ctx) % HGRN_UNROLL == 0
    cmat, codes = _hgrn_tables(CH)
    head = pl.BlockSpec((1, NT, HG_DIM), lambda b, h: (b, 0, h))
    both = pl.BlockSpec((2, 1, NT, HG_DIM), lambda b, h: (0, b, 0, h))
    table = pl.BlockSpec((2, CH, CH), lambda b, h: (0, 0, 0))
    return pl.pallas_call(
        functools.partial(_hgrn_kernel, n_lat=n_lat, n_ctx=n_ctx),
        grid=(B, HG_HEADS),
        in_specs=[head, head, both, table, table],
        out_specs=both,
        out_shape=jax.ShapeDtypeStruct((2, B, NT, HG_WIDTH), BF16),
        scratch_shapes=[pltpu.VMEM((2, HG_DIM, HG_DIM), F32)],
        compiler_params=_cparams(("arbitrary", "arbitrary")),
        name="hgrn_scan",
    )(q, v, lf, cmat, codes)


def _attn_kernel(qT_ref, k_ref, vT_ref, lam_ref, sub_ref, o_ref, s_ref, *, chunks, n_q, out_scale):
    n = len(chunks)
    odd_head = (pl.program_id(1) % 2) == 1
    ones_row = lax.broadcasted_iota(jnp.int32, (VT_ROWS - DA_V, n * KC), 0) == 0
    vt = jnp.concatenate(
        [vT_ref[0, :, chunks[0] * KC:(chunks[-1] + 1) * KC],
         jnp.where(ones_row, 1.0, 0.0).astype(BF16)], axis=0)
    streams = [(t, mp) for t in range(n_q) for mp in range(2)]
    zeros = jnp.zeros((DA_QK, TQ), BF16)

    def query_operand(t, mp):
        q = qT_ref[0, mp * DA_QK:(mp + 1) * DA_QK, t * TQ:(t + 1) * TQ]
        even = jnp.concatenate([q if j == mp else zeros for j in range(4)], axis=0)
        odd = jnp.concatenate([q if j == 2 + mp else zeros for j in range(4)], axis=0)
        return jnp.where(odd_head, odd, even)

    slot0 = jnp.minimum(pl.program_id(2), 0)

    def pass1(slot):
        rhs = query_operand(*streams[slot])
        m8 = None
        for i, c in enumerate(chunks):
            s = jnp.dot(k_ref[0, c * KC:(c + 1) * KC, :], rhs, preferred_element_type=F32)
            s_ref[slot, i * KC:(i + 1) * KC, :] = s
            parts = [s[r:r + 8, :] for r in range(0, KC, 8)]
            while len(parts) > 1:
                parts = [jnp.maximum(parts[j], parts[j + 1]) for j in range(0, len(parts), 2)]
            m8 = parts[0] if m8 is None else jnp.maximum(m8, parts[0])
        return jnp.max(m8, axis=0, keepdims=True)

    def pass2(slot, m):
        p = jnp.exp2((s_ref[slot0 + slot] - m).astype(BF16))
        return jnp.dot(vt, p, preferred_element_type=F32)

    maxes = [pass1(slot) for slot in range(len(streams))]
    accs = {st: pass2(slot, maxes[slot]) for slot, st in enumerate(streams)}

    for t in range(n_q):
        acc0, acc1 = accs[(t, 0)], accs[(t, 1)]
        o = (acc0[0:DA_V] / acc0[DA_V:DA_V + 1]
             - lam_ref[...] * (acc1[0:DA_V] / acc1[DA_V:DA_V + 1]))
        y = o * lax.rsqrt(jnp.mean(o * o, axis=0, keepdims=True) + EPS)
        o_ref[0, :, t * TQ:(t + 1) * TQ] = (y * sub_ref[...] * out_scale).astype(BF16)


def _attention_call(dqT, dk, dvT, lam_l, subln_l, chunks, n_q, q_block0, n_steps, lam_init):
    B, _, NT = dqT.shape
    w = n_q * TQ
    kern = functools.partial(_attn_kernel, chunks=tuple(chunks), n_q=n_q, out_scale=1.0 - lam_init)
    return pl.pallas_call(
        kern,
        grid=(B, DA_HEADS, n_steps),
        in_specs=[
            pl.BlockSpec((1, 2 * DA_QK, w), lambda b, h, i: (b, h, q_block0 + i)),
            pl.BlockSpec((1, NT, 2 * DA_V), lambda b, h, i: (b, 0, h // 2)),
            pl.BlockSpec((1, DA_V, NT), lambda b, h, i: (b, h, 0)),
            pl.BlockSpec((1, TQ), lambda b, h, i: (0, 0)),
            pl.BlockSpec((DA_V, 1), lambda b, h, i: (0, 0)),
        ],
        out_specs=pl.BlockSpec((1, DA_V, w), lambda b, h, i: (b, h, i)),
        out_shape=jax.ShapeDtypeStruct((B, DA_WIDTH, n_steps * w), BF16),
        scratch_shapes=[pltpu.VMEM((2 * n_q, len(chunks) * KC, TQ), F32)],
        compiler_params=_cparams(("arbitrary", "arbitrary", "arbitrary")),
        name="diff_attention",
    )(dqT, dk, dvT, lam_l, subln_l)


def _attention(dqT, dk, dvT, lam_l, subln_l, n_lat_rows, lam_init, with_ctx):
    NT = dqT.shape[2]
    n_chunks, n_lat_chunks = NT // KC, n_lat_rows // KC
    n_ctx_rows = NT - n_lat_rows
    lat = _attention_call(dqT, dk, dvT, lam_l, subln_l, range(n_chunks), ATT_TILES, 0,
                          n_lat_rows // (ATT_TILES * TQ), lam_init)
    if not with_ctx:
        return lat, lat
    ctx = _attention_call(dqT, dk, dvT, lam_l, subln_l, range(n_lat_chunks, n_chunks), 1,
                          n_lat_rows // TQ, n_ctx_rows // TQ, lam_init)
    return lat, ctx


def _fourier_kernel(ct_ref, st_ref, f_ref, o_ref):
    o_ref[...] = (jnp.dot(ct_ref[...], f_ref[0], preferred_element_type=F32)
                  + jnp.dot(st_ref[...], f_ref[1], preferred_element_type=F32)).astype(BF16)


def _fourier(ct, st_neg, f, row_block, n_rows):
    W = f.shape[2]
    tk = min(TK, n_rows)
    return pl.pallas_call(
        _fourier_kernel,
        grid=(n_rows // tk,),
        in_specs=[
            pl.BlockSpec((tk, n_rows), lambda i: (i, 0)),
            pl.BlockSpec((tk, n_rows), lambda i: (i, 0)),
            pl.BlockSpec((2, n_rows, W), lambda i: (0, row_block, 0)),
        ],
        out_specs=pl.BlockSpec((tk, W), lambda i: (i, 0)),
        out_shape=jax.ShapeDtypeStruct((n_rows, W), BF16),
        compiler_params=_cparams(("arbitrary",)),
        name="fourier_mix",
    )(ct, st_neg, f)


def _post_kernel(x_ref, mod_ref, g_ref, of_ref, gate_ref, dal_ref, dac_ref, ftl_ref, ftc_ref, onorm_ref,
                 wo_ref, wfi_ref, wfo_ref, o_ref, *, n_lat_tiles):
    x = x_ref[0]
    mod = mod_ref[0]
    is_lat = pl.program_id(1) < n_lat_tiles
    ft = jnp.where(is_lat, ftl_ref[...], ftc_ref[...])
    da_t = jnp.where(is_lat, dal_ref[0], dac_ref[0])
    o = of_ref[0, 0].astype(F32) + of_ref[1, 0].astype(F32)
    onorm = onorm_ref[...]
    heads = [_rms(o[:, h * HG_DIM:(h + 1) * HG_DIM]) * onorm for h in range(HG_HEADS)]
    hg = (jnp.concatenate(heads, axis=-1) * gate_ref[0].astype(F32)).astype(BF16)
    y = (jnp.dot(hg, wo_ref[0, 0:HG_WIDTH, :], preferred_element_type=F32)
         + lax.dot_general(da_t, wo_ref[0, HG_WIDTH:HG_WIDTH + DA_WIDTH, :], (((0,), (0,)), ((), ())),
                           preferred_element_type=F32)
         + jnp.dot(ft, wo_ref[0, HG_WIDTH + DA_WIDTH:, :], preferred_element_type=F32))
    x1 = x + mod[2:3, :] * (_rms(y) * g_ref[1:2, :])
    h2 = ((_rms(x1) * g_ref[2:3, :]) * (1.0 + mod[4:5, :]) + mod[3:4, :]).astype(BF16)
    gu = jnp.dot(h2, wfi_ref[0], preferred_element_type=F32)
    act = (_silu(gu[:, :FF_HIDDEN]) * gu[:, FF_HIDDEN:]).astype(BF16)
    y2 = jnp.dot(act, wfo_ref[0], preferred_element_type=F32)
    o_ref[0] = x1 + mod[5:6, :] * (_rms(y2) * g_ref[3:4, :])


def _post(X, mod_l, g_l, of, gate, da_lat, da_ctx, ft_lat, ft_ctx, onorm_l, wo, wfi, wfo, layer, n_lat_tiles, n_tiles):
    B, NT, D = X.shape
    tok = lambda w: pl.BlockSpec((1, TM, w), lambda b, i: (b, i, 0))
    const = lambda a: pl.BlockSpec(a.shape, lambda b, i: (0,) * a.ndim, pipeline_mode=pl.Buffered(1))
    weight = lambda a: pl.BlockSpec((1,) + a.shape[1:], lambda b, i: (layer, 0, 0), pipeline_mode=pl.Buffered(1))
    return pl.pallas_call(
        functools.partial(_post_kernel, n_lat_tiles=n_lat_tiles),
        grid=(B, n_tiles),
        in_specs=[
            tok(D),
            pl.BlockSpec((1, N_MOD, D), lambda b, i: (jnp.where(i < n_lat_tiles, b, B), 0, 0)),
            const(g_l),
            pl.BlockSpec((2, 1, TM, HG_WIDTH), lambda b, i: (0, b, i, 0)),
            tok(HG_WIDTH),
            pl.BlockSpec((1, DA_WIDTH, TM), lambda b, i: (b, 0, jnp.minimum(i, n_lat_tiles - 1))),
            pl.BlockSpec((1, DA_WIDTH, TM), lambda b, i: (b, 0, jnp.maximum(i - n_lat_tiles, 0))),
            pl.BlockSpec((TM, FT_WIDTH), lambda b, i: (jnp.minimum(i, n_lat_tiles - 1), b)),
            pl.BlockSpec((TM, FT_WIDTH), lambda b, i: (jnp.maximum(i - n_lat_tiles, 0), b)),
            const(onorm_l), weight(wo), weight(wfi), weight(wfo),
        ],
        out_specs=tok(D),
        out_shape=jax.ShapeDtypeStruct((B, n_tiles * TM, D), F32),
        compiler_params=_cparams(("arbitrary", "arbitrary")),
        name="outproj_ffn",
    )(X, mod_l, g_l, of, gate, da_lat, da_ctx, ft_lat, ft_ctx, onorm_l, wo, wfi, wfo)


def _rope_tables(t_lat, t_ctx):
    pos = jnp.arange(t_lat)
    inv_freq = 1.0 / (ROPE_THETA ** (jnp.arange(0, ROPE_AXIS_DIM, 2, dtype=F32) / ROPE_AXIS_DIM))
    ang = jnp.stack([pos // GRID_W, pos % GRID_W], axis=-1).astype(F32)[:, :, None] * inv_freq
    cos, sin = jnp.cos(ang), jnp.sin(ang)
    cos32 = jnp.stack([cos, cos], axis=2).reshape(t_lat, DA_QK)
    sin32 = jnp.stack([-sin, sin], axis=2).reshape(t_lat, DA_QK)
    reps = DA_WIDTH // DA_QK
    cos_t = jnp.concatenate([jnp.tile(cos32, (1, reps)), jnp.ones((t_ctx, DA_WIDTH), F32)], axis=0)
    sin_t = jnp.concatenate([jnp.tile(sin32, (1, reps)), jnp.zeros((t_ctx, DA_WIDTH), F32)], axis=0)
    return cos_t, sin_t


def _channel_table():
    idx = np.arange(FT_GDIM)
    ang = 2.0 * np.pi * ((idx[:, None] * idx[None, :]) % FT_GDIM) / FT_GDIM
    eye = np.eye(FT_GROUPS)
    return jnp.asarray(np.concatenate([np.kron(eye, np.cos(ang)), np.kron(eye, np.sin(ang))], axis=1), BF16)


def _position_tables(n):
    lo = 1
    while lo * lo < n:
        lo *= 2
    hi = n // lo
    k = jnp.arange(n)
    a1 = 2.0 * np.pi * ((k[:, None] * jnp.arange(hi)[None, :]) % hi).astype(F32) / hi
    a2 = 2.0 * np.pi * ((k[:, None] * jnp.arange(lo)[None, :]) % n).astype(F32) / n
    c1, s1 = jnp.cos(a1)[:, :, None], jnp.sin(a1)[:, :, None]
    c2, s2 = jnp.cos(a2)[:, None, :], jnp.sin(a2)[:, None, :]
    scale = 1.0 / math.sqrt(n * FT_GDIM)
    ct = ((c1 * c2 - s1 * s2) * scale).reshape(n, n).astype(BF16)
    st_neg = ((s1 * c2 + c1 * s2) * (-scale)).reshape(n, n).astype(BF16)
    return ct, st_neg


def kernel(x, c, ctx, c_ctx, w_mod, b_mod, norm_g, w_in, w_out, hg_lb_logits, hg_onorm,
           da_lambda, da_subln, w_ffn_in, w_ffn_out):
    B, T, D = x.shape
    Tc = ctx.shape[1]
    depth = w_mod.shape[0]
    assert D == D_MODEL and B + 1 <= MOD_ROWS
    assert T % TM == 0 and Tc % TM == 0 and T % KC == 0 and Tc % KC == 0 and T % CH == 0 and Tc % CH == 0
    n_lat_tiles = T // TM
    NT = T + Tc

    lam_init = [0.8 - 0.6 * math.exp(-0.3 * l) for l in range(depth)]
    lb, lam = _prep(hg_lb_logits, da_lambda, lam_init)

    cond = jnp.concatenate([c.astype(F32), c_ctx.astype(F32)[None, :],
                            jnp.zeros((MOD_ROWS - B - 1, D), F32)], axis=0)
    mods = _modulation(cond, w_mod, b_mod).reshape(depth, MOD_ROWS, N_MOD, D)

    cos_t, sin_t = _rope_tables(T, Tc)
    cs_tab = _channel_table()
    ct_lat, st_lat = _position_tables(T)
    ct_ctx, st_ctx = _position_tables(Tc)

    w_in_b, w_out_b = w_in.astype(BF16), w_out.astype(BF16)
    w_ffn_in_b, w_ffn_out_b = w_ffn_in.astype(BF16), w_ffn_out.astype(BF16)
    X = jnp.concatenate([x, ctx], axis=1).astype(F32)
    for l in range(depth):
        mod_l = mods[l]
        g_l = norm_g[l].astype(F32)
        q, vi, gate, lf, dq, dk, dv, f = _inproj(
            X, mod_l, g_l, w_in_b, l, cos_t, sin_t, lb[l], cs_tab, n_lat_tiles)
        of = _hgrn(q, vi, lf, T)
        with_ctx = l < depth - 1
        da_lat, da_ctx = _attention(dq, dk, dv, lam[l:l + 1], da_subln[l].astype(F32).reshape(DA_V, 1), T,
                                    lam_init[l], with_ctx)
        ft_lat = _fourier(ct_lat, st_lat, f, 0, T)
        ft_ctx = _fourier(ct_ctx, st_ctx, f, T // Tc, Tc) if with_ctx else ft_lat
        X = _post(X, mod_l, g_l, of, gate, da_lat, da_ctx, ft_lat, ft_ctx, hg_onorm[l].astype(F32).reshape(1, HG_DIM),
                  w_out_b, w_ffn_in_b, w_ffn_out_b, l, n_lat_tiles,
                  NT // TM if with_ctx else n_lat_tiles)
    return X.astype(x.dtype)
```

```python
import functools
import math

import numpy as np
import jax
import jax.numpy as jnp
from jax import lax
from jax.experimental import pallas as pl
from jax.experimental.pallas import tpu as pltpu

F32 = jnp.float32
BF16 = jnp.bfloat16

D_MODEL = 1024
GRID_W = 64
HG_WIDTH = 512
HG_DIM = 128
HG_HEADS = 4
DA_WIDTH = 256
DA_HEADS = 4
DA_V = 64
DA_QK = 32
FT_WIDTH = 256
FT_GROUPS = 4
FT_GDIM = FT_WIDTH // FT_GROUPS
FF_HIDDEN = 2816
N_MOD = 6
EPS = 1e-6
ROPE_THETA = 10000.0
ROPE_AXIS_DIM = DA_QK // 2

_OFF_Q, _OFF_I, _OFF_G, _OFF_F, _OFF_DQ, _OFF_DK, _OFF_DV, _OFF_FT, IN_WIDTH = (
    0, 512, 1024, 1536, 2560, 2816, 3072, 3328, 3584)

TM = 256
TQ = 256
ATT_TILES = 2
KC = 256
CH = 128
SUB = 8
HGRN_UNROLL = 17
TK = 256
VT_ROWS = 80
MOD_ROWS = 8
MOD_TN = 1536
VMEM_LIMIT = 52 * 1024 * 1024

Q_SCALE = (DA_QK ** -0.5) * math.log2(math.e)


def _silu(x):
    return x * jax.nn.sigmoid(x)


def _rms(x):
    return x * lax.rsqrt(jnp.mean(x * x, axis=-1, keepdims=True) + EPS)


def _cparams(sem):
    return pltpu.CompilerParams(dimension_semantics=sem, vmem_limit_bytes=VMEM_LIMIT)


def _mod_kernel(c_ref, w_ref, b_ref, o_ref):
    a = _silu(c_ref[...]).astype(BF16)
    w = w_ref[0].astype(BF16)
    o_ref[0] = jnp.dot(a, w, preferred_element_type=F32) + b_ref[0]


def _modulation(cond, w_mod, b_mod):
    depth, d, n = w_mod.shape
    return pl.pallas_call(
        _mod_kernel,
        grid=(depth, n // MOD_TN),
        in_specs=[
            pl.BlockSpec((MOD_ROWS, d), lambda l, j: (0, 0)),
            pl.BlockSpec((1, d, MOD_TN), lambda l, j: (l, 0, j)),
            pl.BlockSpec((1, 1, MOD_TN), lambda l, j: (l, 0, j)),
        ],
        out_specs=pl.BlockSpec((1, MOD_ROWS, MOD_TN), lambda l, j: (l, 0, j)),
        out_shape=jax.ShapeDtypeStruct((depth, MOD_ROWS, n), F32),
        compiler_params=_cparams(("arbitrary", "arbitrary")),
        name="modulation",
    )(cond, w_mod, b_mod.reshape(depth, 1, n))


def _prep_kernel(lb_ref, lam_ref, lam_init_ref, lbo_ref, lamo_ref, *, depth):
    rows = [lb_ref[l:l + 1, :] for l in range(depth)]
    m = rows[0]
    for r in rows[1:]:
        m = jnp.maximum(m, r)
    e = [jnp.exp(r - m) for r in rows]
    tot = e[0]
    for r in e[1:]:
        tot = tot + r
    p = [r / tot for r in e]
    acc = p[0]
    lbo_ref[0:1, :] = acc - p[0]
    for l in range(1, depth):
        acc = acc + p[l]
        lbo_ref[l:l + 1, :] = acc - p[0]
    x = lam_ref[...]
    a = jnp.sum(x[:, 0:DA_QK] * x[:, DA_QK:2 * DA_QK], axis=-1, keepdims=True)
    b = jnp.sum(x[:, 2 * DA_QK:3 * DA_QK] * x[:, 3 * DA_QK:4 * DA_QK], axis=-1, keepdims=True)
    lamo_ref[...] = jnp.exp(a) - jnp.exp(b) + lam_init_ref[...]


def _prep(hg_lb_logits, da_lambda, lam_init):
    depth = hg_lb_logits.shape[1]
    lb_in = jnp.transpose(hg_lb_logits.astype(F32), (1, 0, 2)).reshape(depth, 2 * HG_WIDTH)
    lam_in = da_lambda.astype(F32).reshape(depth, 4 * DA_QK)
    lam_init_arr = jnp.asarray(np.broadcast_to(np.asarray(lam_init, np.float32)[:, None], (depth, TQ)))
    lb, lam = pl.pallas_call(
        functools.partial(_prep_kernel, depth=depth),
        out_shape=(jax.ShapeDtypeStruct((depth, 2 * HG_WIDTH), F32),
                   jax.ShapeDtypeStruct((depth, TQ), F32)),
        name="param_prep",
    )(lb_in, lam_in, lam_init_arr)
    return lb.reshape(depth, 2, HG_WIDTH), lam


def _inproj_kernel(x_ref, *refs):
    _inproj_compute(x_ref[0], *refs)


def _inproj_compute(x, mod_ref, g_ref, w_ref, cos_ref, sin_ref, lb_ref, cs_ref,
                    q_ref, i_ref, gate_ref, lf_ref, dq_ref, dk_ref, dv_ref, f_ref):
    mod = mod_ref[0]
    h = (_rms(x) * g_ref[0:1, :]) * (1.0 + mod[1:2, :]) + mod[0:1, :]
    hb = h.astype(BF16)

    def proj(a, b):
        return jnp.dot(hb, w_ref[0, :, a:b], preferred_element_type=F32)

    q_ref[0] = _silu(proj(_OFF_Q, _OFF_I)).astype(BF16)
    i_ref[0] = proj(_OFF_I, _OFF_G).astype(BF16)
    gate_ref[0] = _silu(proj(_OFF_G, _OFF_F)).astype(BF16)
    for d in range(2):
        z = proj(_OFF_F + d * HG_WIDTH, _OFF_F + (d + 1) * HG_WIDTH)
        lb = lb_ref[d:d + 1, :]
        lf_ref[d, 0] = jnp.log(lb + (1.0 - lb) * jax.nn.sigmoid(z))

    cos = cos_ref[...]
    sin = sin_ref[...]
    lane = lax.broadcasted_iota(jnp.int32, cos.shape, 1)
    upper_half = (lane & (ROPE_AXIS_DIM // 2)) != 0

    def rope(t):
        partner = jnp.where(upper_half,
                            pltpu.roll(t, ROPE_AXIS_DIM // 2, 1),
                            pltpu.roll(t, 2 * DA_HEADS * DA_QK - ROPE_AXIS_DIM // 2, 1))
        return t * cos + partner * sin

    dq_ref[0] = (rope(proj(_OFF_DQ, _OFF_DK)) * Q_SCALE).T.astype(BF16)
    dk_ref[0] = rope(proj(_OFF_DK, _OFF_DV)).astype(BF16)
    dv_ref[0] = proj(_OFF_DV, _OFF_FT).T.astype(BF16)
    ft = proj(_OFF_FT, IN_WIDTH).astype(BF16)
    cs = jnp.dot(ft, cs_ref[...], preferred_element_type=F32)
    f_ref[0] = cs[:, :FT_WIDTH].astype(BF16)
    f_ref[1] = cs[:, FT_WIDTH:].astype(BF16)


def _tok(w):
    return pl.BlockSpec((1, TM, w), lambda b, i: (b, i, 0))


def _mod_spec(B, n_lat_tiles):
    return pl.BlockSpec((1, N_MOD, D_MODEL), lambda b, i: (jnp.where(i < n_lat_tiles, b, B), 0, 0))


def _inproj_io(B, NT, g_l, w_in, layer, lb_l, cs_tab, n_lat_tiles, single):
    tok = _tok
    mode = dict(pipeline_mode=pl.Buffered(1)) if single else {}
    full = lambda a: pl.BlockSpec(a.shape, lambda b, i: (0,) * a.ndim, **mode)
    tok_t = pl.BlockSpec((1, DA_WIDTH, TM), lambda b, i: (b, 0, i))
    in_specs = [
        _mod_spec(B, n_lat_tiles),
        full(g_l),
        pl.BlockSpec((1,) + w_in.shape[1:], lambda b, i: (layer, 0, 0), **mode),
        pl.BlockSpec((TM, DA_WIDTH), lambda b, i: (i, 0)),
        pl.BlockSpec((TM, DA_WIDTH), lambda b, i: (i, 0)),
        full(lb_l), full(cs_tab),
    ]
    out_shapes = (
        jax.ShapeDtypeStruct((B, NT, HG_WIDTH), BF16),
        jax.ShapeDtypeStruct((B, NT, HG_WIDTH), BF16),
        jax.ShapeDtypeStruct((B, NT, HG_WIDTH), BF16),
        jax.ShapeDtypeStruct((2, B, NT, HG_WIDTH), F32),
        jax.ShapeDtypeStruct((B, DA_WIDTH, NT), BF16),
        jax.ShapeDtypeStruct((B, NT, DA_WIDTH), BF16),
        jax.ShapeDtypeStruct((B, DA_WIDTH, NT), BF16),
        jax.ShapeDtypeStruct((2, NT, B * FT_WIDTH), BF16),
    )
    out_specs = (
        tok(HG_WIDTH), tok(HG_WIDTH), tok(HG_WIDTH),
        pl.BlockSpec((2, 1, TM, HG_WIDTH), lambda b, i: (0, b, i, 0)),
        tok_t, tok(DA_WIDTH), tok_t,
        pl.BlockSpec((2, TM, FT_WIDTH), lambda b, i: (0, i, b)),
    )
    return in_specs, out_specs, out_shapes


def _inproj(X, mod_l, g_l, w_in, layer, cos_t, sin_t, lb_l, cs_tab, n_lat_tiles):
    B, NT, D = X.shape
    in_specs, out_specs, out_shapes = _inproj_io(B, NT, g_l, w_in, layer, lb_l, cs_tab, n_lat_tiles, False)
    return pl.pallas_call(
        _inproj_kernel,
        grid=(B, NT // TM),
        in_specs=[_tok(D)] + in_specs,
        out_specs=out_specs,
        out_shape=out_shapes,
        compiler_params=_cparams(("arbitrary", "arbitrary")),
        name="adaln_inproj",
    )(X, mod_l, g_l, w_in, cos_t, sin_t, lb_l, cs_tab)


def _hgrn_direction(q, f, v, g, st, cmat, codes, rev):
    C, K = q.shape
    k = 1.0 - f
    g_hi = g.astype(BF16)
    g_lo = (g - g_hi.astype(F32)).astype(BF16)
    cs = jnp.dot(cmat, jnp.concatenate([g_hi, g_lo], axis=1), preferred_element_type=F32)
    bc = cs[:, :K] + cs[:, K:]
    tot = bc[0:1, :] if rev else bc[C - 1:C, :]

    nt_dims = (((1,), (1,)), ((), ()))
    tn_dims = (((0,), (0,)), ((), ()))
    qd = (q * jnp.exp(bc)).astype(BF16)
    o = lax.dot_general(qd, st.astype(BF16), nt_dims, preferred_element_type=F32)
    kdec = (k * jnp.exp(tot - bc)).astype(BF16)
    st_new = st * jnp.exp(tot) + lax.dot_general(v, kdec, tn_dims, preferred_element_type=F32)

    a = jnp.zeros((C, C), F32)
    f3 = f.reshape(C // SUB, SUB, K)
    qd_d = q
    for d8 in range(SUB):
        f_shift = f if d8 == 0 else pltpu.roll(f3, (SUB - d8) if rev else d8, 1).reshape(C, K)
        qd_next = qd_d * f_shift
        red = jnp.sum(qd_d - qd_next, axis=-1, keepdims=True)
        a = jnp.where(codes == d8, red, a)
        qd_d = qd_next

    zeros8 = jnp.zeros((SUB, K), F32)
    b = SUB
    level = 0
    while b < C:
        ql, kl = [], []
        for r in range(0, C, SUB):
            base = (r // (2 * b)) * 2 * b
            upper = (r - base) >= b
            ref_row = base + b if rev else base + b - 1
            q_side = (not upper) if rev else upper
            bref = bc[ref_row:ref_row + 1, :]
            bg = bc[r:r + SUB, :]
            if q_side:
                ql.append(q[r:r + SUB, :] * jnp.exp(bg - bref))
                kl.append(zeros8)
            else:
                kl.append(k[r:r + SUB, :] * jnp.exp(bref - bg))
                ql.append(zeros8)
        qlb = jnp.concatenate(ql, axis=0).astype(BF16)
        klb = jnp.concatenate(kl, axis=0).astype(BF16)
        p = lax.dot_general(qlb, klb, nt_dims, preferred_element_type=F32)
        a = jnp.where(codes == SUB + level, p, a)
        b *= 2
        level += 1

    o = o + jnp.dot(a.astype(BF16), v, preferred_element_type=F32)
    return o, st_new


def _hgrn_tables(C):
    t = np.arange(C)[:, None]
    s = np.arange(C)[None, :]
    cmats, codes = [], []
    for rev in (False, True):
        d = (s - t) if rev else (t - s)
        cmats.append((d >= 0).astype(np.float32))
        code = np.full((C, C), -1, np.int32)
        inside = (d >= 0) & ((t // SUB) == (s // SUB))
        code[inside] = d[inside]
        b, level = SUB, 0
        while b < C:
            split = (d > 0) & ((t // (2 * b)) == (s // (2 * b))) & ((t // b) != (s // b))
            code[split] = SUB + level
            b *= 2
            level += 1
        codes.append(code)
    return jnp.asarray(np.stack(cmats), BF16), jnp.asarray(np.stack(codes), jnp.int32)


def _hgrn_kernel(q_ref, v_ref, lf_ref, cmat_ref, code_ref, o_ref, st_ref, *, n_lat, n_ctx):
    C = CH
    n_chunks = n_lat + n_ctx
    st_ref[...] = jnp.zeros(st_ref.shape, F32)

    def body(j, carry):
        for d, rev in enumerate((False, True)):
            c = (n_chunks - 1 - j) if rev else jnp.where(j < n_ctx, n_lat + j, j - n_ctx)
            r0 = pl.multiple_of(c * C, C)
            q = q_ref[0, pl.ds(r0, C), :].astype(F32)
            v = v_ref[0, pl.ds(r0, C), :]
            g = lf_ref[d, 0, pl.ds(r0, C), :]
            o, st_new = _hgrn_direction(q, jnp.exp(g), v, g, st_ref[d], cmat_ref[d], code_ref[d], rev)
            st_ref[d] = st_new
            o_ref[d, 0, pl.ds(r0, C), :] = o.astype(o_ref.dtype)
        return carry

    lax.fori_loop(0, n_chunks, body, 0, unroll=HGRN_UNROLL)


def _hgrn(q, v, lf, n_lat_rows):
    B, NT, _ = q.shape
    n_lat = n_lat_rows // CH
    n_ctx = (NT - n_lat_rows) // CH
    assert (n_lat + n_ctx) % HGRN_UNROLL == 0
    cmat, codes = _hgrn_tables(CH)
    head = pl.BlockSpec((1, NT, HG_DIM), lambda b, h: (b, 0, h))
    both = pl.BlockSpec((2, 1, NT, HG_DIM), lambda b, h: (0, b, 0, h))
    table = pl.BlockSpec((2, CH, CH), lambda b, h: (0, 0, 0))
    return pl.pallas_call(
        functools.partial(_hgrn_kernel, n_lat=n_lat, n_ctx=n_ctx),
        grid=(B, HG_HEADS),
        in_specs=[head, head, both, table, table],
        out_specs=both,
        out_shape=jax.ShapeDtypeStruct((2, B, NT, HG_WIDTH), BF16),
        scratch_shapes=[pltpu.VMEM((2, HG_DIM, HG_DIM), F32)],
        compiler_params=_cparams(("arbitrary", "arbitrary")),
        name="hgrn_scan",
    )(q, v, lf, cmat, codes)


def _attn_kernel(qT_ref, k_ref, vT_ref, lam_ref, sub_ref, o_ref, s_ref, *, chunks, n_q, out_scale):
    n = len(chunks)
    odd_head = (pl.program_id(1) % 2) == 1
    ones_row = lax.broadcasted_iota(jnp.int32, (VT_ROWS - DA_V, n * KC), 0) == 0
    vt = jnp.concatenate(
        [vT_ref[0, :, chunks[0] * KC:(chunks[-1] + 1) * KC],
         jnp.where(ones_row, 1.0, 0.0).astype(BF16)], axis=0)
    streams = [(t, mp) for t in range(n_q) for mp in range(2)]
    zeros = jnp.zeros((DA_QK, TQ), BF16)

    def query_operand(t, mp):
        q = qT_ref[0, mp * DA_QK:(mp + 1) * DA_QK, t * TQ:(t + 1) * TQ]
        even = jnp.concatenate([q if j == mp else zeros for j in range(4)], axis=0)
        odd = jnp.concatenate([q if j == 2 + mp else zeros for j in range(4)], axis=0)
        return jnp.where(odd_head, odd, even)

    slot0 = jnp.minimum(pl.program_id(2), 0)

    def pass1(slot):
        rhs = query_operand(*streams[slot])
        m8 = None
        for i, c in enumerate(chunks):
            s = jnp.dot(k_ref[0, c * KC:(c + 1) * KC, :], rhs, preferred_element_type=F32)
            s_ref[slot, i * KC:(i + 1) * KC, :] = s
            parts = [s[r:r + 8, :] for r in range(0, KC, 8)]
            while len(parts) > 1:
                parts = [jnp.maximum(parts[j], parts[j + 1]) for j in range(0, len(parts), 2)]
            m8 = parts[0] if m8 is None else jnp.maximum(m8, parts[0])
        return jnp.max(m8, axis=0, keepdims=True)

    def pass2(slot, m):
        p = jnp.exp2((s_ref[slot0 + slot] - m).astype(BF16))
        return jnp.dot(vt, p, preferred_element_type=F32)

    maxes = [pass1(slot) for slot in range(len(streams))]
    accs = {st: pass2(slot, maxes[slot]) for slot, st in enumerate(streams)}

    for t in range(n_q):
        acc0, acc1 = accs[(t, 0)], accs[(t, 1)]
        o = (acc0[0:DA_V] / acc0[DA_V:DA_V + 1]
             - lam_ref[...] * (acc1[0:DA_V] / acc1[DA_V:DA_V + 1]))
        y = o * lax.rsqrt(jnp.mean(o * o, axis=0, keepdims=True) + EPS)
        o_ref[0, :, t * TQ:(t + 1) * TQ] = (y * sub_ref[...] * out_scale).astype(BF16)


def _attention_call(dqT, dk, dvT, lam_l, subln_l, chunks, n_q, q_block0, n_steps, lam_init):
    B, _, NT = dqT.shape
    w = n_q * TQ
    kern = functools.partial(_attn_kernel, chunks=tuple(chunks), n_q=n_q, out_scale=1.0 - lam_init)
    return pl.pallas_call(
        kern,
        grid=(B, DA_HEADS, n_steps),
        in_specs=[
            pl.BlockSpec((1, 2 * DA_QK, w), lambda b, h, i: (b, h, q_block0 + i)),
            pl.BlockSpec((1, NT, 2 * DA_V), lambda b, h, i: (b, 0, h // 2)),
            pl.BlockSpec((1, DA_V, NT), lambda b, h, i: (b, h, 0)),
            pl.BlockSpec((1, TQ), lambda b, h, i: (0, 0)),
            pl.BlockSpec((DA_V, 1), lambda b, h, i: (0, 0)),
        ],
        out_specs=pl.BlockSpec((1, DA_V, w), lambda b, h, i: (b, h, i)),
        out_shape=jax.ShapeDtypeStruct((B, DA_WIDTH, n_steps * w), BF16),
        scratch_shapes=[pltpu.VMEM((2 * n_q, len(chunks) * KC, TQ), F32)],
        compiler_params=_cparams(("arbitrary", "arbitrary", "arbitrary")),
        name="diff_attention",
    )(dqT, dk, dvT, lam_l, subln_l)


def _attention(dqT, dk, dvT, lam_l, subln_l, n_lat_rows, lam_init, with_ctx):
    NT = dqT.shape[2]
    n_chunks, n_lat_chunks = NT // KC, n_lat_rows // KC
    n_ctx_rows = NT - n_lat_rows
    lat = _attention_call(dqT, dk, dvT, lam_l, subln_l, range(n_chunks), ATT_TILES, 0,
                          n_lat_rows // (ATT_TILES * TQ), lam_init)
    if not with_ctx:
        return lat, lat
    ctx = _attention_call(dqT, dk, dvT, lam_l, subln_l, range(n_lat_chunks, n_chunks), 1,
                          n_lat_rows // TQ, n_ctx_rows // TQ, lam_init)
    return lat, ctx


def _fourier_kernel(ct_ref, st_ref, f_ref, o_ref):
    o_ref[...] = (jnp.dot(ct_ref[...], f_ref[0], preferred_element_type=F32)
                  + jnp.dot(st_ref[...], f_ref[1], preferred_element_type=F32)).astype(BF16)


def _fourier(ct, st_neg, f, row_block, n_rows):
    W = f.shape[2]
    tk = min(TK, n_rows)
    return pl.pallas_call(
        _fourier_kernel,
        grid=(n_rows // tk,),
        in_specs=[
            pl.BlockSpec((tk, n_rows), lambda i: (i, 0)),
            pl.BlockSpec((tk, n_rows), lambda i: (i, 0)),
            pl.BlockSpec((2, n_rows, W), lambda i: (0, row_block, 0)),
        ],
        out_specs=pl.BlockSpec((tk, W), lambda i: (i, 0)),
        out_shape=jax.ShapeDtypeStruct((n_rows, W), BF16),
        compiler_params=_cparams(("arbitrary",)),
        name="fourier_mix",
    )(ct, st_neg, f)


N_POST_IN = 13


def _post_kernel(*refs, n_lat_tiles):
    refs[N_POST_IN][0] = _post_compute(*refs[:N_POST_IN], n_lat_tiles=n_lat_tiles)


def _post_inproj_kernel(*refs, n_lat_tiles):
    n_in = N_POST_IN + 7
    x_new = _post_compute(*refs[:N_POST_IN], n_lat_tiles=n_lat_tiles)
    refs[n_in][0] = x_new
    _inproj_compute(x_new, *refs[N_POST_IN:n_in], *refs[n_in + 1:])


def _post_compute(x_ref, mod_ref, g_ref, of_ref, gate_ref, dal_ref, dac_ref, ftl_ref, ftc_ref, onorm_ref,
                  wo_ref, wfi_ref, wfo_ref, *, n_lat_tiles):
    x = x_ref[0]
    mod = mod_ref[0]
    is_lat = pl.program_id(1) < n_lat_tiles
    ft = jnp.where(is_lat, ftl_ref[...], ftc_ref[...])
    da_t = jnp.where(is_lat, dal_ref[0], dac_ref[0])
    o = of_ref[0, 0].astype(F32) + of_ref[1, 0].astype(F32)
    onorm = onorm_ref[...]
    heads = [_rms(o[:, h * HG_DIM:(h + 1) * HG_DIM]) * onorm for h in range(HG_HEADS)]
    hg = (jnp.concatenate(heads, axis=-1) * gate_ref[0].astype(F32)).astype(BF16)
    y = (jnp.dot(hg, wo_ref[0, 0:HG_WIDTH, :], preferred_element_type=F32)
         + lax.dot_general(da_t, wo_ref[0, HG_WIDTH:HG_WIDTH + DA_WIDTH, :], (((0,), (0,)), ((), ())),
                           preferred_element_type=F32)
         + jnp.dot(ft, wo_ref[0, HG_WIDTH + DA_WIDTH:, :], preferred_element_type=F32))
    x1 = x + mod[2:3, :] * (_rms(y) * g_ref[1:2, :])
    h2 = ((_rms(x1) * g_ref[2:3, :]) * (1.0 + mod[4:5, :]) + mod[3:4, :]).astype(BF16)
    gu = jnp.dot(h2, wfi_ref[0], preferred_element_type=F32)
    act = (_silu(gu[:, :FF_HIDDEN]) * gu[:, FF_HIDDEN:]).astype(BF16)
    y2 = jnp.dot(act, wfo_ref[0], preferred_element_type=F32)
    return x1 + mod[5:6, :] * (_rms(y2) * g_ref[3:4, :])


def _post_in_specs(B, g_l, onorm_l, wo, wfi, wfo, layer, n_lat_tiles):
    const = lambda a: pl.BlockSpec(a.shape, lambda b, i: (0,) * a.ndim, pipeline_mode=pl.Buffered(1))
    weight = lambda a: pl.BlockSpec((1,) + a.shape[1:], lambda b, i: (layer, 0, 0), pipeline_mode=pl.Buffered(1))
    specs = [
        _tok(D_MODEL),
        _mod_spec(B, n_lat_tiles),
        const(g_l),
        pl.BlockSpec((2, 1, TM, HG_WIDTH), lambda b, i: (0, b, i, 0)),
        _tok(HG_WIDTH),
        pl.BlockSpec((1, DA_WIDTH, TM), lambda b, i: (b, 0, jnp.minimum(i, n_lat_tiles - 1))),
        pl.BlockSpec((1, DA_WIDTH, TM), lambda b, i: (b, 0, jnp.maximum(i - n_lat_tiles, 0))),
        pl.BlockSpec((TM, FT_WIDTH), lambda b, i: (jnp.minimum(i, n_lat_tiles - 1), b)),
        pl.BlockSpec((TM, FT_WIDTH), lambda b, i: (jnp.maximum(i - n_lat_tiles, 0), b)),
        const(onorm_l), weight(wo), weight(wfi), weight(wfo),
    ]
    assert len(specs) == N_POST_IN
    return specs


def _post(X, mod_l, g_l, of, gate, da_lat, da_ctx, ft_lat, ft_ctx, onorm_l, wo, wfi, wfo, layer, n_lat_tiles, n_tiles):
    B, NT, D = X.shape
    return pl.pallas_call(
        functools.partial(_post_kernel, n_lat_tiles=n_lat_tiles),
        grid=(B, n_tiles),
        in_specs=_post_in_specs(B, g_l, onorm_l, wo, wfi, wfo, layer, n_lat_tiles),
        out_specs=_tok(D),
        out_shape=jax.ShapeDtypeStruct((B, n_tiles * TM, D), F32),
        compiler_params=_cparams(("arbitrary", "arbitrary")),
        name="outproj_ffn",
    )(X, mod_l, g_l, of, gate, da_lat, da_ctx, ft_lat, ft_ctx, onorm_l, wo, wfi, wfo)


def _post_inproj(X, post_args, inproj_args, layer, n_lat_tiles):
    B, NT, D = X.shape
    mod_l, g_l, of, gate, da_lat, da_ctx, ft_lat, ft_ctx, onorm_l, wo, wfi, wfo = post_args
    mod_n, g_n, w_in, cos_t, sin_t, lb_n, cs_tab = inproj_args
    in_specs, out_specs, out_shapes = _inproj_io(B, NT, g_n, w_in, layer + 1, lb_n, cs_tab, n_lat_tiles, True)
    outs = pl.pallas_call(
        functools.partial(_post_inproj_kernel, n_lat_tiles=n_lat_tiles),
        grid=(B, NT // TM),
        in_specs=_post_in_specs(B, g_l, onorm_l, wo, wfi, wfo, layer, n_lat_tiles) + in_specs,
        out_specs=(_tok(D),) + tuple(out_specs),
        out_shape=(jax.ShapeDtypeStruct((B, NT, D), F32),) + tuple(out_shapes),
        compiler_params=_cparams(("arbitrary", "arbitrary")),
        name="outproj_ffn_inproj",
    )(X, *post_args, *inproj_args)
    return outs[0], outs[1:]


def _rope_tables(t_lat, t_ctx):
    pos = jnp.arange(t_lat)
    inv_freq = 1.0 / (ROPE_THETA ** (jnp.arange(0, ROPE_AXIS_DIM, 2, dtype=F32) / ROPE_AXIS_DIM))
    ang = jnp.stack([pos // GRID_W, pos % GRID_W], axis=-1).astype(F32)[:, :, None] * inv_freq
    cos, sin = jnp.cos(ang), jnp.sin(ang)
    cos32 = jnp.stack([cos, cos], axis=2).reshape(t_lat, DA_QK)
    sin32 = jnp.stack([-sin, sin], axis=2).reshape(t_lat, DA_QK)
    reps = DA_WIDTH // DA_QK
    cos_t = jnp.concatenate([jnp.tile(cos32, (1, reps)), jnp.ones((t_ctx, DA_WIDTH), F32)], axis=0)
    sin_t = jnp.concatenate([jnp.tile(sin32, (1, reps)), jnp.zeros((t_ctx, DA_WIDTH), F32)], axis=0)
    return cos_t, sin_t


def _channel_table():
    idx = np.arange(FT_GDIM)
    ang = 2.0 * np.pi * ((idx[:, None] * idx[None, :]) % FT_GDIM) / FT_GDIM
    eye = np.eye(FT_GROUPS)
    return jnp.asarray(np.concatenate([np.kron(eye, np.cos(ang)), np.kron(eye, np.sin(ang))], axis=1), BF16)


def _position_tables(n):
    lo = 1
    while lo * lo < n:
        lo *= 2
    hi = n // lo
    k = jnp.arange(n)
    a1 = 2.0 * np.pi * ((k[:, None] * jnp.arange(hi)[None, :]) % hi).astype(F32) / hi
    a2 = 2.0 * np.pi * ((k[:, None] * jnp.arange(lo)[None, :]) % n).astype(F32) / n
    c1, s1 = jnp.cos(a1)[:, :, None], jnp.sin(a1)[:, :, None]
    c2, s2 = jnp.cos(a2)[:, None, :], jnp.sin(a2)[:, None, :]
    scale = 1.0 / math.sqrt(n * FT_GDIM)
    ct = ((c1 * c2 - s1 * s2) * scale).reshape(n, n).astype(BF16)
    st_neg = ((s1 * c2 + c1 * s2) * (-scale)).reshape(n, n).astype(BF16)
    return ct, st_neg


def kernel(x, c, ctx, c_ctx, w_mod, b_mod, norm_g, w_in, w_out, hg_lb_logits, hg_onorm,
           da_lambda, da_subln, w_ffn_in, w_ffn_out):
    B, T, D = x.shape
    Tc = ctx.shape[1]
    depth = w_mod.shape[0]
    assert D == D_MODEL and B + 1 <= MOD_ROWS
    assert T % TM == 0 and Tc % TM == 0 and T % KC == 0 and Tc % KC == 0 and T % CH == 0 and Tc % CH == 0
    n_lat_tiles = T // TM
    NT = T + Tc

    lam_init = [0.8 - 0.6 * math.exp(-0.3 * l) for l in range(depth)]
    lb, lam = _prep(hg_lb_logits, da_lambda, lam_init)

    cond = jnp.concatenate([c.astype(F32), c_ctx.astype(F32)[None, :],
                            jnp.zeros((MOD_ROWS - B - 1, D), F32)], axis=0)
    mods = _modulation(cond, w_mod, b_mod).reshape(depth, MOD_ROWS, N_MOD, D)

    cos_t, sin_t = _rope_tables(T, Tc)
    cs_tab = _channel_table()
    ct_lat, st_lat = _position_tables(T)
    ct_ctx, st_ctx = _position_tables(Tc)

    w_in_b, w_out_b = w_in.astype(BF16), w_out.astype(BF16)
    w_ffn_in_b, w_ffn_out_b = w_ffn_in.astype(BF16), w_ffn_out.astype(BF16)
    X = jnp.concatenate([x, ctx], axis=1).astype(F32)
    g_all = norm_g.astype(F32)
    projected = _inproj(X, mods[0], g_all[0], w_in_b, 0, cos_t, sin_t, lb[0], cs_tab, n_lat_tiles)
    for l in range(depth):
        q, vi, gate, lf, dq, dk, dv, f = projected
        of = _hgrn(q, vi, lf, T)
        with_ctx = l < depth - 1
        da_lat, da_ctx = _attention(dq, dk, dv, lam[l:l + 1], da_subln[l].astype(F32).reshape(DA_V, 1), T,
                                    lam_init[l], with_ctx)
        ft_lat = _fourier(ct_lat, st_lat, f, 0, T)
        ft_ctx = _fourier(ct_ctx, st_ctx, f, T // Tc, Tc) if with_ctx else ft_lat
        post_args = (mods[l], g_all[l], of, gate, da_lat, da_ctx, ft_lat, ft_ctx,
                     hg_onorm[l].astype(F32).reshape(1, HG_DIM), w_out_b, w_ffn_in_b, w_ffn_out_b)
        if with_ctx:
            next_args = (mods[l + 1], g_all[l + 1], w_in_b, cos_t, sin_t, lb[l + 1], cs_tab)
            X, projected = _post_inproj(X, post_args, next_args, l, n_lat_tiles)
        else:
            X = _post(X, *post_args, l, n_lat_tiles, n_lat_tiles)
    return X.astype(x.dtype)
```

```python
import functools
import math

import numpy as np
import jax
import jax.numpy as jnp
from jax import lax
from jax.experimental import pallas as pl
from jax.experimental.pallas import tpu as pltpu

F32 = jnp.float32
BF16 = jnp.bfloat16

D_MODEL = 1024
GRID_W = 64
HG_WIDTH = 512
HG_DIM = 128
HG_HEADS = 4
DA_WIDTH = 256
DA_HEADS = 4
DA_V = 64
DA_QK = 32
FT_WIDTH = 256
FT_GROUPS = 4
FT_GDIM = FT_WIDTH // FT_GROUPS
FF_HIDDEN = 2816
N_MOD = 6
EPS = 1e-6
ROPE_THETA = 10000.0
ROPE_AXIS_DIM = DA_QK // 2

_OFF_Q, _OFF_I, _OFF_G, _OFF_F, _OFF_DQ, _OFF_DK, _OFF_DV, _OFF_FT, IN_WIDTH = (
    0, 512, 1024, 1536, 2560, 2816, 3072, 3328, 3584)

TM = 256
TQ = 256
TQ_CTX = 256
ATT_TILES = 4
KC = 256
CH = 128
SUB = 8
HGRN_UNROLL = 17
TK = 256
VT_ROWS = 80
MOD_ROWS = 8
MOD_TN = 1536
VMEM_LIMIT = 52 * 1024 * 1024

Q_SCALE = (DA_QK ** -0.5) * math.log2(math.e)


def _silu(x):
    return x * jax.nn.sigmoid(x)


def _rms(x):
    return x * lax.rsqrt(jnp.mean(x * x, axis=-1, keepdims=True) + EPS)


def _cparams(sem):
    return pltpu.CompilerParams(dimension_semantics=sem, vmem_limit_bytes=VMEM_LIMIT)


def _mod_kernel(c_ref, w_ref, b_ref, o_ref):
    a = _silu(c_ref[...]).astype(BF16)
    w = w_ref[0].astype(BF16)
    o_ref[0] = jnp.dot(a, w, preferred_element_type=F32) + b_ref[0]


def _modulation(cond, w_mod, b_mod):
    depth, d, n = w_mod.shape
    return pl.pallas_call(
        _mod_kernel,
        grid=(depth, n // MOD_TN),
        in_specs=[
            pl.BlockSpec((MOD_ROWS, d), lambda l, j: (0, 0)),
            pl.BlockSpec((1, d, MOD_TN), lambda l, j: (l, 0, j)),
            pl.BlockSpec((1, 1, MOD_TN), lambda l, j: (l, 0, j)),
        ],
        out_specs=pl.BlockSpec((1, MOD_ROWS, MOD_TN), lambda l, j: (l, 0, j)),
        out_shape=jax.ShapeDtypeStruct((depth, MOD_ROWS, n), F32),
        compiler_params=_cparams(("arbitrary", "arbitrary")),
        name="modulation",
    )(cond, w_mod, b_mod.reshape(depth, 1, n))


def _prep_kernel(lb_ref, lam_ref, lam_init_ref, lbo_ref, lamo_ref, *, depth):
    rows = [lb_ref[l:l + 1, :] for l in range(depth)]
    m = rows[0]
    for r in rows[1:]:
        m = jnp.maximum(m, r)
    e = [jnp.exp(r - m) for r in rows]
    tot = e[0]
    for r in e[1:]:
        tot = tot + r
    p = [r / tot for r in e]
    acc = p[0]
    lbo_ref[0:1, :] = acc - p[0]
    for l in range(1, depth):
        acc = acc + p[l]
        lbo_ref[l:l + 1, :] = acc - p[0]
    x = lam_ref[...]
    a = jnp.sum(x[:, 0:DA_QK] * x[:, DA_QK:2 * DA_QK], axis=-1, keepdims=True)
    b = jnp.sum(x[:, 2 * DA_QK:3 * DA_QK] * x[:, 3 * DA_QK:4 * DA_QK], axis=-1, keepdims=True)
    lamo_ref[...] = jnp.exp(a) - jnp.exp(b) + lam_init_ref[...]


def _prep(hg_lb_logits, da_lambda, lam_init):
    depth = hg_lb_logits.shape[1]
    lb_in = jnp.transpose(hg_lb_logits.astype(F32), (1, 0, 2)).reshape(depth, 2 * HG_WIDTH)
    lam_in = da_lambda.astype(F32).reshape(depth, 4 * DA_QK)
    lam_init_arr = jnp.asarray(np.broadcast_to(np.asarray(lam_init, np.float32)[:, None], (depth, TQ)))
    lb, lam = pl.pallas_call(
        functools.partial(_prep_kernel, depth=depth),
        out_shape=(jax.ShapeDtypeStruct((depth, 2 * HG_WIDTH), F32),
                   jax.ShapeDtypeStruct((depth, TQ), F32)),
        name="param_prep",
    )(lb_in, lam_in, lam_init_arr)
    return lb.reshape(depth, 2, HG_WIDTH), lam


def _inproj_kernel(x_ref, *refs):
    _inproj_compute(x_ref[0], *refs)


def _inproj_compute(x, mod_ref, g_ref, w_ref, cos_ref, sin_ref, lb_ref, cs_ref,
                    q_ref, i_ref, gate_ref, lf_ref, dq_ref, dk_ref, dv_ref, f_ref):
    mod = mod_ref[0]
    h = (_rms(x) * g_ref[0:1, :]) * (1.0 + mod[1:2, :]) + mod[0:1, :]
    hb = h.astype(BF16)

    def proj(a, b):
        return jnp.dot(hb, w_ref[0, :, a:b], preferred_element_type=F32)

    q_ref[0] = _silu(proj(_OFF_Q, _OFF_I)).astype(BF16)
    i_ref[0] = proj(_OFF_I, _OFF_G).astype(BF16)
    gate_ref[0] = _silu(proj(_OFF_G, _OFF_F)).astype(BF16)
    for d in range(2):
        z = proj(_OFF_F + d * HG_WIDTH, _OFF_F + (d + 1) * HG_WIDTH)
        lb = lb_ref[d:d + 1, :]
        lf_ref[d, 0] = jnp.log(lb + (1.0 - lb) * jax.nn.sigmoid(z))

    cos = cos_ref[...]
    sin = sin_ref[...]
    lane = lax.broadcasted_iota(jnp.int32, cos.shape, 1)
    upper_half = (lane & (ROPE_AXIS_DIM // 2)) != 0

    def rope(t):
        partner = jnp.where(upper_half,
                            pltpu.roll(t, ROPE_AXIS_DIM // 2, 1),
                            pltpu.roll(t, 2 * DA_HEADS * DA_QK - ROPE_AXIS_DIM // 2, 1))
        return t * cos + partner * sin

    dq_ref[0] = (rope(proj(_OFF_DQ, _OFF_DK)) * Q_SCALE).T.astype(BF16)
    dk_ref[0] = rope(proj(_OFF_DK, _OFF_DV)).astype(BF16)
    dv_ref[0] = proj(_OFF_DV, _OFF_FT).T.astype(BF16)
    ft = proj(_OFF_FT, IN_WIDTH).astype(BF16)
    cs = jnp.dot(ft, cs_ref[...], preferred_element_type=F32)
    f_ref[0] = cs[:, :FT_WIDTH].astype(BF16)
    f_ref[1] = cs[:, FT_WIDTH:].astype(BF16)


def _tok(w):
    return pl.BlockSpec((1, TM, w), lambda b, i: (b, i, 0))


def _mod_spec(B, n_lat_tiles):
    return pl.BlockSpec((1, N_MOD, D_MODEL), lambda b, i: (jnp.where(i < n_lat_tiles, b, B), 0, 0))


def _inproj_io(B, NT, g_l, w_in, layer, lb_l, cs_tab, n_lat_tiles, single):
    tok = _tok
    mode = dict(pipeline_mode=pl.Buffered(1)) if single else {}
    full = lambda a: pl.BlockSpec(a.shape, lambda b, i: (0,) * a.ndim, **mode)
    tok_t = pl.BlockSpec((1, DA_WIDTH, TM), lambda b, i: (b, 0, i))
    in_specs = [
        _mod_spec(B, n_lat_tiles),
        full(g_l),
        pl.BlockSpec((1,) + w_in.shape[1:], lambda b, i: (layer, 0, 0), **mode),
        pl.BlockSpec((TM, DA_WIDTH), lambda b, i: (i, 0)),
        pl.BlockSpec((TM, DA_WIDTH), lambda b, i: (i, 0)),
        full(lb_l), full(cs_tab),
    ]
    out_shapes = (
        jax.ShapeDtypeStruct((B, NT, HG_WIDTH), BF16),
        jax.ShapeDtypeStruct((B, NT, HG_WIDTH), BF16),
        jax.ShapeDtypeStruct((B, NT, HG_WIDTH), BF16),
        jax.ShapeDtypeStruct((2, B, NT, HG_WIDTH), F32),
        jax.ShapeDtypeStruct((B, DA_WIDTH, NT), BF16),
        jax.ShapeDtypeStruct((B, NT, DA_WIDTH), BF16),
        jax.ShapeDtypeStruct((B, DA_WIDTH, NT), BF16),
        jax.ShapeDtypeStruct((2, NT, B * FT_WIDTH), BF16),
    )
    out_specs = (
        tok(HG_WIDTH), tok(HG_WIDTH), tok(HG_WIDTH),
        pl.BlockSpec((2, 1, TM, HG_WIDTH), lambda b, i: (0, b, i, 0)),
        tok_t, tok(DA_WIDTH), tok_t,
        pl.BlockSpec((2, TM, FT_WIDTH), lambda b, i: (0, i, b)),
    )
    return in_specs, out_specs, out_shapes


def _inproj(X, mod_l, g_l, w_in, layer, cos_t, sin_t, lb_l, cs_tab, n_lat_tiles):
    B, NT, D = X.shape
    in_specs, out_specs, out_shapes = _inproj_io(B, NT, g_l, w_in, layer, lb_l, cs_tab, n_lat_tiles, False)
    return pl.pallas_call(
        _inproj_kernel,
        grid=(B, NT // TM),
        in_specs=[_tok(D)] + in_specs,
        out_specs=out_specs,
        out_shape=out_shapes,
        compiler_params=_cparams(("arbitrary", "arbitrary")),
        name="adaln_inproj",
    )(X, mod_l, g_l, w_in, cos_t, sin_t, lb_l, cs_tab)


def _hgrn_direction(q, f, v, g, st, cmat, codes, rev):
    C, K = q.shape
    k = 1.0 - f
    g_hi = g.astype(BF16)
    g_lo = (g - g_hi.astype(F32)).astype(BF16)
    cs = jnp.dot(cmat, jnp.concatenate([g_hi, g_lo], axis=1), preferred_element_type=F32)
    bc = cs[:, :K] + cs[:, K:]
    tot = bc[0:1, :] if rev else bc[C - 1:C, :]

    nt_dims = (((1,), (1,)), ((), ()))
    tn_dims = (((0,), (0,)), ((), ()))
    qd = (q * jnp.exp(bc)).astype(BF16)
    o = lax.dot_general(qd, st.astype(BF16), nt_dims, preferred_element_type=F32)
    kdec = (k * jnp.exp(tot - bc)).astype(BF16)
    st_new = st * jnp.exp(tot) + lax.dot_general(v, kdec, tn_dims, preferred_element_type=F32)

    a = jnp.zeros((C, C), F32)
    f3 = f.reshape(C // SUB, SUB, K)
    qd_d = q
    for d8 in range(SUB):
        f_shift = f if d8 == 0 else pltpu.roll(f3, (SUB - d8) if rev else d8, 1).reshape(C, K)
        qd_next = qd_d * f_shift
        red = jnp.sum(qd_d - qd_next, axis=-1, keepdims=True)
        a = jnp.where(codes == d8, red, a)
        qd_d = qd_next

    zeros8 = jnp.zeros((SUB, K), F32)
    b = SUB
    level = 0
    while b < C:
        ql, kl = [], []
        for r in range(0, C, SUB):
            base = (r // (2 * b)) * 2 * b
            upper = (r - base) >= b
            ref_row = base + b if rev else base + b - 1
            q_side = (not upper) if rev else upper
            bref = bc[ref_row:ref_row + 1, :]
            bg = bc[r:r + SUB, :]
            if q_side:
                ql.append(q[r:r + SUB, :] * jnp.exp(bg - bref))
                kl.append(zeros8)
            else:
                kl.append(k[r:r + SUB, :] * jnp.exp(bref - bg))
                ql.append(zeros8)
        qlb = jnp.concatenate(ql, axis=0).astype(BF16)
        klb = jnp.concatenate(kl, axis=0).astype(BF16)
        p = lax.dot_general(qlb, klb, nt_dims, preferred_element_type=F32)
        a = jnp.where(codes == SUB + level, p, a)
        b *= 2
        level += 1

    o = o + jnp.dot(a.astype(BF16), v, preferred_element_type=F32)
    return o, st_new


def _hgrn_tables(C):
    t = np.arange(C)[:, None]
    s = np.arange(C)[None, :]
    cmats, codes = [], []
    for rev in (False, True):
        d = (s - t) if rev else (t - s)
        cmats.append((d >= 0).astype(np.float32))
        code = np.full((C, C), -1, np.int32)
        inside = (d >= 0) & ((t // SUB) == (s // SUB))
        code[inside] = d[inside]
        b, level = SUB, 0
        while b < C:
            split = (d > 0) & ((t // (2 * b)) == (s // (2 * b))) & ((t // b) != (s // b))
            code[split] = SUB + level
            b *= 2
            level += 1
        codes.append(code)
    return jnp.asarray(np.stack(cmats), BF16), jnp.asarray(np.stack(codes), jnp.int32)


def _hgrn_kernel(q_ref, v_ref, lf_ref, cmat_ref, code_ref, o_ref, st_ref, *, n_lat, n_ctx):
    C = CH
    n_chunks = n_lat + n_ctx
    st_ref[...] = jnp.zeros(st_ref.shape, F32)

    def body(j, carry):
        for d, rev in enumerate((False, True)):
            c = (n_chunks - 1 - j) if rev else jnp.where(j < n_ctx, n_lat + j, j - n_ctx)
            r0 = pl.multiple_of(c * C, C)
            q = q_ref[0, pl.ds(r0, C), :].astype(F32)
            v = v_ref[0, pl.ds(r0, C), :]
            g = lf_ref[d, 0, pl.ds(r0, C), :]
            o, st_new = _hgrn_direction(q, jnp.exp(g), v, g, st_ref[d], cmat_ref[d], code_ref[d], rev)
            st_ref[d] = st_new
            o_ref[d, 0, pl.ds(r0, C), :] = o.astype(o_ref.dtype)
        return carry

    lax.fori_loop(0, n_chunks, body, 0, unroll=HGRN_UNROLL)


def _hgrn(q, v, lf, n_lat_rows):
    B, NT, _ = q.shape
    n_lat = n_lat_rows // CH
    n_ctx = (NT - n_lat_rows) // CH
    assert (n_lat + n_ctx) % HGRN_UNROLL == 0
    cmat, codes = _hgrn_tables(CH)
    head = pl.BlockSpec((1, NT, HG_DIM), lambda b, h: (b, 0, h))
    both = pl.BlockSpec((2, 1, NT, HG_DIM), lambda b, h: (0, b, 0, h))
    table = pl.BlockSpec((2, CH, CH), lambda b, h: (0, 0, 0))
    return pl.pallas_call(
        functools.partial(_hgrn_kernel, n_lat=n_lat, n_ctx=n_ctx),
        grid=(B, HG_HEADS),
        in_specs=[head, head, both, table, table],
        out_specs=both,
        out_shape=jax.ShapeDtypeStruct((2, B, NT, HG_WIDTH), BF16),
        scratch_shapes=[pltpu.VMEM((2, HG_DIM, HG_DIM), F32)],
        compiler_params=_cparams(("arbitrary", "arbitrary")),
        name="hgrn_scan",
    )(q, v, lf, cmat, codes)


def _attn_kernel(qT_ref, k_ref, vT_ref, lam_ref, sub_ref, o_ref, *s_refs, chunks, n_q, tq, out_scale):
    n = len(chunks)
    odd_head = (pl.program_id(1) % 2) == 1
    ones_row = lax.broadcasted_iota(jnp.int32, (VT_ROWS - DA_V, n * KC), 0) == 0
    vt = jnp.concatenate(
        [vT_ref[0, :, chunks[0] * KC:(chunks[-1] + 1) * KC],
         jnp.where(ones_row, 1.0, 0.0).astype(BF16)], axis=0)
    streams = [(t, mp) for t in range(n_q) for mp in range(2)]
    zeros = jnp.zeros((DA_QK, tq), BF16)

    def query_operand(t, mp):
        q = qT_ref[0, mp * DA_QK:(mp + 1) * DA_QK, t * tq:(t + 1) * tq]
        even = jnp.concatenate([q if j == mp else zeros for j in range(4)], axis=0)
        odd = jnp.concatenate([q if j == 2 + mp else zeros for j in range(4)], axis=0)
        return jnp.where(odd_head, odd, even)

    row0 = pl.multiple_of(jnp.minimum(pl.program_id(2), 0) * KC, KC)

    def pass1(slot):
        rhs = query_operand(*streams[slot])
        m8 = None
        for i, c in enumerate(chunks):
            s = jnp.dot(k_ref[0, c * KC:(c + 1) * KC, :], rhs, preferred_element_type=F32)
            s_refs[slot][i * KC:(i + 1) * KC, :] = s
            parts = [s[r:r + 8, :] for r in range(0, KC, 8)]
            while len(parts) > 1:
                parts = [jnp.maximum(parts[j], parts[j + 1]) for j in range(0, len(parts), 2)]
            m8 = parts[0] if m8 is None else jnp.maximum(m8, parts[0])
        return jnp.max(m8, axis=0, keepdims=True)

    def pass2(slot, m):
        p = jnp.exp2((s_refs[slot][pl.ds(row0, n * KC), :] - m).astype(BF16))
        return jnp.dot(vt, p, preferred_element_type=F32)

    accs, maxes = {}, {}
    for slot in range(len(streams) + 1):
        if slot < len(streams):
            maxes[slot] = pass1(slot)
        if slot >= 1:
            accs[streams[slot - 1]] = pass2(slot - 1, maxes[slot - 1])

    for t in range(n_q):
        acc0, acc1 = accs[(t, 0)], accs[(t, 1)]
        o = (acc0[0:DA_V] / acc0[DA_V:DA_V + 1]
             - lam_ref[...] * (acc1[0:DA_V] / acc1[DA_V:DA_V + 1]))
        y = o * lax.rsqrt(jnp.mean(o * o, axis=0, keepdims=True) + EPS)
        o_ref[0, :, t * tq:(t + 1) * tq] = (y * sub_ref[...] * out_scale).astype(BF16)


def _attention_call(dqT, dk, dvT, lam_l, subln_l, chunks, n_q, tq, q_block0, n_steps, lam_init):
    B, _, NT = dqT.shape
    w = n_q * tq
    kern = functools.partial(_attn_kernel, chunks=tuple(chunks), n_q=n_q, tq=tq, out_scale=1.0 - lam_init)
    return pl.pallas_call(
        kern,
        grid=(B, DA_HEADS, n_steps),
        in_specs=[
            pl.BlockSpec((1, 2 * DA_QK, w), lambda b, h, i: (b, h, q_block0 + i)),
            pl.BlockSpec((1, NT, 2 * DA_V), lambda b, h, i: (b, 0, h // 2)),
            pl.BlockSpec((1, DA_V, NT), lambda b, h, i: (b, h, 0)),
            pl.BlockSpec((1, tq), lambda b, h, i: (0, 0)),
            pl.BlockSpec((DA_V, 1), lambda b, h, i: (0, 0)),
        ],
        out_specs=pl.BlockSpec((1, DA_V, w), lambda b, h, i: (b, h, i)),
        out_shape=jax.ShapeDtypeStruct((B, DA_WIDTH, n_steps * w), BF16),
        scratch_shapes=[pltpu.VMEM((len(chunks) * KC, tq), F32) for _ in range(2 * n_q)],
        compiler_params=_cparams(("arbitrary", "arbitrary", "arbitrary")),
        name="diff_attention",
    )(dqT, dk, dvT, lam_l, subln_l)


def _attention(dqT, dk, dvT, lam_l, subln_l, n_lat_rows, lam_init, with_ctx):
    NT = dqT.shape[2]
    n_chunks, n_lat_chunks = NT // KC, n_lat_rows // KC
    n_ctx_rows = NT - n_lat_rows
    lat = _attention_call(dqT, dk, dvT, lam_l, subln_l, range(n_chunks), ATT_TILES, TQ, 0,
                          n_lat_rows // (ATT_TILES * TQ), lam_init)
    if not with_ctx:
        return lat, lat
    ctx = _attention_call(dqT, dk, dvT, lam_l, subln_l, range(n_lat_chunks, n_chunks), 1, TQ_CTX,
                          n_lat_rows // TQ_CTX, n_ctx_rows // TQ_CTX, lam_init)
    return lat, ctx


def _fourier_kernel(ct_ref, st_ref, f_ref, o_ref):
    o_ref[...] = (jnp.dot(ct_ref[...], f_ref[0], preferred_element_type=F32)
                  + jnp.dot(st_ref[...], f_ref[1], preferred_element_type=F32)).astype(BF16)


def _fourier(ct, st_neg, f, row_block, n_rows):
    W = f.shape[2]
    tk = min(TK, n_rows)
    return pl.pallas_call(
        _fourier_kernel,
        grid=(n_rows // tk,),
        in_specs=[
            pl.BlockSpec((tk, n_rows), lambda i: (i, 0)),
            pl.BlockSpec((tk, n_rows), lambda i: (i, 0)),
            pl.BlockSpec((2, n_rows, W), lambda i: (0, row_block, 0)),
        ],
        out_specs=pl.BlockSpec((tk, W), lambda i: (i, 0)),
        out_shape=jax.ShapeDtypeStruct((n_rows, W), BF16),
        compiler_params=_cparams(("arbitrary",)),
        name="fourier_mix",
    )(ct, st_neg, f)


N_POST_IN = 13


def _post_kernel(*refs, n_lat_tiles):
    refs[N_POST_IN][0] = _post_compute(*refs[:N_POST_IN], n_lat_tiles=n_lat_tiles)


def _post_inproj_kernel(*refs, n_lat_tiles):
    n_in = N_POST_IN + 7
    x_new = _post_compute(*refs[:N_POST_IN], n_lat_tiles=n_lat_tiles)
    refs[n_in][0] = x_new
    _inproj_compute(x_new, *refs[N_POST_IN:n_in], *refs[n_in + 1:])


def _post_compute(x_ref, mod_ref, g_ref, of_ref, gate_ref, dal_ref, dac_ref, ftl_ref, ftc_ref, onorm_ref,
                  wo_ref, wfi_ref, wfo_ref, *, n_lat_tiles):
    x = x_ref[0]
    mod = mod_ref[0]
    is_lat = pl.program_id(1) < n_lat_tiles
    ft = jnp.where(is_lat, ftl_ref[...], ftc_ref[...])
    da_t = jnp.where(is_lat, dal_ref[0], dac_ref[0])
    o = of_ref[0, 0].astype(F32) + of_ref[1, 0].astype(F32)
    onorm = onorm_ref[...]
    heads = [_rms(o[:, h * HG_DIM:(h + 1) * HG_DIM]) * onorm for h in range(HG_HEADS)]
    hg = (jnp.concatenate(heads, axis=-1) * gate_ref[0].astype(F32)).astype(BF16)
    y = (jnp.dot(hg, wo_ref[0, 0:HG_WIDTH, :], preferred_element_type=F32)
         + lax.dot_general(da_t, wo_ref[0, HG_WIDTH:HG_WIDTH + DA_WIDTH, :], (((0,), (0,)), ((), ())),
                           preferred_element_type=F32)
         + jnp.dot(ft, wo_ref[0, HG_WIDTH + DA_WIDTH:, :], preferred_element_type=F32))
    x1 = x + mod[2:3, :] * (_rms(y) * g_ref[1:2, :])
    h2 = ((_rms(x1) * g_ref[2:3, :]) * (1.0 + mod[4:5, :]) + mod[3:4, :]).astype(BF16)
    gu = jnp.dot(h2, wfi_ref[0], preferred_element_type=F32)
    act = (_silu(gu[:, :FF_HIDDEN]) * gu[:, FF_HIDDEN:]).astype(BF16)
    y2 = jnp.dot(act, wfo_ref[0], preferred_element_type=F32)
    return x1 + mod[5:6, :] * (_rms(y2) * g_ref[3:4, :])


def _post_in_specs(B, g_l, onorm_l, wo, wfi, wfo, layer, n_lat_tiles):
    const = lambda a: pl.BlockSpec(a.shape, lambda b, i: (0,) * a.ndim, pipeline_mode=pl.Buffered(1))
    weight = lambda a: pl.BlockSpec((1,) + a.shape[1:], lambda b, i: (layer, 0, 0), pipeline_mode=pl.Buffered(1))
    specs = [
        _tok(D_MODEL),
        _mod_spec(B, n_lat_tiles),
        const(g_l),
        pl.BlockSpec((2, 1, TM, HG_WIDTH), lambda b, i: (0, b, i, 0)),
        _tok(HG_WIDTH),
        pl.BlockSpec((1, DA_WIDTH, TM), lambda b, i: (b, 0, jnp.minimum(i, n_lat_tiles - 1))),
        pl.BlockSpec((1, DA_WIDTH, TM), lambda b, i: (b, 0, jnp.maximum(i - n_lat_tiles, 0))),
        pl.BlockSpec((TM, FT_WIDTH), lambda b, i: (jnp.minimum(i, n_lat_tiles - 1), b)),
        pl.BlockSpec((TM, FT_WIDTH), lambda b, i: (jnp.maximum(i - n_lat_tiles, 0), b)),
        const(onorm_l), weight(wo), weight(wfi), weight(wfo),
    ]
    assert len(specs) == N_POST_IN
    return specs


def _post(X, mod_l, g_l, of, gate, da_lat, da_ctx, ft_lat, ft_ctx, onorm_l, wo, wfi, wfo, layer, n_lat_tiles, n_tiles):
    B, NT, D = X.shape
    return pl.pallas_call(
        functools.partial(_post_kernel, n_lat_tiles=n_lat_tiles),
        grid=(B, n_tiles),
        in_specs=_post_in_specs(B, g_l, onorm_l, wo, wfi, wfo, layer, n_lat_tiles),
        out_specs=_tok(D),
        out_shape=jax.ShapeDtypeStruct((B, n_tiles * TM, D), F32),
        compiler_params=_cparams(("arbitrary", "arbitrary")),
        name="outproj_ffn",
    )(X, mod_l, g_l, of, gate, da_lat, da_ctx, ft_lat, ft_ctx, onorm_l, wo, wfi, wfo)


def _post_inproj(X, post_args, inproj_args, layer, n_lat_tiles):
    B, NT, D = X.shape
    mod_l, g_l, of, gate, da_lat, da_ctx, ft_lat, ft_ctx, onorm_l, wo, wfi, wfo = post_args
    mod_n, g_n, w_in, cos_t, sin_t, lb_n, cs_tab = inproj_args
    in_specs, out_specs, out_shapes = _inproj_io(B, NT, g_n, w_in, layer + 1, lb_n, cs_tab, n_lat_tiles, True)
    outs = pl.pallas_call(
        functools.partial(_post_inproj_kernel, n_lat_tiles=n_lat_tiles),
        grid=(B, NT // TM),
        in_specs=_post_in_specs(B, g_l, onorm_l, wo, wfi, wfo, layer, n_lat_tiles) + in_specs,
        out_specs=(_tok(D),) + tuple(out_specs),
        out_shape=(jax.ShapeDtypeStruct((B, NT, D), F32),) + tuple(out_shapes),
        compiler_params=_cparams(("arbitrary", "arbitrary")),
        name="outproj_ffn_inproj",
    )(X, *post_args, *inproj_args)
    return outs[0], outs[1:]


def _rope_tables(t_lat, t_ctx):
    pos = jnp.arange(t_lat)
    inv_freq = 1.0 / (ROPE_THETA ** (jnp.arange(0, ROPE_AXIS_DIM, 2, dtype=F32) / ROPE_AXIS_DIM))
    ang = jnp.stack([pos // GRID_W, pos % GRID_W], axis=-1).astype(F32)[:, :, None] * inv_freq
    cos, sin = jnp.cos(ang), jnp.sin(ang)
    cos32 = jnp.stack([cos, cos], axis=2).reshape(t_lat, DA_QK)
    sin32 = jnp.stack([-sin, sin], axis=2).reshape(t_lat, DA_QK)
    reps = DA_WIDTH // DA_QK
    cos_t = jnp.concatenate([jnp.tile(cos32, (1, reps)), jnp.ones((t_ctx, DA_WIDTH), F32)], axis=0)
    sin_t = jnp.concatenate([jnp.tile(sin32, (1, reps)), jnp.zeros((t_ctx, DA_WIDTH), F32)], axis=0)
    return cos_t, sin_t


def _channel_table():
    idx = np.arange(FT_GDIM)
    ang = 2.0 * np.pi * ((idx[:, None] * idx[None, :]) % FT_GDIM) / FT_GDIM
    eye = np.eye(FT_GROUPS)
    return jnp.asarray(np.concatenate([np.kron(eye, np.cos(ang)), np.kron(eye, np.sin(ang))], axis=1), BF16)


def _position_tables(n):
    lo = 1
    while lo * lo < n:
        lo *= 2
    hi = n // lo
    k = jnp.arange(n)
    a1 = 2.0 * np.pi * ((k[:, None] * jnp.arange(hi)[None, :]) % hi).astype(F32) / hi
    a2 = 2.0 * np.pi * ((k[:, None] * jnp.arange(lo)[None, :]) % n).astype(F32) / n
    c1, s1 = jnp.cos(a1)[:, :, None], jnp.sin(a1)[:, :, None]
    c2, s2 = jnp.cos(a2)[:, None, :], jnp.sin(a2)[:, None, :]
    scale = 1.0 / math.sqrt(n * FT_GDIM)
    ct = ((c1 * c2 - s1 * s2) * scale).reshape(n, n).astype(BF16)
    st_neg = ((s1 * c2 + c1 * s2) * (-scale)).reshape(n, n).astype(BF16)
    return ct, st_neg


def kernel(x, c, ctx, c_ctx, w_mod, b_mod, norm_g, w_in, w_out, hg_lb_logits, hg_onorm,
           da_lambda, da_subln, w_ffn_in, w_ffn_out):
    B, T, D = x.shape
    Tc = ctx.shape[1]
    depth = w_mod.shape[0]
    assert D == D_MODEL and B + 1 <= MOD_ROWS
    assert T % TM == 0 and Tc % TM == 0 and T % KC == 0 and Tc % KC == 0 and T % CH == 0 and Tc % CH == 0
    n_lat_tiles = T // TM
    NT = T + Tc

    lam_init = [0.8 - 0.6 * math.exp(-0.3 * l) for l in range(depth)]
    lb, lam = _prep(hg_lb_logits, da_lambda, lam_init)

    cond = jnp.concatenate([c.astype(F32), c_ctx.astype(F32)[None, :],
                            jnp.zeros((MOD_ROWS - B - 1, D), F32)], axis=0)
    mods = _modulation(cond, w_mod, b_mod).reshape(depth, MOD_ROWS, N_MOD, D)

    cos_t, sin_t = _rope_tables(T, Tc)
    cs_tab = _channel_table()
    ct_lat, st_lat = _position_tables(T)
    ct_ctx, st_ctx = _position_tables(Tc)

    w_in_b, w_out_b = w_in.astype(BF16), w_out.astype(BF16)
    w_ffn_in_b, w_ffn_out_b = w_ffn_in.astype(BF16), w_ffn_out.astype(BF16)
    X = jnp.concatenate([x, ctx], axis=1).astype(F32)
    g_all = norm_g.astype(F32)
    projected = _inproj(X, mods[0], g_all[0], w_in_b, 0, cos_t, sin_t, lb[0], cs_tab, n_lat_tiles)
    for l in range(depth):
        q, vi, gate, lf, dq, dk, dv, f = projected
        of = _hgrn(q, vi, lf, T)
        with_ctx = l < depth - 1
        da_lat, da_ctx = _attention(dq, dk, dv, lam[l:l + 1], da_subln[l].astype(F32).reshape(DA_V, 1), T,
                                    lam_init[l], with_ctx)
        ft_lat = _fourier(ct_lat, st_lat, f, 0, T)
        ft_ctx = _fourier(ct_ctx, st_ctx, f, T // Tc, Tc) if with_ctx else ft_lat
        post_args = (mods[l], g_all[l], of, gate, da_lat, da_ctx, ft_lat, ft_ctx,
                     hg_onorm[l].astype(F32).reshape(1, HG_DIM), w_out_b, w_ffn_in_b, w_ffn_out_b)
        if with_ctx:
            next_args = (mods[l + 1], g_all[l + 1], w_in_b, cos_t, sin_t, lb[l + 1], cs_tab)
            X, projected = _post_inproj(X, post_args, next_args, l, n_lat_tiles)
        else:
            X = _post(X, *post_args, l, n_lat_tiles, n_lat_tiles)
    return X.astype(x.dtype)
```

```python
import functools
import math

import numpy as np
import jax
import jax.numpy as jnp
from jax import lax
from jax.experimental import pallas as pl
from jax.experimental.pallas import tpu as pltpu

F32 = jnp.float32
BF16 = jnp.bfloat16

D_MODEL = 1024
GRID_W = 64
HG_WIDTH = 512
HG_DIM = 128
HG_HEADS = 4
DA_WIDTH = 256
DA_HEADS = 4
DA_V = 64
DA_QK = 32
FT_WIDTH = 256
FT_GROUPS = 4
FT_GDIM = FT_WIDTH // FT_GROUPS
FF_HIDDEN = 2816
N_MOD = 6
EPS = 1e-6
ROPE_THETA = 10000.0
ROPE_AXIS_DIM = DA_QK // 2

_OFF_Q, _OFF_I, _OFF_G, _OFF_F, _OFF_DQ, _OFF_DK, _OFF_DV, _OFF_FT, IN_WIDTH = (
    0, 512, 1024, 1536, 2560, 2816, 3072, 3328, 3584)

TM = 256
TQ = 256
TQ_CTX = 256
ATT_TILES = 4
KC = 256
CH = 128
SUB = 8
HGRN_UNROLL = 17
TK = 256
VT_ROWS = 80
MOD_ROWS = 8
MOD_TN = 1536
VMEM_LIMIT = 52 * 1024 * 1024

Q_SCALE = (DA_QK ** -0.5) * math.log2(math.e)


def _silu(x):
    return x * jax.nn.sigmoid(x)


def _rms(x):
    return x * lax.rsqrt(jnp.mean(x * x, axis=-1, keepdims=True) + EPS)


def _cparams(sem):
    return pltpu.CompilerParams(dimension_semantics=sem, vmem_limit_bytes=VMEM_LIMIT)


def _mod_kernel(c_ref, w_ref, b_ref, o_ref):
    a = _silu(c_ref[...]).astype(BF16)
    w = w_ref[0].astype(BF16)
    o_ref[0] = jnp.dot(a, w, preferred_element_type=F32) + b_ref[0]


def _modulation(cond, w_mod, b_mod):
    depth, d, n = w_mod.shape
    return pl.pallas_call(
        _mod_kernel,
        grid=(depth, n // MOD_TN),
        in_specs=[
            pl.BlockSpec((MOD_ROWS, d), lambda l, j: (0, 0)),
            pl.BlockSpec((1, d, MOD_TN), lambda l, j: (l, 0, j)),
            pl.BlockSpec((1, 1, MOD_TN), lambda l, j: (l, 0, j)),
        ],
        out_specs=pl.BlockSpec((1, MOD_ROWS, MOD_TN), lambda l, j: (l, 0, j)),
        out_shape=jax.ShapeDtypeStruct((depth, MOD_ROWS, n), F32),
        compiler_params=_cparams(("arbitrary", "arbitrary")),
        name="modulation",
    )(cond, w_mod, b_mod.reshape(depth, 1, n))


def _prep_kernel(lb_ref, lam_ref, lam_init_ref, lbo_ref, lamo_ref, *, depth):
    rows = [lb_ref[l:l + 1, :] for l in range(depth)]
    m = rows[0]
    for r in rows[1:]:
        m = jnp.maximum(m, r)
    e = [jnp.exp(r - m) for r in rows]
    tot = e[0]
    for r in e[1:]:
        tot = tot + r
    p = [r / tot for r in e]
    acc = p[0]
    lbo_ref[0:1, :] = acc - p[0]
    for l in range(1, depth):
        acc = acc + p[l]
        lbo_ref[l:l + 1, :] = acc - p[0]
    x = lam_ref[...]
    a = jnp.sum(x[:, 0:DA_QK] * x[:, DA_QK:2 * DA_QK], axis=-1, keepdims=True)
    b = jnp.sum(x[:, 2 * DA_QK:3 * DA_QK] * x[:, 3 * DA_QK:4 * DA_QK], axis=-1, keepdims=True)
    lamo_ref[...] = jnp.exp(a) - jnp.exp(b) + lam_init_ref[...]


def _prep(hg_lb_logits, da_lambda, lam_init):
    depth = hg_lb_logits.shape[1]
    lb_in = jnp.transpose(hg_lb_logits.astype(F32), (1, 0, 2)).reshape(depth, 2 * HG_WIDTH)
    lam_in = da_lambda.astype(F32).reshape(depth, 4 * DA_QK)
    lam_init_arr = jnp.asarray(np.broadcast_to(np.asarray(lam_init, np.float32)[:, None], (depth, TQ)))
    lb, lam = pl.pallas_call(
        functools.partial(_prep_kernel, depth=depth),
        out_shape=(jax.ShapeDtypeStruct((depth, 2 * HG_WIDTH), F32),
                   jax.ShapeDtypeStruct((depth, TQ), F32)),
        name="param_prep",
    )(lb_in, lam_in, lam_init_arr)
    return lb.reshape(depth, 2, HG_WIDTH), lam


def _inproj_kernel(x_ref, *refs):
    _inproj_compute(x_ref[0], *refs)


def _inproj_compute(x, mod_ref, g_ref, w_ref, cos_ref, sin_ref, lb_ref, cs_ref,
                    q_ref, i_ref, gate_ref, lf_ref, dq_ref, dk_ref, dv_ref, f_ref):
    mod = mod_ref[0]
    h = _rms(x) * (g_ref[0:1, :] * (1.0 + mod[1:2, :])) + mod[0:1, :]
    hb = h.astype(BF16)

    def proj(a, b):
        return jnp.dot(hb, w_ref[0, :, a:b], preferred_element_type=F32)

    q_ref[0] = _silu(proj(_OFF_Q, _OFF_I)).astype(BF16)
    i_ref[0] = proj(_OFF_I, _OFF_G).astype(BF16)
    gate_ref[0] = _silu(proj(_OFF_G, _OFF_F)).astype(BF16)
    for d in range(2):
        z = proj(_OFF_F + d * HG_WIDTH, _OFF_F + (d + 1) * HG_WIDTH)
        lb = lb_ref[d:d + 1, :]
        lf_ref[d, 0] = jnp.log(lb + (1.0 - lb) * jax.nn.sigmoid(z))

    cos = cos_ref[...]
    sin = sin_ref[...]
    lane = lax.broadcasted_iota(jnp.int32, cos.shape, 1)
    upper_half = (lane & (ROPE_AXIS_DIM // 2)) != 0

    def rope(t):
        partner = jnp.where(upper_half,
                            pltpu.roll(t, ROPE_AXIS_DIM // 2, 1),
                            pltpu.roll(t, 2 * DA_HEADS * DA_QK - ROPE_AXIS_DIM // 2, 1))
        return t * cos + partner * sin

    dq_ref[0] = (rope(proj(_OFF_DQ, _OFF_DK)) * Q_SCALE).T.astype(BF16)
    dk_ref[0] = rope(proj(_OFF_DK, _OFF_DV)).astype(BF16)
    dv_ref[0] = proj(_OFF_DV, _OFF_FT).T.astype(BF16)
    ft = proj(_OFF_FT, IN_WIDTH).astype(BF16)
    cs = jnp.dot(ft, cs_ref[...], preferred_element_type=F32)
    f_ref[0] = cs[:, :FT_WIDTH].astype(BF16)
    f_ref[1] = cs[:, FT_WIDTH:].astype(BF16)


def _tok(w):
    return pl.BlockSpec((1, TM, w), lambda b, i: (b, i, 0))


def _mod_spec(B, n_lat_tiles):
    return pl.BlockSpec((1, N_MOD, D_MODEL), lambda b, i: (jnp.where(i < n_lat_tiles, b, B), 0, 0))


def _inproj_io(B, NT, g_l, w_in, layer, lb_l, cs_tab, n_lat_tiles, single):
    tok = _tok
    mode = dict(pipeline_mode=pl.Buffered(1)) if single else {}
    full = lambda a: pl.BlockSpec(a.shape, lambda b, i: (0,) * a.ndim, **mode)
    tok_t = pl.BlockSpec((1, DA_WIDTH, TM), lambda b, i: (b, 0, i))
    in_specs = [
        _mod_spec(B, n_lat_tiles),
        full(g_l),
        pl.BlockSpec((1,) + w_in.shape[1:], lambda b, i: (layer, 0, 0), **mode),
        pl.BlockSpec((TM, DA_WIDTH), lambda b, i: (i, 0)),
        pl.BlockSpec((TM, DA_WIDTH), lambda b, i: (i, 0)),
        full(lb_l), full(cs_tab),
    ]
    out_shapes = (
        jax.ShapeDtypeStruct((B, NT, HG_WIDTH), BF16),
        jax.ShapeDtypeStruct((B, NT, HG_WIDTH), BF16),
        jax.ShapeDtypeStruct((B, NT, HG_WIDTH), BF16),
        jax.ShapeDtypeStruct((2, B, NT, HG_WIDTH), F32),
        jax.ShapeDtypeStruct((B, DA_WIDTH, NT), BF16),
        jax.ShapeDtypeStruct((B, NT, DA_WIDTH), BF16),
        jax.ShapeDtypeStruct((B, DA_WIDTH, NT), BF16),
        jax.ShapeDtypeStruct((2, NT, B * FT_WIDTH), BF16),
    )
    out_specs = (
        tok(HG_WIDTH), tok(HG_WIDTH), tok(HG_WIDTH),
        pl.BlockSpec((2, 1, TM, HG_WIDTH), lambda b, i: (0, b, i, 0)),
        tok_t, tok(DA_WIDTH), tok_t,
        pl.BlockSpec((2, TM, FT_WIDTH), lambda b, i: (0, i, b)),
    )
    return in_specs, out_specs, out_shapes


def _inproj(X, mod_l, g_l, w_in, layer, cos_t, sin_t, lb_l, cs_tab, n_lat_tiles):
    B, NT, D = X.shape
    in_specs, out_specs, out_shapes = _inproj_io(B, NT, g_l, w_in, layer, lb_l, cs_tab, n_lat_tiles, False)
    return pl.pallas_call(
        _inproj_kernel,
        grid=(B, NT // TM),
        in_specs=[_tok(D)] + in_specs,
        out_specs=out_specs,
        out_shape=out_shapes,
        compiler_params=_cparams(("arbitrary", "arbitrary")),
        name="adaln_inproj",
    )(X, mod_l, g_l, w_in, cos_t, sin_t, lb_l, cs_tab)


def _hgrn_direction(q, f, v, g, st, cmat, codes, rev):
    C, K = q.shape
    k = 1.0 - f
    g_hi = g.astype(BF16)
    g_lo = (g - g_hi.astype(F32)).astype(BF16)
    cs = jnp.dot(cmat, jnp.concatenate([g_hi, g_lo], axis=1), preferred_element_type=F32)
    bc = cs[:, :K] + cs[:, K:]
    tot = bc[0:1, :] if rev else bc[C - 1:C, :]

    nt_dims = (((1,), (1,)), ((), ()))
    tn_dims = (((0,), (0,)), ((), ()))
    qd = (q * jnp.exp(bc)).astype(BF16)
    o = lax.dot_general(qd, st.astype(BF16), nt_dims, preferred_element_type=F32)
    kdec = (k * jnp.exp(tot - bc)).astype(BF16)
    st_new = st * jnp.exp(tot) + lax.dot_general(v, kdec, tn_dims, preferred_element_type=F32)

    a = jnp.zeros((C, C), F32)
    f3 = f.reshape(C // SUB, SUB, K)
    qd_d = q
    for d8 in range(SUB):
        f_shift = f if d8 == 0 else pltpu.roll(f3, (SUB - d8) if rev else d8, 1).reshape(C, K)
        qd_next = qd_d * f_shift
        red = jnp.sum(qd_d - qd_next, axis=-1, keepdims=True)
        a = jnp.where(codes == d8, red, a)
        qd_d = qd_next

    zeros8 = jnp.zeros((SUB, K), F32)
    b = SUB
    level = 0
    while b < C:
        ql, kl = [], []
        for r in range(0, C, SUB):
            base = (r // (2 * b)) * 2 * b
            upper = (r - base) >= b
            ref_row = base + b if rev else base + b - 1
            q_side = (not upper) if rev else upper
            bref = bc[ref_row:ref_row + 1, :]
            bg = bc[r:r + SUB, :]
            if q_side:
                ql.append(q[r:r + SUB, :] * jnp.exp(bg - bref))
                kl.append(zeros8)
            else:
                kl.append(k[r:r + SUB, :] * jnp.exp(bref - bg))
                ql.append(zeros8)
        qlb = jnp.concatenate(ql, axis=0).astype(BF16)
        klb = jnp.concatenate(kl, axis=0).astype(BF16)
        p = lax.dot_general(qlb, klb, nt_dims, preferred_element_type=F32)
        a = jnp.where(codes == SUB + level, p, a)
        b *= 2
        level += 1

    o = o + jnp.dot(a.astype(BF16), v, preferred_element_type=F32)
    return o, st_new


def _hgrn_tables(C):
    t = np.arange(C)[:, None]
    s = np.arange(C)[None, :]
    cmats, codes = [], []
    for rev in (False, True):
        d = (s - t) if rev else (t - s)
        cmats.append((d >= 0).astype(np.float32))
        code = np.full((C, C), -1, np.int32)
        inside = (d >= 0) & ((t // SUB) == (s // SUB))
        code[inside] = d[inside]
        b, level = SUB, 0
        while b < C:
            split = (d > 0) & ((t // (2 * b)) == (s // (2 * b))) & ((t // b) != (s // b))
            code[split] = SUB + level
            b *= 2
            level += 1
        codes.append(code)
    return jnp.asarray(np.stack(cmats), BF16), jnp.asarray(np.stack(codes), jnp.int32)


def _hgrn_kernel(q_ref, v_ref, lf_ref, cmat_ref, code_ref, o_ref, st_ref, *, n_lat, n_ctx):
    C = CH
    n_chunks = n_lat + n_ctx
    st_ref[...] = jnp.zeros(st_ref.shape, F32)

    def body(j, carry):
        for d, rev in enumerate((False, True)):
            c = (n_chunks - 1 - j) if rev else jnp.where(j < n_ctx, n_lat + j, j - n_ctx)
            r0 = pl.multiple_of(c * C, C)
            q = q_ref[0, pl.ds(r0, C), :].astype(F32)
            v = v_ref[0, pl.ds(r0, C), :]
            g = lf_ref[d, 0, pl.ds(r0, C), :]
            o, st_new = _hgrn_direction(q, jnp.exp(g), v, g, st_ref[d], cmat_ref[d], code_ref[d], rev)
            st_ref[d] = st_new
            o_ref[d, 0, pl.ds(r0, C), :] = o.astype(o_ref.dtype)
        return carry

    lax.fori_loop(0, n_chunks, body, 0, unroll=HGRN_UNROLL)


def _hgrn(q, v, lf, n_lat_rows):
    B, NT, _ = q.shape
    n_lat = n_lat_rows // CH
    n_ctx = (NT - n_lat_rows) // CH
    assert (n_lat + n_ctx) % HGRN_UNROLL == 0
    cmat, codes = _hgrn_tables(CH)
    head = pl.BlockSpec((1, NT, HG_DIM), lambda b, h: (b, 0, h))
    both = pl.BlockSpec((2, 1, NT, HG_DIM), lambda b, h: (0, b, 0, h))
    table = pl.BlockSpec((2, CH, CH), lambda b, h: (0, 0, 0))
    return pl.pallas_call(
        functools.partial(_hgrn_kernel, n_lat=n_lat, n_ctx=n_ctx),
        grid=(B, HG_HEADS),
        in_specs=[head, head, both, table, table],
        out_specs=both,
        out_shape=jax.ShapeDtypeStruct((2, B, NT, HG_WIDTH), BF16),
        scratch_shapes=[pltpu.VMEM((2, HG_DIM, HG_DIM), F32)],
        compiler_params=_cparams(("arbitrary", "arbitrary")),
        name="hgrn_scan",
    )(q, v, lf, cmat, codes)


def _attn_kernel(qT_ref, k_ref, vT_ref, lam_ref, sub_ref, o_ref, *s_refs, chunks, n_q, tq, out_scale):
    n = len(chunks)
    odd_head = (pl.program_id(1) % 2) == 1
    ones_row = lax.broadcasted_iota(jnp.int32, (VT_ROWS - DA_V, n * KC), 0) == 0
    vt = jnp.concatenate(
        [vT_ref[0, :, chunks[0] * KC:(chunks[-1] + 1) * KC],
         jnp.where(ones_row, 1.0, 0.0).astype(BF16)], axis=0)
    streams = [(t, mp) for t in range(n_q) for mp in range(2)]
    zeros = jnp.zeros((DA_QK, tq), BF16)

    def query_operand(t, mp):
        q = qT_ref[0, mp * DA_QK:(mp + 1) * DA_QK, t * tq:(t + 1) * tq]
        even = jnp.concatenate([q if j == mp else zeros for j in range(4)], axis=0)
        odd = jnp.concatenate([q if j == 2 + mp else zeros for j in range(4)], axis=0)
        return jnp.where(odd_head, odd, even)

    row0 = pl.multiple_of(jnp.minimum(pl.program_id(2), 0) * KC, KC)

    def pass1(slot):
        rhs = query_operand(*streams[slot])
        m8 = None
        for i, c in enumerate(chunks):
            s = jnp.dot(k_ref[0, c * KC:(c + 1) * KC, :], rhs, preferred_element_type=F32)
            s_refs[slot][i * KC:(i + 1) * KC, :] = s
            parts = [s[r:r + 8, :] for r in range(0, KC, 8)]
            while len(parts) > 1:
                parts = [jnp.maximum(parts[j], parts[j + 1]) for j in range(0, len(parts), 2)]
            m8 = parts[0] if m8 is None else jnp.maximum(m8, parts[0])
        return jnp.max(m8, axis=0, keepdims=True)

    def pass2(slot, m):
        p = jnp.exp2((s_refs[slot][pl.ds(row0, n * KC), :] - m).astype(BF16))
        return jnp.dot(vt, p, preferred_element_type=F32)

    accs, maxes = {}, {}
    for slot in range(len(streams) + 1):
        if slot < len(streams):
            maxes[slot] = pass1(slot)
        if slot >= 1:
            accs[streams[slot - 1]] = pass2(slot - 1, maxes[slot - 1])

    for t in range(n_q):
        acc0, acc1 = accs[(t, 0)], accs[(t, 1)]
        o = (acc0[0:DA_V] / acc0[DA_V:DA_V + 1]
             - lam_ref[...] * (acc1[0:DA_V] / acc1[DA_V:DA_V + 1]))
        y = o * lax.rsqrt(jnp.mean(o * o, axis=0, keepdims=True) + EPS)
        o_ref[0, :, t * tq:(t + 1) * tq] = (y * sub_ref[...] * out_scale).astype(BF16)


def _attention_call(dqT, dk, dvT, lam_l, subln_l, chunks, n_q, tq, q_block0, n_steps, lam_init):
    B, _, NT = dqT.shape
    w = n_q * tq
    kern = functools.partial(_attn_kernel, chunks=tuple(chunks), n_q=n_q, tq=tq, out_scale=1.0 - lam_init)
    return pl.pallas_call(
        kern,
        grid=(B, DA_HEADS, n_steps),
        in_specs=[
            pl.BlockSpec((1, 2 * DA_QK, w), lambda b, h, i: (b, h, q_block0 + i)),
            pl.BlockSpec((1, NT, 2 * DA_V), lambda b, h, i: (b, 0, h // 2)),
            pl.BlockSpec((1, DA_V, NT), lambda b, h, i: (b, h, 0)),
            pl.BlockSpec((1, tq), lambda b, h, i: (0, 0)),
            pl.BlockSpec((DA_V, 1), lambda b, h, i: (0, 0)),
        ],
        out_specs=pl.BlockSpec((1, DA_V, w), lambda b, h, i: (b, h, i)),
        out_shape=jax.ShapeDtypeStruct((B, DA_WIDTH, n_steps * w), BF16),
        scratch_shapes=[pltpu.VMEM((len(chunks) * KC, tq), F32) for _ in range(2 * n_q)],
        compiler_params=_cparams(("arbitrary", "arbitrary", "arbitrary")),
        name="diff_attention",
    )(dqT, dk, dvT, lam_l, subln_l)


def _attention(dqT, dk, dvT, lam_l, subln_l, n_lat_rows, lam_init, with_ctx):
    NT = dqT.shape[2]
    n_chunks, n_lat_chunks = NT // KC, n_lat_rows // KC
    n_ctx_rows = NT - n_lat_rows
    lat = _attention_call(dqT, dk, dvT, lam_l, subln_l, range(n_chunks), ATT_TILES, TQ, 0,
                          n_lat_rows // (ATT_TILES * TQ), lam_init)
    if not with_ctx:
        return lat, lat
    ctx = _attention_call(dqT, dk, dvT, lam_l, subln_l, range(n_lat_chunks, n_chunks), 1, TQ_CTX,
                          n_lat_rows // TQ_CTX, n_ctx_rows // TQ_CTX, lam_init)
    return lat, ctx


def _fourier_kernel(ct_ref, st_ref, f_ref, o_ref):
    o_ref[...] = (jnp.dot(ct_ref[...], f_ref[0], preferred_element_type=F32)
                  + jnp.dot(st_ref[...], f_ref[1], preferred_element_type=F32)).astype(BF16)


def _fourier(ct, st_neg, f, row_block, n_rows):
    W = f.shape[2]
    tk = min(TK, n_rows)
    return pl.pallas_call(
        _fourier_kernel,
        grid=(n_rows // tk,),
        in_specs=[
            pl.BlockSpec((tk, n_rows), lambda i: (i, 0)),
            pl.BlockSpec((tk, n_rows), lambda i: (i, 0)),
            pl.BlockSpec((2, n_rows, W), lambda i: (0, row_block, 0)),
        ],
        out_specs=pl.BlockSpec((tk, W), lambda i: (i, 0)),
        out_shape=jax.ShapeDtypeStruct((n_rows, W), BF16),
        compiler_params=_cparams(("arbitrary",)),
        name="fourier_mix",
    )(ct, st_neg, f)


def _fourier_folded_kernel(ct_ref, st_ref, f_ref, fr_ref, fmid_ref, o_ref, xs_ref, xa_ref, *, scale):
    tk = o_ref.shape[0]

    @pl.when(pl.program_id(0) == 0)
    def _():
        row = lax.broadcasted_iota(jnp.int32, xs_ref.shape, 0)
        mirrored = jnp.where(row > 0, fr_ref[0].astype(F32), 0.0)
        xs_ref[...] = (f_ref[0].astype(F32) + mirrored).astype(BF16)
        xa_ref[...] = (f_ref[1].astype(F32) - fr_ref[1].astype(F32)).astype(BF16)

    k = pl.program_id(0) * tk + lax.broadcasted_iota(jnp.int32, (tk, 1), 0)
    sign = (1 - 2 * (k & 1)).astype(F32)
    mid = fmid_ref[0, 0:1, :].astype(F32) * scale
    o_ref[...] = (jnp.dot(ct_ref[...], xs_ref[...], preferred_element_type=F32)
                  + jnp.dot(st_ref[...], xa_ref[...], preferred_element_type=F32)
                  + sign * mid).astype(BF16)


def _fourier_folded(ct_half, st_neg_half, f, n_rows):
    W = f.shape[2]
    half = n_rows // 2
    fr = jnp.roll(jnp.flip(f[:, :n_rows], axis=1), 1, axis=1)[:, :half]
    single = dict(pipeline_mode=pl.Buffered(1))
    return pl.pallas_call(
        functools.partial(_fourier_folded_kernel, scale=1.0 / math.sqrt(n_rows * FT_GDIM)),
        grid=(n_rows // TK,),
        in_specs=[
            pl.BlockSpec((TK, half), lambda i: (i, 0)),
            pl.BlockSpec((TK, half), lambda i: (i, 0)),
            pl.BlockSpec((2, half, W), lambda i: (0, 0, 0), **single),
            pl.BlockSpec((2, half, W), lambda i: (0, 0, 0), **single),
            pl.BlockSpec((2, 2 * SUB, W), lambda i: (0, half // (2 * SUB), 0), **single),
        ],
        out_specs=pl.BlockSpec((TK, W), lambda i: (i, 0)),
        out_shape=jax.ShapeDtypeStruct((n_rows, W), BF16),
        scratch_shapes=[pltpu.VMEM((half, W), BF16), pltpu.VMEM((half, W), BF16)],
        compiler_params=_cparams(("arbitrary",)),
        name="fourier_mix_folded",
    )(ct_half, st_neg_half, f, fr, f)


N_POST_IN = 13


def _post_kernel(*refs, n_lat_tiles):
    refs[N_POST_IN][0] = _post_compute(*refs[:N_POST_IN], n_lat_tiles=n_lat_tiles)


def _post_inproj_kernel(*refs, n_lat_tiles):
    n_in = N_POST_IN + 7
    x_new = _post_compute(*refs[:N_POST_IN], n_lat_tiles=n_lat_tiles)
    refs[n_in][0] = x_new
    _inproj_compute(x_new, *refs[N_POST_IN:n_in], *refs[n_in + 1:])


def _post_compute(x_ref, mod_ref, g_ref, of_ref, gate_ref, dal_ref, dac_ref, ftl_ref, ftc_ref, onorm_ref,
                  wo_ref, wfi_ref, wfo_ref, *, n_lat_tiles):
    x = x_ref[0]
    mod = mod_ref[0]
    is_lat = pl.program_id(1) < n_lat_tiles
    ft = jnp.where(is_lat, ftl_ref[...], ftc_ref[...])
    da_t = jnp.where(is_lat, dal_ref[0], dac_ref[0])
    o = of_ref[0, 0].astype(F32) + of_ref[1, 0].astype(F32)
    onorm = onorm_ref[...]
    heads = [_rms(o[:, h * HG_DIM:(h + 1) * HG_DIM]) * onorm for h in range(HG_HEADS)]
    hg = (jnp.concatenate(heads, axis=-1) * gate_ref[0].astype(F32)).astype(BF16)
    y = (jnp.dot(hg, wo_ref[0, 0:HG_WIDTH, :], preferred_element_type=F32)
         + lax.dot_general(da_t, wo_ref[0, HG_WIDTH:HG_WIDTH + DA_WIDTH, :], (((0,), (0,)), ((), ())),
                           preferred_element_type=F32)
         + jnp.dot(ft, wo_ref[0, HG_WIDTH + DA_WIDTH:, :], preferred_element_type=F32))
    x1 = x + _rms(y) * (mod[2:3, :] * g_ref[1:2, :])
    h2 = (_rms(x1) * (g_ref[2:3, :] * (1.0 + mod[4:5, :])) + mod[3:4, :]).astype(BF16)
    gu = jnp.dot(h2, wfi_ref[0], preferred_element_type=F32)
    act = (_silu(gu[:, :FF_HIDDEN]) * gu[:, FF_HIDDEN:]).astype(BF16)
    y2 = jnp.dot(act, wfo_ref[0], preferred_element_type=F32)
    return x1 + _rms(y2) * (mod[5:6, :] * g_ref[3:4, :])


def _post_in_specs(B, g_l, onorm_l, wo, wfi, wfo, layer, n_lat_tiles):
    const = lambda a: pl.BlockSpec(a.shape, lambda b, i: (0,) * a.ndim, pipeline_mode=pl.Buffered(1))
    weight = lambda a: pl.BlockSpec((1,) + a.shape[1:], lambda b, i: (layer, 0, 0), pipeline_mode=pl.Buffered(1))
    specs = [
        _tok(D_MODEL),
        _mod_spec(B, n_lat_tiles),
        const(g_l),
        pl.BlockSpec((2, 1, TM, HG_WIDTH), lambda b, i: (0, b, i, 0)),
        _tok(HG_WIDTH),
        pl.BlockSpec((1, DA_WIDTH, TM), lambda b, i: (b, 0, jnp.minimum(i, n_lat_tiles - 1))),
        pl.BlockSpec((1, DA_WIDTH, TM), lambda b, i: (b, 0, jnp.maximum(i - n_lat_tiles, 0))),
        pl.BlockSpec((TM, FT_WIDTH), lambda b, i: (jnp.minimum(i, n_lat_tiles - 1), b)),
        pl.BlockSpec((TM, FT_WIDTH), lambda b, i: (jnp.maximum(i - n_lat_tiles, 0), b)),
        const(onorm_l), weight(wo), weight(wfi), weight(wfo),
    ]
    assert len(specs) == N_POST_IN
    return specs


def _post(X, mod_l, g_l, of, gate, da_lat, da_ctx, ft_lat, ft_ctx, onorm_l, wo, wfi, wfo, layer, n_lat_tiles, n_tiles):
    B, NT, D = X.shape
    return pl.pallas_call(
        functools.partial(_post_kernel, n_lat_tiles=n_lat_tiles),
        grid=(B, n_tiles),
        in_specs=_post_in_specs(B, g_l, onorm_l, wo, wfi, wfo, layer, n_lat_tiles),
        out_specs=_tok(D),
        out_shape=jax.ShapeDtypeStruct((B, n_tiles * TM, D), F32),
        compiler_params=_cparams(("arbitrary", "arbitrary")),
        name="outproj_ffn",
    )(X, mod_l, g_l, of, gate, da_lat, da_ctx, ft_lat, ft_ctx, onorm_l, wo, wfi, wfo)


def _post_inproj(X, post_args, inproj_args, layer, n_lat_tiles):
    B, NT, D = X.shape
    mod_l, g_l, of, gate, da_lat, da_ctx, ft_lat, ft_ctx, onorm_l, wo, wfi, wfo = post_args
    mod_n, g_n, w_in, cos_t, sin_t, lb_n, cs_tab = inproj_args
    in_specs, out_specs, out_shapes = _inproj_io(B, NT, g_n, w_in, layer + 1, lb_n, cs_tab, n_lat_tiles, True)
    outs = pl.pallas_call(
        functools.partial(_post_inproj_kernel, n_lat_tiles=n_lat_tiles),
        grid=(B, NT // TM),
        in_specs=_post_in_specs(B, g_l, onorm_l, wo, wfi, wfo, layer, n_lat_tiles) + in_specs,
        out_specs=(_tok(D),) + tuple(out_specs),
        out_shape=(jax.ShapeDtypeStruct((B, NT, D), F32),) + tuple(out_shapes),
        compiler_params=_cparams(("arbitrary", "arbitrary")),
        name="outproj_ffn_inproj",
    )(X, *post_args, *inproj_args)
    return outs[0], outs[1:]


def _rope_tables(t_lat, t_ctx):
    pos = jnp.arange(t_lat)
    inv_freq = 1.0 / (ROPE_THETA ** (jnp.arange(0, ROPE_AXIS_DIM, 2, dtype=F32) / ROPE_AXIS_DIM))
    ang = jnp.stack([pos // GRID_W, pos % GRID_W], axis=-1).astype(F32)[:, :, None] * inv_freq
    cos, sin = jnp.cos(ang), jnp.sin(ang)
    cos32 = jnp.stack([cos, cos], axis=2).reshape(t_lat, DA_QK)
    sin32 = jnp.stack([-sin, sin], axis=2).reshape(t_lat, DA_QK)
    reps = DA_WIDTH // DA_QK
    cos_t = jnp.concatenate([jnp.tile(cos32, (1, reps)), jnp.ones((t_ctx, DA_WIDTH), F32)], axis=0)
    sin_t = jnp.concatenate([jnp.tile(sin32, (1, reps)), jnp.zeros((t_ctx, DA_WIDTH), F32)], axis=0)
    return cos_t, sin_t


def _channel_table():
    idx = np.arange(FT_GDIM)
    ang = 2.0 * np.pi * ((idx[:, None] * idx[None, :]) % FT_GDIM) / FT_GDIM
    eye = np.eye(FT_GROUPS)
    return jnp.asarray(np.concatenate([np.kron(eye, np.cos(ang)), np.kron(eye, np.sin(ang))], axis=1), BF16)


def _position_tables(n, cols):
    lo = 1
    while lo * lo < n:
        lo *= 2
    hi = n // lo
    k = jnp.arange(n)
    a1 = 2.0 * np.pi * ((k[:, None] * jnp.arange(cols // lo)[None, :]) % hi).astype(F32) / hi
    a2 = 2.0 * np.pi * ((k[:, None] * jnp.arange(lo)[None, :]) % n).astype(F32) / n
    c1, s1 = jnp.cos(a1)[:, :, None], jnp.sin(a1)[:, :, None]
    c2, s2 = jnp.cos(a2)[:, None, :], jnp.sin(a2)[:, None, :]
    scale = 1.0 / math.sqrt(n * FT_GDIM)
    ct = ((c1 * c2 - s1 * s2) * scale).reshape(n, cols).astype(BF16)
    st_neg = ((s1 * c2 + c1 * s2) * (-scale)).reshape(n, cols).astype(BF16)
    return ct, st_neg


def kernel(x, c, ctx, c_ctx, w_mod, b_mod, norm_g, w_in, w_out, hg_lb_logits, hg_onorm,
           da_lambda, da_subln, w_ffn_in, w_ffn_out):
    B, T, D = x.shape
    Tc = ctx.shape[1]
    depth = w_mod.shape[0]
    assert D == D_MODEL and B + 1 <= MOD_ROWS
    assert T % TM == 0 and Tc % TM == 0 and T % KC == 0 and Tc % KC == 0 and T % CH == 0 and Tc % CH == 0
    n_lat_tiles = T // TM
    NT = T + Tc

    lam_init = [0.8 - 0.6 * math.exp(-0.3 * l) for l in range(depth)]
    lb, lam = _prep(hg_lb_logits, da_lambda, lam_init)

    cond = jnp.concatenate([c.astype(F32), c_ctx.astype(F32)[None, :],
                            jnp.zeros((MOD_ROWS - B - 1, D), F32)], axis=0)
    mods = _modulation(cond, w_mod, b_mod).reshape(depth, MOD_ROWS, N_MOD, D)

    cos_t, sin_t = _rope_tables(T, Tc)
    cs_tab = _channel_table()
    ct_lat, st_lat = _position_tables(T, T // 2)
    ct_ctx, st_ctx = _position_tables(Tc, Tc)

    w_in_b, w_out_b = w_in.astype(BF16), w_out.astype(BF16)
    w_ffn_in_b, w_ffn_out_b = w_ffn_in.astype(BF16), w_ffn_out.astype(BF16)
    X = jnp.concatenate([x, ctx], axis=1).astype(F32)
    g_all = norm_g.astype(F32)
    projected = _inproj(X, mods[0], g_all[0], w_in_b, 0, cos_t, sin_t, lb[0], cs_tab, n_lat_tiles)
    for l in range(depth):
        q, vi, gate, lf, dq, dk, dv, f = projected
        of = _hgrn(q, vi, lf, T)
        with_ctx = l < depth - 1
        da_lat, da_ctx = _attention(dq, dk, dv, lam[l:l + 1], da_subln[l].astype(F32).reshape(DA_V, 1), T,
                                    lam_init[l], with_ctx)
        ft_lat = _fourier_folded(ct_lat, st_lat, f, T)
        ft_ctx = _fourier(ct_ctx, st_ctx, f, T // Tc, Tc) if with_ctx else ft_lat
        post_args = (mods[l], g_all[l], of, gate, da_lat, da_ctx, ft_lat, ft_ctx,
                     hg_onorm[l].astype(F32).reshape(1, HG_DIM), w_out_b, w_ffn_in_b, w_ffn_out_b)
        if with_ctx:
            next_args = (mods[l + 1], g_all[l + 1], w_in_b, cos_t, sin_t, lb[l + 1], cs_tab)
            X, projected = _post_inproj(X, post_args, next_args, l, n_lat_tiles)
        else:
            X = _post(X, *post_args, l, n_lat_tiles, n_lat_tiles)
    return X.astype(x.dtype)
```

```python
import functools
import math

import numpy as np
import jax
import jax.numpy as jnp
from jax import lax
from jax.experimental import pallas as pl
from jax.experimental.pallas import tpu as pltpu

F32 = jnp.float32
BF16 = jnp.bfloat16

D_MODEL = 1024
GRID_W = 64
HG_WIDTH = 512
HG_DIM = 128
HG_HEADS = 4
DA_WIDTH = 256
DA_HEADS = 4
DA_V = 64
DA_QK = 32
FT_WIDTH = 256
FT_GROUPS = 4
FT_GDIM = FT_WIDTH // FT_GROUPS
FF_HIDDEN = 2816
N_MOD = 6
EPS = 1e-6
ROPE_THETA = 10000.0
ROPE_AXIS_DIM = DA_QK // 2

_OFF_Q, _OFF_I, _OFF_G, _OFF_F, _OFF_DQ, _OFF_DK, _OFF_DV, _OFF_FT, IN_WIDTH = (
    0, 512, 1024, 1536, 2560, 2816, 3072, 3328, 3584)

TM = 256
TQ = 256
TQ_CTX = 256
ATT_TILES = 4
KC = 256
CH = 128
SUB = 8
HGRN_UNROLL = 17
TK = 256
VT_ROWS = 80
MOD_ROWS = 8
MOD_TN = 1536
VMEM_LIMIT = 52 * 1024 * 1024

Q_SCALE = (DA_QK ** -0.5) * math.log2(math.e)


def _silu(x):
    return x * jax.nn.sigmoid(x)


def _rms(x):
    return x * lax.rsqrt(jnp.mean(x * x, axis=-1, keepdims=True) + EPS)


def _cparams(sem):
    return pltpu.CompilerParams(dimension_semantics=sem, vmem_limit_bytes=VMEM_LIMIT)


def _mod_kernel(c_ref, w_ref, b_ref, o_ref):
    a = _silu(c_ref[...]).astype(BF16)
    w = w_ref[0].astype(BF16)
    o_ref[0] = jnp.dot(a, w, preferred_element_type=F32) + b_ref[0]


def _modulation(cond, w_mod, b_mod):
    depth, d, n = w_mod.shape
    return pl.pallas_call(
        _mod_kernel,
        grid=(depth, n // MOD_TN),
        in_specs=[
            pl.BlockSpec((MOD_ROWS, d), lambda l, j: (0, 0)),
            pl.BlockSpec((1, d, MOD_TN), lambda l, j: (l, 0, j)),
            pl.BlockSpec((1, 1, MOD_TN), lambda l, j: (l, 0, j)),
        ],
        out_specs=pl.BlockSpec((1, MOD_ROWS, MOD_TN), lambda l, j: (l, 0, j)),
        out_shape=jax.ShapeDtypeStruct((depth, MOD_ROWS, n), F32),
        compiler_params=_cparams(("arbitrary", "arbitrary")),
        name="modulation",
    )(cond, w_mod, b_mod.reshape(depth, 1, n))


def _prep_kernel(lb_ref, lam_ref, lam_init_ref, lbo_ref, lamo_ref, *, depth):
    rows = [lb_ref[l:l + 1, :] for l in range(depth)]
    m = rows[0]
    for r in rows[1:]:
        m = jnp.maximum(m, r)
    e = [jnp.exp(r - m) for r in rows]
    tot = e[0]
    for r in e[1:]:
        tot = tot + r
    p = [r / tot for r in e]
    acc = p[0]
    lbo_ref[0:1, :] = acc - p[0]
    for l in range(1, depth):
        acc = acc + p[l]
        lbo_ref[l:l + 1, :] = acc - p[0]
    x = lam_ref[...]
    a = jnp.sum(x[:, 0:DA_QK] * x[:, DA_QK:2 * DA_QK], axis=-1, keepdims=True)
    b = jnp.sum(x[:, 2 * DA_QK:3 * DA_QK] * x[:, 3 * DA_QK:4 * DA_QK], axis=-1, keepdims=True)
    lamo_ref[...] = jnp.exp(a) - jnp.exp(b) + lam_init_ref[...]


def _prep(hg_lb_logits, da_lambda, lam_init):
    depth = hg_lb_logits.shape[1]
    lb_in = jnp.transpose(hg_lb_logits.astype(F32), (1, 0, 2)).reshape(depth, 2 * HG_WIDTH)
    lam_in = da_lambda.astype(F32).reshape(depth, 4 * DA_QK)
    lam_init_arr = jnp.asarray(np.broadcast_to(np.asarray(lam_init, np.float32)[:, None], (depth, TQ)))
    lb, lam = pl.pallas_call(
        functools.partial(_prep_kernel, depth=depth),
        out_shape=(jax.ShapeDtypeStruct((depth, 2 * HG_WIDTH), F32),
                   jax.ShapeDtypeStruct((depth, TQ), F32)),
        name="param_prep",
    )(lb_in, lam_in, lam_init_arr)
    return lb.reshape(depth, 2, HG_WIDTH), lam


def _inproj_kernel(x_ref, *refs):
    _inproj_compute(x_ref[0], *refs)


def _inproj_compute(x, mod_ref, g_ref, w_ref, cos_ref, sin_ref, lb_ref, cs_ref,
                    q_ref, i_ref, gate_ref, lf_ref, dq_ref, dk_ref, dv_ref, f_ref):
    mod = mod_ref[0]
    h = _rms(x) * (g_ref[0:1, :] * (1.0 + mod[1:2, :])) + mod[0:1, :]
    hb = h.astype(BF16)

    def proj(a, b):
        return jnp.dot(hb, w_ref[0, :, a:b], preferred_element_type=F32)

    q_ref[0] = _silu(proj(_OFF_Q, _OFF_I)).astype(BF16)
    i_ref[0] = proj(_OFF_I, _OFF_G).astype(BF16)
    gate_ref[0] = _silu(proj(_OFF_G, _OFF_F)).astype(BF16)
    for d in range(2):
        z = proj(_OFF_F + d * HG_WIDTH, _OFF_F + (d + 1) * HG_WIDTH)
        lb = lb_ref[d:d + 1, :]
        lf_ref[d, 0] = jnp.log(lb + (1.0 - lb) * jax.nn.sigmoid(z))

    cos = cos_ref[...]
    sin = sin_ref[...]
    lane = lax.broadcasted_iota(jnp.int32, cos.shape, 1)
    upper_half = (lane & (ROPE_AXIS_DIM // 2)) != 0

    def rope(t):
        partner = jnp.where(upper_half,
                            pltpu.roll(t, ROPE_AXIS_DIM // 2, 1),
                            pltpu.roll(t, 2 * DA_HEADS * DA_QK - ROPE_AXIS_DIM // 2, 1))
        return t * cos + partner * sin

    dq_ref[0] = (rope(proj(_OFF_DQ, _OFF_DK)) * Q_SCALE).T.astype(BF16)
    dk_ref[0] = rope(proj(_OFF_DK, _OFF_DV)).astype(BF16)
    dv_ref[0] = proj(_OFF_DV, _OFF_FT).T.astype(BF16)
    ft = proj(_OFF_FT, IN_WIDTH).astype(BF16)
    cs = jnp.dot(ft, cs_ref[...], preferred_element_type=F32)
    f_ref[0] = cs[:, :FT_WIDTH].astype(BF16)
    f_ref[1] = cs[:, FT_WIDTH:].astype(BF16)


def _tok(w):
    return pl.BlockSpec((1, TM, w), lambda b, i: (b, i, 0))


def _mod_spec(B, n_lat_tiles):
    return pl.BlockSpec((1, N_MOD, D_MODEL), lambda b, i: (jnp.where(i < n_lat_tiles, b, B), 0, 0))


def _inproj_io(B, NT, g_l, w_in, layer, lb_l, cs_tab, n_lat_tiles, single):
    tok = _tok
    mode = dict(pipeline_mode=pl.Buffered(1)) if single else {}
    full = lambda a: pl.BlockSpec(a.shape, lambda b, i: (0,) * a.ndim, **mode)
    tok_t = pl.BlockSpec((1, DA_WIDTH, TM), lambda b, i: (b, 0, i))
    in_specs = [
        _mod_spec(B, n_lat_tiles),
        full(g_l),
        pl.BlockSpec((1,) + w_in.shape[1:], lambda b, i: (layer, 0, 0), **mode),
        pl.BlockSpec((TM, DA_WIDTH), lambda b, i: (i, 0)),
        pl.BlockSpec((TM, DA_WIDTH), lambda b, i: (i, 0)),
        full(lb_l), full(cs_tab),
    ]
    out_shapes = (
        jax.ShapeDtypeStruct((B, NT, HG_WIDTH), BF16),
        jax.ShapeDtypeStruct((B, NT, HG_WIDTH), BF16),
        jax.ShapeDtypeStruct((B, NT, HG_WIDTH), BF16),
        jax.ShapeDtypeStruct((2, B, NT, HG_WIDTH), F32),
        jax.ShapeDtypeStruct((B, DA_WIDTH, NT), BF16),
        jax.ShapeDtypeStruct((B, NT, DA_WIDTH), BF16),
        jax.ShapeDtypeStruct((B, DA_WIDTH, NT), BF16),
        jax.ShapeDtypeStruct((2, NT, B * FT_WIDTH), BF16),
    )
    out_specs = (
        tok(HG_WIDTH), tok(HG_WIDTH), tok(HG_WIDTH),
        pl.BlockSpec((2, 1, TM, HG_WIDTH), lambda b, i: (0, b, i, 0)),
        tok_t, tok(DA_WIDTH), tok_t,
        pl.BlockSpec((2, TM, FT_WIDTH), lambda b, i: (0, i, b)),
    )
    return in_specs, out_specs, out_shapes


def _inproj(X, mod_l, g_l, w_in, layer, cos_t, sin_t, lb_l, cs_tab, n_lat_tiles):
    B, NT, D = X.shape
    in_specs, out_specs, out_shapes = _inproj_io(B, NT, g_l, w_in, layer, lb_l, cs_tab, n_lat_tiles, False)
    return pl.pallas_call(
        _inproj_kernel,
        grid=(B, NT // TM),
        in_specs=[_tok(D)] + in_specs,
        out_specs=out_specs,
        out_shape=out_shapes,
        compiler_params=_cparams(("arbitrary", "arbitrary")),
        name="adaln_inproj",
    )(X, mod_l, g_l, w_in, cos_t, sin_t, lb_l, cs_tab)


def _hgrn_direction(q, f, v, g, st, cmat, codes, rev):
    C, K = q.shape
    k = 1.0 - f
    g_hi = g.astype(BF16)
    g_lo = (g - g_hi.astype(F32)).astype(BF16)
    cs = jnp.dot(cmat, jnp.concatenate([g_hi, g_lo], axis=1), preferred_element_type=F32)
    bc = cs[:, :K] + cs[:, K:]
    tot = bc[0:1, :] if rev else bc[C - 1:C, :]

    nt_dims = (((1,), (1,)), ((), ()))
    tn_dims = (((0,), (0,)), ((), ()))
    qd = (q * jnp.exp(bc)).astype(BF16)
    o = lax.dot_general(qd, st.astype(BF16), nt_dims, preferred_element_type=F32)
    kdec = (k * jnp.exp(tot - bc)).astype(BF16)
    st_new = st * jnp.exp(tot) + lax.dot_general(v, kdec, tn_dims, preferred_element_type=F32)

    a = jnp.zeros((C, C), F32)
    f3 = f.reshape(C // SUB, SUB, K)
    qd_d = q
    for d8 in range(SUB):
        f_shift = f if d8 == 0 else pltpu.roll(f3, (SUB - d8) if rev else d8, 1).reshape(C, K)
        qd_next = qd_d * f_shift
        red = jnp.sum(qd_d - qd_next, axis=-1, keepdims=True)
        a = jnp.where(codes == d8, red, a)
        qd_d = qd_next

    zeros8 = jnp.zeros((SUB, K), F32)
    b = SUB
    level = 0
    while b < C:
        ql, kl = [], []
        for r in range(0, C, SUB):
            base = (r // (2 * b)) * 2 * b
            upper = (r - base) >= b
            ref_row = base + b if rev else base + b - 1
            q_side = (not upper) if rev else upper
            bref = bc[ref_row:ref_row + 1, :]
            bg = bc[r:r + SUB, :]
            if q_side:
                ql.append(q[r:r + SUB, :] * jnp.exp(bg - bref))
                kl.append(zeros8)
            else:
                kl.append(k[r:r + SUB, :] * jnp.exp(bref - bg))
                ql.append(zeros8)
        qlb = jnp.concatenate(ql, axis=0).astype(BF16)
        klb = jnp.concatenate(kl, axis=0).astype(BF16)
        p = lax.dot_general(qlb, klb, nt_dims, preferred_element_type=F32)
        a = jnp.where(codes == SUB + level, p, a)
        b *= 2
        level += 1

    o = o + jnp.dot(a.astype(BF16), v, preferred_element_type=F32)
    return o, st_new


def _hgrn_tables(C):
    t = np.arange(C)[:, None]
    s = np.arange(C)[None, :]
    cmats, codes = [], []
    for rev in (False, True):
        d = (s - t) if rev else (t - s)
        cmats.append((d >= 0).astype(np.float32))
        code = np.full((C, C), -1, np.int32)
        inside = (d >= 0) & ((t // SUB) == (s // SUB))
        code[inside] = d[inside]
        b, level = SUB, 0
        while b < C:
            split = (d > 0) & ((t // (2 * b)) == (s // (2 * b))) & ((t // b) != (s // b))
            code[split] = SUB + level
            b *= 2
            level += 1
        codes.append(code)
    return jnp.asarray(np.stack(cmats), BF16), jnp.asarray(np.stack(codes), jnp.int32)


def _hgrn_kernel(q_ref, v_ref, lf_ref, cmat_ref, code_ref, o_ref, st_ref, *, n_lat, n_ctx):
    C = CH
    n_chunks = n_lat + n_ctx
    st_ref[...] = jnp.zeros(st_ref.shape, F32)

    def body(j, carry):
        for d, rev in enumerate((False, True)):
            c = (n_chunks - 1 - j) if rev else jnp.where(j < n_ctx, n_lat + j, j - n_ctx)
            r0 = pl.multiple_of(c * C, C)
            q = q_ref[0, pl.ds(r0, C), :].astype(F32)
            v = v_ref[0, pl.ds(r0, C), :]
            g = lf_ref[d, 0, pl.ds(r0, C), :]
            o, st_new = _hgrn_direction(q, jnp.exp(g), v, g, st_ref[d], cmat_ref[d], code_ref[d], rev)
            st_ref[d] = st_new
            o_ref[d, 0, pl.ds(r0, C), :] = o.astype(o_ref.dtype)
        return carry

    lax.fori_loop(0, n_chunks, body, 0, unroll=HGRN_UNROLL)


def _hgrn(q, v, lf, n_lat_rows):
    B, NT, _ = q.shape
    n_lat = n_lat_rows // CH
    n_ctx = (NT - n_lat_rows) // CH
    assert (n_lat + n_ctx) % HGRN_UNROLL == 0
    cmat, codes = _hgrn_tables(CH)
    head = pl.BlockSpec((1, NT, HG_DIM), lambda b, h: (b, 0, h))
    both = pl.BlockSpec((2, 1, NT, HG_DIM), lambda b, h: (0, b, 0, h))
    table = pl.BlockSpec((2, CH, CH), lambda b, h: (0, 0, 0))
    return pl.pallas_call(
        functools.partial(_hgrn_kernel, n_lat=n_lat, n_ctx=n_ctx),
        grid=(B, HG_HEADS),
        in_specs=[head, head, both, table, table],
        out_specs=both,
        out_shape=jax.ShapeDtypeStruct((2, B, NT, HG_WIDTH), BF16),
        scratch_shapes=[pltpu.VMEM((2, HG_DIM, HG_DIM), F32)],
        compiler_params=_cparams(("arbitrary", "arbitrary")),
        name="hgrn_scan",
    )(q, v, lf, cmat, codes)


def _attn_kernel(qT_ref, k_ref, vT_ref, lam_ref, sub_ref, o_ref, *s_refs, chunks, n_q, tq, out_scale):
    n = len(chunks)
    odd_head = (pl.program_id(1) % 2) == 1
    ones_row = lax.broadcasted_iota(jnp.int32, (VT_ROWS - DA_V, n * KC), 0) == 0
    vt = jnp.concatenate(
        [vT_ref[0, :, chunks[0] * KC:(chunks[-1] + 1) * KC],
         jnp.where(ones_row, 1.0, 0.0).astype(BF16)], axis=0)
    streams = [(t, mp) for t in range(n_q) for mp in range(2)]
    zeros = jnp.zeros((DA_QK, tq), BF16)

    def query_operand(t, mp):
        q = qT_ref[0, mp * DA_QK:(mp + 1) * DA_QK, t * tq:(t + 1) * tq]
        even = jnp.concatenate([q if j == mp else zeros for j in range(4)], axis=0)
        odd = jnp.concatenate([q if j == 2 + mp else zeros for j in range(4)], axis=0)
        return jnp.where(odd_head, odd, even)

    row0 = pl.multiple_of(jnp.minimum(pl.program_id(2), 0) * KC, KC)

    def pass1(slot):
        rhs = query_operand(*streams[slot])
        m8 = None
        for i, c in enumerate(chunks):
            s = jnp.dot(k_ref[0, c * KC:(c + 1) * KC, :], rhs, preferred_element_type=F32)
            s_refs[slot][i * KC:(i + 1) * KC, :] = s
            parts = [s[r:r + 8, :] for r in range(0, KC, 8)]
            while len(parts) > 1:
                parts = [jnp.maximum(parts[j], parts[j + 1]) for j in range(0, len(parts), 2)]
            m8 = parts[0] if m8 is None else jnp.maximum(m8, parts[0])
        return jnp.max(m8, axis=0, keepdims=True)

    def pass2(slot, m):
        p = jnp.exp2((s_refs[slot][pl.ds(row0, n * KC), :] - m).astype(BF16))
        return jnp.dot(vt, p, preferred_element_type=F32)

    accs, maxes = {}, {}
    for slot in range(len(streams) + 1):
        if slot < len(streams):
            maxes[slot] = pass1(slot)
        if slot >= 1:
            accs[streams[slot - 1]] = pass2(slot - 1, maxes[slot - 1])

    for t in range(n_q):
        acc0, acc1 = accs[(t, 0)], accs[(t, 1)]
        o = (acc0[0:DA_V] / acc0[DA_V:DA_V + 1]
             - lam_ref[...] * (acc1[0:DA_V] / acc1[DA_V:DA_V + 1]))
        y = o * lax.rsqrt(jnp.mean(o * o, axis=0, keepdims=True) + EPS)
        o_ref[0, :, t * tq:(t + 1) * tq] = (y * sub_ref[...] * out_scale).astype(BF16)


def _attention_call(dqT, dk, dvT, lam_l, subln_l, chunks, n_q, tq, q_block0, n_steps, lam_init):
    B, _, NT = dqT.shape
    w = n_q * tq
    kern = functools.partial(_attn_kernel, chunks=tuple(chunks), n_q=n_q, tq=tq, out_scale=1.0 - lam_init)
    return pl.pallas_call(
        kern,
        grid=(B, DA_HEADS, n_steps),
        in_specs=[
            pl.BlockSpec((1, 2 * DA_QK, w), lambda b, h, i: (b, h, q_block0 + i)),
            pl.BlockSpec((1, NT, 2 * DA_V), lambda b, h, i: (b, 0, h // 2)),
            pl.BlockSpec((1, DA_V, NT), lambda b, h, i: (b, h, 0)),
            pl.BlockSpec((1, tq), lambda b, h, i: (0, 0)),
            pl.BlockSpec((DA_V, 1), lambda b, h, i: (0, 0)),
        ],
        out_specs=pl.BlockSpec((1, DA_V, w), lambda b, h, i: (b, h, i)),
        out_shape=jax.ShapeDtypeStruct((B, DA_WIDTH, n_steps * w), BF16),
        scratch_shapes=[pltpu.VMEM((len(chunks) * KC, tq), F32) for _ in range(2 * n_q)],
        compiler_params=_cparams(("arbitrary", "arbitrary", "arbitrary")),
        name="diff_attention",
    )(dqT, dk, dvT, lam_l, subln_l)


def _attention(dqT, dk, dvT, lam_l, subln_l, n_lat_rows, lam_init, with_ctx):
    NT = dqT.shape[2]
    n_chunks, n_lat_chunks = NT // KC, n_lat_rows // KC
    n_ctx_rows = NT - n_lat_rows
    lat = _attention_call(dqT, dk, dvT, lam_l, subln_l, range(n_chunks), ATT_TILES, TQ, 0,
                          n_lat_rows // (ATT_TILES * TQ), lam_init)
    if not with_ctx:
        return lat, lat
    ctx = _attention_call(dqT, dk, dvT, lam_l, subln_l, range(n_lat_chunks, n_chunks), 1, TQ_CTX,
                          n_lat_rows // TQ_CTX, n_ctx_rows // TQ_CTX, lam_init)
    return lat, ctx


def _fourier_kernel(ct_ref, st_ref, f_ref, o_ref):
    o_ref[...] = (jnp.dot(ct_ref[...], f_ref[0], preferred_element_type=F32)
                  + jnp.dot(st_ref[...], f_ref[1], preferred_element_type=F32)).astype(BF16)


def _fourier(ct, st_neg, f, row_block, n_rows):
    W = f.shape[2]
    tk = min(TK, n_rows)
    return pl.pallas_call(
        _fourier_kernel,
        grid=(n_rows // tk,),
        in_specs=[
            pl.BlockSpec((tk, n_rows), lambda i: (i, 0)),
            pl.BlockSpec((tk, n_rows), lambda i: (i, 0)),
            pl.BlockSpec((2, n_rows, W), lambda i: (0, row_block, 0)),
        ],
        out_specs=pl.BlockSpec((tk, W), lambda i: (i, 0)),
        out_shape=jax.ShapeDtypeStruct((n_rows, W), BF16),
        compiler_params=_cparams(("arbitrary",)),
        name="fourier_mix",
    )(ct, st_neg, f)


def _fourier_folded_kernel(ct_ref, st_ref, f_ref, up_ref, fmid_ref, o_ref, xs_ref, xa_ref, *, scale):
    tk = o_ref.shape[0]

    @pl.when(pl.program_id(0) == 0)
    def _():
        n_tiles = xs_ref.shape[0] // tk
        r = lax.broadcasted_iota(jnp.int32, (tk, tk), 0)
        c = lax.broadcasted_iota(jnp.int32, (tk, tk), 1)
        flip = jnp.where(r + c == tk - 1, 1.0, 0.0).astype(BF16)
        first_row = lax.broadcasted_iota(jnp.int32, (tk, xs_ref.shape[1]), 0) == 0
        for j in range(n_tiles):
            rows = slice(j * tk, (j + 1) * tk)
            src = slice((n_tiles - 1 - j) * tk, (n_tiles - j) * tk)
            m0 = jnp.dot(flip, up_ref[0, src, :], preferred_element_type=F32)
            m1 = jnp.dot(flip, up_ref[1, src, :], preferred_element_type=F32)
            if j == 0:
                m0 = jnp.where(first_row, 0.0, m0)
            xs_ref[rows, :] = (f_ref[0, rows, :].astype(F32) + m0).astype(BF16)
            xa_ref[rows, :] = (f_ref[1, rows, :].astype(F32) - m1).astype(BF16)

    k = pl.program_id(0) * tk + lax.broadcasted_iota(jnp.int32, (tk, 1), 0)
    sign = (1 - 2 * (k & 1)).astype(F32)
    mid = fmid_ref[0, 0:1, :].astype(F32) * scale
    o_ref[...] = (jnp.dot(ct_ref[...], xs_ref[...], preferred_element_type=F32)
                  + jnp.dot(st_ref[...], xa_ref[...], preferred_element_type=F32)
                  + sign * mid).astype(BF16)


def _fourier_folded(ct_half, st_neg_half, f, n_rows):
    W = f.shape[2]
    half = n_rows // 2
    up = jnp.concatenate([f[:, half + 1:n_rows], f[:, 0:1]], axis=1)
    single = dict(pipeline_mode=pl.Buffered(1))
    return pl.pallas_call(
        functools.partial(_fourier_folded_kernel, scale=1.0 / math.sqrt(n_rows * FT_GDIM)),
        grid=(n_rows // TK,),
        in_specs=[
            pl.BlockSpec((TK, half), lambda i: (i, 0)),
            pl.BlockSpec((TK, half), lambda i: (i, 0)),
            pl.BlockSpec((2, half, W), lambda i: (0, 0, 0), **single),
            pl.BlockSpec((2, half, W), lambda i: (0, 0, 0), **single),
            pl.BlockSpec((2, 2 * SUB, W), lambda i: (0, half // (2 * SUB), 0), **single),
        ],
        out_specs=pl.BlockSpec((TK, W), lambda i: (i, 0)),
        out_shape=jax.ShapeDtypeStruct((n_rows, W), BF16),
        scratch_shapes=[pltpu.VMEM((half, W), BF16), pltpu.VMEM((half, W), BF16)],
        compiler_params=_cparams(("arbitrary",)),
        name="fourier_mix_folded",
    )(ct_half, st_neg_half, f, up, f)


N_POST_IN = 13


def _post_kernel(*refs, n_lat_tiles):
    refs[N_POST_IN][0] = _post_compute(*refs[:N_POST_IN], n_lat_tiles=n_lat_tiles)


def _post_inproj_kernel(*refs, n_lat_tiles):
    n_in = N_POST_IN + 7
    x_new = _post_compute(*refs[:N_POST_IN], n_lat_tiles=n_lat_tiles)
    refs[n_in][0] = x_new
    _inproj_compute(x_new, *refs[N_POST_IN:n_in], *refs[n_in + 1:])


def _post_compute(x_ref, mod_ref, g_ref, of_ref, gate_ref, dal_ref, dac_ref, ftl_ref, ftc_ref, onorm_ref,
                  wo_ref, wfi_ref, wfo_ref, *, n_lat_tiles):
    x = x_ref[0]
    mod = mod_ref[0]
    is_lat = pl.program_id(1) < n_lat_tiles
    ft = jnp.where(is_lat, ftl_ref[...], ftc_ref[...])
    da_t = jnp.where(is_lat, dal_ref[0], dac_ref[0])
    o = of_ref[0, 0].astype(F32) + of_ref[1, 0].astype(F32)
    onorm = onorm_ref[...]
    heads = [_rms(o[:, h * HG_DIM:(h + 1) * HG_DIM]) * onorm for h in range(HG_HEADS)]
    hg = (jnp.concatenate(heads, axis=-1) * gate_ref[0].astype(F32)).astype(BF16)
    y = (jnp.dot(hg, wo_ref[0, 0:HG_WIDTH, :], preferred_element_type=F32)
         + lax.dot_general(da_t, wo_ref[0, HG_WIDTH:HG_WIDTH + DA_WIDTH, :], (((0,), (0,)), ((), ())),
                           preferred_element_type=F32)
         + jnp.dot(ft, wo_ref[0, HG_WIDTH + DA_WIDTH:, :], preferred_element_type=F32))
    x1 = x + _rms(y) * (mod[2:3, :] * g_ref[1:2, :])
    h2 = (_rms(x1) * (g_ref[2:3, :] * (1.0 + mod[4:5, :])) + mod[3:4, :]).astype(BF16)
    gu = jnp.dot(h2, wfi_ref[0], preferred_element_type=F32)
    act = (_silu(gu[:, :FF_HIDDEN]) * gu[:, FF_HIDDEN:]).astype(BF16)
    y2 = jnp.dot(act, wfo_ref[0], preferred_element_type=F32)
    return x1 + _rms(y2) * (mod[5:6, :] * g_ref[3:4, :])


def _post_in_specs(B, g_l, onorm_l, wo, wfi, wfo, layer, n_lat_tiles):
    const = lambda a: pl.BlockSpec(a.shape, lambda b, i: (0,) * a.ndim, pipeline_mode=pl.Buffered(1))
    weight = lambda a: pl.BlockSpec((1,) + a.shape[1:], lambda b, i: (layer, 0, 0), pipeline_mode=pl.Buffered(1))
    specs = [
        _tok(D_MODEL),
        _mod_spec(B, n_lat_tiles),
        const(g_l),
        pl.BlockSpec((2, 1, TM, HG_WIDTH), lambda b, i: (0, b, i, 0)),
        _tok(HG_WIDTH),
        pl.BlockSpec((1, DA_WIDTH, TM), lambda b, i: (b, 0, jnp.minimum(i, n_lat_tiles - 1))),
        pl.BlockSpec((1, DA_WIDTH, TM), lambda b, i: (b, 0, jnp.maximum(i - n_lat_tiles, 0))),
        pl.BlockSpec((TM, FT_WIDTH), lambda b, i: (jnp.minimum(i, n_lat_tiles - 1), b)),
        pl.BlockSpec((TM, FT_WIDTH), lambda b, i: (jnp.maximum(i - n_lat_tiles, 0), b)),
        const(onorm_l), weight(wo), weight(wfi), weight(wfo),
    ]
    assert len(specs) == N_POST_IN
    return specs


def _post(X, mod_l, g_l, of, gate, da_lat, da_ctx, ft_lat, ft_ctx, onorm_l, wo, wfi, wfo, layer, n_lat_tiles, n_tiles):
    B, NT, D = X.shape
    return pl.pallas_call(
        functools.partial(_post_kernel, n_lat_tiles=n_lat_tiles),
        grid=(B, n_tiles),
        in_specs=_post_in_specs(B, g_l, onorm_l, wo, wfi, wfo, layer, n_lat_tiles),
        out_specs=_tok(D),
        out_shape=jax.ShapeDtypeStruct((B, n_tiles * TM, D), F32),
        compiler_params=_cparams(("arbitrary", "arbitrary")),
        name="outproj_ffn",
    )(X, mod_l, g_l, of, gate, da_lat, da_ctx, ft_lat, ft_ctx, onorm_l, wo, wfi, wfo)


def _post_inproj(X, post_args, inproj_args, layer, n_lat_tiles):
    B, NT, D = X.shape
    mod_l, g_l, of, gate, da_lat, da_ctx, ft_lat, ft_ctx, onorm_l, wo, wfi, wfo = post_args
    mod_n, g_n, w_in, cos_t, sin_t, lb_n, cs_tab = inproj_args
    in_specs, out_specs, out_shapes = _inproj_io(B, NT, g_n, w_in, layer + 1, lb_n, cs_tab, n_lat_tiles, True)
    outs = pl.pallas_call(
        functools.partial(_post_inproj_kernel, n_lat_tiles=n_lat_tiles),
        grid=(B, NT // TM),
        in_specs=_post_in_specs(B, g_l, onorm_l, wo, wfi, wfo, layer, n_lat_tiles) + in_specs,
        out_specs=(_tok(D),) + tuple(out_specs),
        out_shape=(jax.ShapeDtypeStruct((B, NT, D), F32),) + tuple(out_shapes),
        compiler_params=_cparams(("arbitrary", "arbitrary")),
        name="outproj_ffn_inproj",
    )(X, *post_args, *inproj_args)
    return outs[0], outs[1:]


def _rope_tables(t_lat, t_ctx):
    pos = jnp.arange(t_lat)
    inv_freq = 1.0 / (ROPE_THETA ** (jnp.arange(0, ROPE_AXIS_DIM, 2, dtype=F32) / ROPE_AXIS_DIM))
    ang = jnp.stack([pos // GRID_W, pos % GRID_W], axis=-1).astype(F32)[:, :, None] * inv_freq
    cos, sin = jnp.cos(ang), jnp.sin(ang)
    cos32 = jnp.stack([cos, cos], axis=2).reshape(t_lat, DA_QK)
    sin32 = jnp.stack([-sin, sin], axis=2).reshape(t_lat, DA_QK)
    reps = DA_WIDTH // DA_QK
    cos_t = jnp.concatenate([jnp.tile(cos32, (1, reps)), jnp.ones((t_ctx, DA_WIDTH), F32)], axis=0)
    sin_t = jnp.concatenate([jnp.tile(sin32, (1, reps)), jnp.zeros((t_ctx, DA_WIDTH), F32)], axis=0)
    return cos_t, sin_t


def _channel_table():
    idx = np.arange(FT_GDIM)
    ang = 2.0 * np.pi * ((idx[:, None] * idx[None, :]) % FT_GDIM) / FT_GDIM
    eye = np.eye(FT_GROUPS)
    return jnp.asarray(np.concatenate([np.kron(eye, np.cos(ang)), np.kron(eye, np.sin(ang))], axis=1), BF16)


def _position_tables(n, cols):
    lo = 1
    while lo * lo < n:
        lo *= 2
    hi = n // lo
    k = jnp.arange(n)
    a1 = 2.0 * np.pi * ((k[:, None] * jnp.arange(cols // lo)[None, :]) % hi).astype(F32) / hi
    a2 = 2.0 * np.pi * ((k[:, None] * jnp.arange(lo)[None, :]) % n).astype(F32) / n
    c1, s1 = jnp.cos(a1)[:, :, None], jnp.sin(a1)[:, :, None]
    c2, s2 = jnp.cos(a2)[:, None, :], jnp.sin(a2)[:, None, :]
    scale = 1.0 / math.sqrt(n * FT_GDIM)
    ct = ((c1 * c2 - s1 * s2) * scale).reshape(n, cols).astype(BF16)
    st_neg = ((s1 * c2 + c1 * s2) * (-scale)).reshape(n, cols).astype(BF16)
    return ct, st_neg


def kernel(x, c, ctx, c_ctx, w_mod, b_mod, norm_g, w_in, w_out, hg_lb_logits, hg_onorm,
           da_lambda, da_subln, w_ffn_in, w_ffn_out):
    B, T, D = x.shape
    Tc = ctx.shape[1]
    depth = w_mod.shape[0]
    assert D == D_MODEL and B + 1 <= MOD_ROWS
    assert T % TM == 0 and Tc % TM == 0 and T % KC == 0 and Tc % KC == 0 and T % CH == 0 and Tc % CH == 0
    n_lat_tiles = T // TM
    NT = T + Tc

    lam_init = [0.8 - 0.6 * math.exp(-0.3 * l) for l in range(depth)]
    lb, lam = _prep(hg_lb_logits, da_lambda, lam_init)

    cond = jnp.concatenate([c.astype(F32), c_ctx.astype(F32)[None, :],
                            jnp.zeros((MOD_ROWS - B - 1, D), F32)], axis=0)
    mods = _modulation(cond, w_mod, b_mod).reshape(depth, MOD_ROWS, N_MOD, D)

    cos_t, sin_t = _rope_tables(T, Tc)
    cs_tab = _channel_table()
    ct_lat, st_lat = _position_tables(T, T // 2)
    ct_ctx, st_ctx = _position_tables(Tc, Tc)

    w_in_b, w_out_b = w_in.astype(BF16), w_out.astype(BF16)
    w_ffn_in_b, w_ffn_out_b = w_ffn_in.astype(BF16), w_ffn_out.astype(BF16)
    X = jnp.concatenate([x, ctx], axis=1).astype(F32)
    g_all = norm_g.astype(F32)
    projected = _inproj(X, mods[0], g_all[0], w_in_b, 0, cos_t, sin_t, lb[0], cs_tab, n_lat_tiles)
    for l in range(depth):
        q, vi, gate, lf, dq, dk, dv, f = projected
        of = _hgrn(q, vi, lf, T)
        with_ctx = l < depth - 1
        da_lat, da_ctx = _attention(dq, dk, dv, lam[l:l + 1], da_subln[l].astype(F32).reshape(DA_V, 1), T,
                                    lam_init[l], with_ctx)
        ft_lat = _fourier_folded(ct_lat, st_lat, f, T)
        ft_ctx = _fourier(ct_ctx, st_ctx, f, T // Tc, Tc) if with_ctx else ft_lat
        post_args = (mods[l], g_all[l], of, gate, da_lat, da_ctx, ft_lat, ft_ctx,
                     hg_onorm[l].astype(F32).reshape(1, HG_DIM), w_out_b, w_ffn_in_b, w_ffn_out_b)
        if with_ctx:
            next_args = (mods[l + 1], g_all[l + 1], w_in_b, cos_t, sin_t, lb[l + 1], cs_tab)
            X, projected = _post_inproj(X, post_args, next_args, l, n_lat_tiles)
        else:
            X = _post(X, *post_args, l, n_lat_tiles, n_lat_tiles)
    return X.astype(x.dtype)
```

```python
import functools
import math

import numpy as np
import jax
import jax.numpy as jnp
from jax import lax
from jax.experimental import pallas as pl
from jax.experimental.pallas import tpu as pltpu

F32 = jnp.float32
BF16 = jnp.bfloat16

D_MODEL = 1024
GRID_W = 64
HG_WIDTH = 512
HG_DIM = 128
HG_HEADS = 4
DA_WIDTH = 256
DA_HEADS = 4
DA_V = 64
DA_QK = 32
FT_WIDTH = 256
FT_GROUPS = 4
FT_GDIM = FT_WIDTH // FT_GROUPS
FF_HIDDEN = 2816
N_MOD = 6
EPS = 1e-6
ROPE_THETA = 10000.0
ROPE_AXIS_DIM = DA_QK // 2

_OFF_Q, _OFF_I, _OFF_G, _OFF_F, _OFF_DQ, _OFF_DK, _OFF_DV, _OFF_FT, IN_WIDTH = (
    0, 512, 1024, 1536, 2560, 2816, 3072, 3328, 3584)

TM = 256
TQ = 256
TQ_CTX = 256
ATT_TILES = 8
ATT_BUFFERS = 4
KC = 256
CH = 128
SUB = 8
HGRN_UNROLL = 17
TK = 256
VT_ROWS = 80
MOD_ROWS = 8
MOD_TN = 1536
VMEM_LIMIT = 52 * 1024 * 1024

Q_SCALE = (DA_QK ** -0.5) * math.log2(math.e)


def _silu(x):
    return x * jax.nn.sigmoid(x)


def _rms(x):
    return x * lax.rsqrt(jnp.mean(x * x, axis=-1, keepdims=True) + EPS)


def _cparams(sem):
    return pltpu.CompilerParams(dimension_semantics=sem, vmem_limit_bytes=VMEM_LIMIT)


def _mod_kernel(c_ref, w_ref, b_ref, o_ref):
    a = _silu(c_ref[...]).astype(BF16)
    w = w_ref[0].astype(BF16)
    o_ref[0] = jnp.dot(a, w, preferred_element_type=F32) + b_ref[0]


def _modulation(cond, w_mod, b_mod):
    depth, d, n = w_mod.shape
    return pl.pallas_call(
        _mod_kernel,
        grid=(depth, n // MOD_TN),
        in_specs=[
            pl.BlockSpec((MOD_ROWS, d), lambda l, j: (0, 0)),
            pl.BlockSpec((1, d, MOD_TN), lambda l, j: (l, 0, j)),
            pl.BlockSpec((1, 1, MOD_TN), lambda l, j: (l, 0, j)),
        ],
        out_specs=pl.BlockSpec((1, MOD_ROWS, MOD_TN), lambda l, j: (l, 0, j)),
        out_shape=jax.ShapeDtypeStruct((depth, MOD_ROWS, n), F32),
        compiler_params=_cparams(("arbitrary", "arbitrary")),
        name="modulation",
    )(cond, w_mod, b_mod.reshape(depth, 1, n))


def _prep_kernel(lb_ref, lam_ref, lam_init_ref, lbo_ref, lamo_ref, *, depth):
    rows = [lb_ref[l:l + 1, :] for l in range(depth)]
    m = rows[0]
    for r in rows[1:]:
        m = jnp.maximum(m, r)
    e = [jnp.exp(r - m) for r in rows]
    tot = e[0]
    for r in e[1:]:
        tot = tot + r
    p = [r / tot for r in e]
    acc = p[0]
    lbo_ref[0:1, :] = acc - p[0]
    for l in range(1, depth):
        acc = acc + p[l]
        lbo_ref[l:l + 1, :] = acc - p[0]
    x = lam_ref[...]
    a = jnp.sum(x[:, 0:DA_QK] * x[:, DA_QK:2 * DA_QK], axis=-1, keepdims=True)
    b = jnp.sum(x[:, 2 * DA_QK:3 * DA_QK] * x[:, 3 * DA_QK:4 * DA_QK], axis=-1, keepdims=True)
    lamo_ref[...] = jnp.exp(a) - jnp.exp(b) + lam_init_ref[...]


def _prep(hg_lb_logits, da_lambda, lam_init):
    depth = hg_lb_logits.shape[1]
    lb_in = jnp.transpose(hg_lb_logits.astype(F32), (1, 0, 2)).reshape(depth, 2 * HG_WIDTH)
    lam_in = da_lambda.astype(F32).reshape(depth, 4 * DA_QK)
    lam_init_arr = jnp.asarray(np.broadcast_to(np.asarray(lam_init, np.float32)[:, None], (depth, TQ)))
    lb, lam = pl.pallas_call(
        functools.partial(_prep_kernel, depth=depth),
        out_shape=(jax.ShapeDtypeStruct((depth, 2 * HG_WIDTH), F32),
                   jax.ShapeDtypeStruct((depth, TQ), F32)),
        name="param_prep",
    )(lb_in, lam_in, lam_init_arr)
    return lb.reshape(depth, 2, HG_WIDTH), lam


def _inproj_kernel(x_ref, *refs):
    _inproj_compute(x_ref[0], *refs)


def _inproj_compute(x, mod_ref, g_ref, w_ref, cos_ref, sin_ref, lb_ref, cs_ref,
                    q_ref, i_ref, gate_ref, lf_ref, dq_ref, dk_ref, dv_ref, f_ref):
    mod = mod_ref[0]
    h = _rms(x) * (g_ref[0:1, :] * (1.0 + mod[1:2, :])) + mod[0:1, :]
    hb = h.astype(BF16)

    def proj(a, b):
        return jnp.dot(hb, w_ref[0, :, a:b], preferred_element_type=F32)

    q_ref[0] = _silu(proj(_OFF_Q, _OFF_I)).astype(BF16)
    i_ref[0] = proj(_OFF_I, _OFF_G).astype(BF16)
    gate_ref[0] = _silu(proj(_OFF_G, _OFF_F)).astype(BF16)
    for d in range(2):
        z = proj(_OFF_F + d * HG_WIDTH, _OFF_F + (d + 1) * HG_WIDTH)
        lb = lb_ref[d:d + 1, :]
        lf_ref[d, 0] = jnp.log(lb + (1.0 - lb) * jax.nn.sigmoid(z))

    cos = cos_ref[...]
    sin = sin_ref[...]
    lane = lax.broadcasted_iota(jnp.int32, cos.shape, 1)
    upper_half = (lane & (ROPE_AXIS_DIM // 2)) != 0

    def rope(t):
        partner = jnp.where(upper_half,
                            pltpu.roll(t, ROPE_AXIS_DIM // 2, 1),
                            pltpu.roll(t, 2 * DA_HEADS * DA_QK - ROPE_AXIS_DIM // 2, 1))
        return t * cos + partner * sin

    dq_ref[0] = (rope(proj(_OFF_DQ, _OFF_DK)) * Q_SCALE).T.astype(BF16)
    dk_ref[0] = rope(proj(_OFF_DK, _OFF_DV)).astype(BF16)
    dv_ref[0] = proj(_OFF_DV, _OFF_FT).T.astype(BF16)
    ft = proj(_OFF_FT, IN_WIDTH).astype(BF16)
    cs = jnp.dot(ft, cs_ref[...], preferred_element_type=F32)
    f_ref[0] = cs[:, :FT_WIDTH].astype(BF16)
    f_ref[1] = cs[:, FT_WIDTH:].astype(BF16)


def _tok(w):
    return pl.BlockSpec((1, TM, w), lambda b, i: (b, i, 0))


def _mod_spec(B, n_lat_tiles):
    return pl.BlockSpec((1, N_MOD, D_MODEL), lambda b, i: (jnp.where(i < n_lat_tiles, b, B), 0, 0))


def _inproj_io(B, NT, g_l, w_in, layer, lb_l, cs_tab, n_lat_tiles, single):
    tok = _tok
    mode = dict(pipeline_mode=pl.Buffered(1)) if single else {}
    full = lambda a: pl.BlockSpec(a.shape, lambda b, i: (0,) * a.ndim, **mode)
    tok_t = pl.BlockSpec((1, DA_WIDTH, TM), lambda b, i: (b, 0, i))
    in_specs = [
        _mod_spec(B, n_lat_tiles),
        full(g_l),
        pl.BlockSpec((1,) + w_in.shape[1:], lambda b, i: (layer, 0, 0), **mode),
        pl.BlockSpec((TM, DA_WIDTH), lambda b, i: (i, 0)),
        pl.BlockSpec((TM, DA_WIDTH), lambda b, i: (i, 0)),
        full(lb_l), full(cs_tab),
    ]
    out_shapes = (
        jax.ShapeDtypeStruct((B, NT, HG_WIDTH), BF16),
        jax.ShapeDtypeStruct((B, NT, HG_WIDTH), BF16),
        jax.ShapeDtypeStruct((B, NT, HG_WIDTH), BF16),
        jax.ShapeDtypeStruct((2, B, NT, HG_WIDTH), F32),
        jax.ShapeDtypeStruct((B, DA_WIDTH, NT), BF16),
        jax.ShapeDtypeStruct((B, NT, DA_WIDTH), BF16),
        jax.ShapeDtypeStruct((B, DA_WIDTH, NT), BF16),
        jax.ShapeDtypeStruct((2, NT, B * FT_WIDTH), BF16),
    )
    out_specs = (
        tok(HG_WIDTH), tok(HG_WIDTH), tok(HG_WIDTH),
        pl.BlockSpec((2, 1, TM, HG_WIDTH), lambda b, i: (0, b, i, 0)),
        tok_t, tok(DA_WIDTH), tok_t,
        pl.BlockSpec((2, TM, FT_WIDTH), lambda b, i: (0, i, b)),
    )
    return in_specs, out_specs, out_shapes


def _inproj(X, mod_l, g_l, w_in, layer, cos_t, sin_t, lb_l, cs_tab, n_lat_tiles):
    B, NT, D = X.shape
    in_specs, out_specs, out_shapes = _inproj_io(B, NT, g_l, w_in, layer, lb_l, cs_tab, n_lat_tiles, False)
    return pl.pallas_call(
        _inproj_kernel,
        grid=(B, NT // TM),
        in_specs=[_tok(D)] + in_specs,
        out_specs=out_specs,
        out_shape=out_shapes,
        compiler_params=_cparams(("arbitrary", "arbitrary")),
        name="adaln_inproj",
    )(X, mod_l, g_l, w_in, cos_t, sin_t, lb_l, cs_tab)


def _hgrn_direction(q, f, v, g, st, cmat, codes, rev):
    C, K = q.shape
    k = 1.0 - f
    g_hi = g.astype(BF16)
    g_lo = (g - g_hi.astype(F32)).astype(BF16)
    cs = jnp.dot(cmat, jnp.concatenate([g_hi, g_lo], axis=1), preferred_element_type=F32)
    bc = cs[:, :K] + cs[:, K:]
    tot = bc[0:1, :] if rev else bc[C - 1:C, :]

    nt_dims = (((1,), (1,)), ((), ()))
    tn_dims = (((0,), (0,)), ((), ()))
    qd = (q * jnp.exp(bc)).astype(BF16)
    o = lax.dot_general(qd, st.astype(BF16), nt_dims, preferred_element_type=F32)
    kdec = (k * jnp.exp(tot - bc)).astype(BF16)
    st_new = st * jnp.exp(tot) + lax.dot_general(v, kdec, tn_dims, preferred_element_type=F32)

    a = jnp.zeros((C, C), F32)
    f3 = f.reshape(C // SUB, SUB, K)
    qd_d = q
    for d8 in range(SUB):
        f_shift = f if d8 == 0 else pltpu.roll(f3, (SUB - d8) if rev else d8, 1).reshape(C, K)
        qd_next = qd_d * f_shift
        red = jnp.sum(qd_d - qd_next, axis=-1, keepdims=True)
        a = jnp.where(codes == d8, red, a)
        qd_d = qd_next

    zeros8 = jnp.zeros((SUB, K), F32)
    b = SUB
    level = 0
    while b < C:
        ql, kl = [], []
        for r in range(0, C, SUB):
            base = (r // (2 * b)) * 2 * b
            upper = (r - base) >= b
            ref_row = base + b if rev else base + b - 1
            q_side = (not upper) if rev else upper
            bref = bc[ref_row:ref_row + 1, :]
            bg = bc[r:r + SUB, :]
            if q_side:
                ql.append(q[r:r + SUB, :] * jnp.exp(bg - bref))
                kl.append(zeros8)
            else:
                kl.append(k[r:r + SUB, :] * jnp.exp(bref - bg))
                ql.append(zeros8)
        qlb = jnp.concatenate(ql, axis=0).astype(BF16)
        klb = jnp.concatenate(kl, axis=0).astype(BF16)
        p = lax.dot_general(qlb, klb, nt_dims, preferred_element_type=F32)
        a = jnp.where(codes == SUB + level, p, a)
        b *= 2
        level += 1

    o = o + jnp.dot(a.astype(BF16), v, preferred_element_type=F32)
    return o, st_new


def _hgrn_tables(C):
    t = np.arange(C)[:, None]
    s = np.arange(C)[None, :]
    cmats, codes = [], []
    for rev in (False, True):
        d = (s - t) if rev else (t - s)
        cmats.append((d >= 0).astype(np.float32))
        code = np.full((C, C), -1, np.int32)
        inside = (d >= 0) & ((t // SUB) == (s // SUB))
        code[inside] = d[inside]
        b, level = SUB, 0
        while b < C:
            split = (d > 0) & ((t // (2 * b)) == (s // (2 * b))) & ((t // b) != (s // b))
            code[split] = SUB + level
            b *= 2
            level += 1
        codes.append(code)
    return jnp.asarray(np.stack(cmats), BF16), jnp.asarray(np.stack(codes), jnp.int32)


def _hgrn_kernel(q_ref, v_ref, lf_ref, cmat_ref, code_ref, o_ref, st_ref, *, n_lat, n_ctx):
    C = CH
    n_chunks = n_lat + n_ctx
    st_ref[...] = jnp.zeros(st_ref.shape, F32)

    def body(j, carry):
        for d, rev in enumerate((False, True)):
            c = (n_chunks - 1 - j) if rev else jnp.where(j < n_ctx, n_lat + j, j - n_ctx)
            r0 = pl.multiple_of(c * C, C)
            q = q_ref[0, pl.ds(r0, C), :].astype(F32)
            v = v_ref[0, pl.ds(r0, C), :]
            g = lf_ref[d, 0, pl.ds(r0, C), :]
            o, st_new = _hgrn_direction(q, jnp.exp(g), v, g, st_ref[d], cmat_ref[d], code_ref[d], rev)
            st_ref[d] = st_new
            o_ref[d, 0, pl.ds(r0, C), :] = o.astype(o_ref.dtype)
        return carry

    lax.fori_loop(0, n_chunks, body, 0, unroll=HGRN_UNROLL)


def _hgrn(q, v, lf, n_lat_rows):
    B, NT, _ = q.shape
    n_lat = n_lat_rows // CH
    n_ctx = (NT - n_lat_rows) // CH
    assert (n_lat + n_ctx) % HGRN_UNROLL == 0
    cmat, codes = _hgrn_tables(CH)
    head = pl.BlockSpec((1, NT, HG_DIM), lambda b, h: (b, 0, h))
    both = pl.BlockSpec((2, 1, NT, HG_DIM), lambda b, h: (0, b, 0, h))
    table = pl.BlockSpec((2, CH, CH), lambda b, h: (0, 0, 0))
    return pl.pallas_call(
        functools.partial(_hgrn_kernel, n_lat=n_lat, n_ctx=n_ctx),
        grid=(B, HG_HEADS),
        in_specs=[head, head, both, table, table],
        out_specs=both,
        out_shape=jax.ShapeDtypeStruct((2, B, NT, HG_WIDTH), BF16),
        scratch_shapes=[pltpu.VMEM((2, HG_DIM, HG_DIM), F32)],
        compiler_params=_cparams(("arbitrary", "arbitrary")),
        name="hgrn_scan",
    )(q, v, lf, cmat, codes)


def _attn_kernel(qT_ref, k_ref, vT_ref, lam_ref, sub_ref, o_ref, *s_refs, chunks, n_q, tq, out_scale):
    n = len(chunks)
    odd_head = (pl.program_id(1) % 2) == 1
    ones_row = lax.broadcasted_iota(jnp.int32, (VT_ROWS - DA_V, n * KC), 0) == 0
    vt = jnp.concatenate(
        [vT_ref[0, :, chunks[0] * KC:(chunks[-1] + 1) * KC],
         jnp.where(ones_row, 1.0, 0.0).astype(BF16)], axis=0)
    streams = [(t, mp) for t in range(n_q) for mp in range(2)]
    zeros = jnp.zeros((DA_QK, tq), BF16)

    def query_operand(t, mp):
        q = qT_ref[0, mp * DA_QK:(mp + 1) * DA_QK, t * tq:(t + 1) * tq]
        even = jnp.concatenate([q if j == mp else zeros for j in range(4)], axis=0)
        odd = jnp.concatenate([q if j == 2 + mp else zeros for j in range(4)], axis=0)
        return jnp.where(odd_head, odd, even)

    row0 = pl.multiple_of(jnp.minimum(pl.program_id(2), 0) * KC, KC)

    def pass1(slot):
        rhs = query_operand(*streams[slot])
        m8 = None
        for i, c in enumerate(chunks):
            s = jnp.dot(k_ref[0, c * KC:(c + 1) * KC, :], rhs, preferred_element_type=F32)
            s_refs[slot % len(s_refs)][i * KC:(i + 1) * KC, :] = s
            parts = [s[r:r + 8, :] for r in range(0, KC, 8)]
            while len(parts) > 1:
                parts = [jnp.maximum(parts[j], parts[j + 1]) for j in range(0, len(parts), 2)]
            m8 = parts[0] if m8 is None else jnp.maximum(m8, parts[0])
        return jnp.max(m8, axis=0, keepdims=True)

    def pass2(slot, m):
        p = jnp.exp2((s_refs[slot % len(s_refs)][pl.ds(row0, n * KC), :] - m).astype(BF16))
        return jnp.dot(vt, p, preferred_element_type=F32)

    accs, maxes = {}, {}
    for slot in range(len(streams) + 1):
        if slot < len(streams):
            maxes[slot] = pass1(slot)
        if slot >= 1:
            accs[streams[slot - 1]] = pass2(slot - 1, maxes[slot - 1])

    for t in range(n_q):
        acc0, acc1 = accs[(t, 0)], accs[(t, 1)]
        o = (acc0[0:DA_V] / acc0[DA_V:DA_V + 1]
             - lam_ref[...] * (acc1[0:DA_V] / acc1[DA_V:DA_V + 1]))
        y = o * lax.rsqrt(jnp.mean(o * o, axis=0, keepdims=True) + EPS)
        o_ref[0, :, t * tq:(t + 1) * tq] = (y * sub_ref[...] * out_scale).astype(BF16)


def _attention_call(dqT, dk, dvT, lam_l, subln_l, chunks, n_q, tq, q_block0, n_steps, lam_init):
    B, _, NT = dqT.shape
    w = n_q * tq
    kern = functools.partial(_attn_kernel, chunks=tuple(chunks), n_q=n_q, tq=tq, out_scale=1.0 - lam_init)
    return pl.pallas_call(
        kern,
        grid=(B, DA_HEADS, n_steps),
        in_specs=[
            pl.BlockSpec((1, 2 * DA_QK, w), lambda b, h, i: (b, h, q_block0 + i)),
            pl.BlockSpec((1, NT, 2 * DA_V), lambda b, h, i: (b, 0, h // 2)),
            pl.BlockSpec((1, DA_V, NT), lambda b, h, i: (b, h, 0)),
            pl.BlockSpec((1, tq), lambda b, h, i: (0, 0)),
            pl.BlockSpec((DA_V, 1), lambda b, h, i: (0, 0)),
        ],
        out_specs=pl.BlockSpec((1, DA_V, w), lambda b, h, i: (b, h, i)),
        out_shape=jax.ShapeDtypeStruct((B, DA_WIDTH, n_steps * w), BF16),
        scratch_shapes=[pltpu.VMEM((len(chunks) * KC, tq), F32) for _ in range(min(2 * n_q, ATT_BUFFERS))],
        compiler_params=_cparams(("arbitrary", "arbitrary", "arbitrary")),
        name="diff_attention",
    )(dqT, dk, dvT, lam_l, subln_l)


def _attention(dqT, dk, dvT, lam_l, subln_l, n_lat_rows, lam_init, with_ctx):
    NT = dqT.shape[2]
    n_chunks, n_lat_chunks = NT // KC, n_lat_rows // KC
    n_ctx_rows = NT - n_lat_rows
    lat = _attention_call(dqT, dk, dvT, lam_l, subln_l, range(n_chunks), ATT_TILES, TQ, 0,
                          n_lat_rows // (ATT_TILES * TQ), lam_init)
    if not with_ctx:
        return lat, lat
    ctx = _attention_call(dqT, dk, dvT, lam_l, subln_l, range(n_lat_chunks, n_chunks), 1, TQ_CTX,
                          n_lat_rows // TQ_CTX, n_ctx_rows // TQ_CTX, lam_init)
    return lat, ctx


def _fourier_kernel(ct_ref, st_ref, f_ref, o_ref):
    o_ref[...] = (jnp.dot(ct_ref[...], f_ref[0], preferred_element_type=F32)
                  + jnp.dot(st_ref[...], f_ref[1], preferred_element_type=F32)).astype(BF16)


def _fourier(ct, st_neg, f, row_block, n_rows):
    W = f.shape[2]
    tk = min(TK, n_rows)
    return pl.pallas_call(
        _fourier_kernel,
        grid=(n_rows // tk,),
        in_specs=[
            pl.BlockSpec((tk, n_rows), lambda i: (i, 0)),
            pl.BlockSpec((tk, n_rows), lambda i: (i, 0)),
            pl.BlockSpec((2, n_rows, W), lambda i: (0, row_block, 0)),
        ],
        out_specs=pl.BlockSpec((tk, W), lambda i: (i, 0)),
        out_shape=jax.ShapeDtypeStruct((n_rows, W), BF16),
        compiler_params=_cparams(("arbitrary",)),
        name="fourier_mix",
    )(ct, st_neg, f)


def _fourier_folded_kernel(ct_ref, st_ref, f_ref, up_ref, fmid_ref, o_ref, xs_ref, xa_ref, *, scale):
    tk = o_ref.shape[0]

    @pl.when(pl.program_id(0) == 0)
    def _():
        n_tiles = xs_ref.shape[0] // tk
        r = lax.broadcasted_iota(jnp.int32, (tk, tk), 0)
        c = lax.broadcasted_iota(jnp.int32, (tk, tk), 1)
        flip = jnp.where(r + c == tk - 1, 1.0, 0.0).astype(BF16)
        first_row = lax.broadcasted_iota(jnp.int32, (tk, xs_ref.shape[1]), 0) == 0
        for j in range(n_tiles):
            rows = slice(j * tk, (j + 1) * tk)
            src = slice((n_tiles - 1 - j) * tk, (n_tiles - j) * tk)
            m0 = jnp.dot(flip, up_ref[0, src, :], preferred_element_type=F32)
            m1 = jnp.dot(flip, up_ref[1, src, :], preferred_element_type=F32)
            if j == 0:
                m0 = jnp.where(first_row, 0.0, m0)
            xs_ref[rows, :] = (f_ref[0, rows, :].astype(F32) + m0).astype(BF16)
            xa_ref[rows, :] = (f_ref[1, rows, :].astype(F32) - m1).astype(BF16)

    k = pl.program_id(0) * tk + lax.broadcasted_iota(jnp.int32, (tk, 1), 0)
    sign = (1 - 2 * (k & 1)).astype(F32)
    mid = fmid_ref[0, 0:1, :].astype(F32) * scale
    o_ref[...] = (jnp.dot(ct_ref[...], xs_ref[...], preferred_element_type=F32)
                  + jnp.dot(st_ref[...], xa_ref[...], preferred_element_type=F32)
                  + sign * mid).astype(BF16)


def _fourier_folded(ct_half, st_neg_half, f, n_rows):
    W = f.shape[2]
    half = n_rows // 2
    up = jnp.concatenate([f[:, half + 1:n_rows], f[:, 0:1]], axis=1)
    single = dict(pipeline_mode=pl.Buffered(1))
    return pl.pallas_call(
        functools.partial(_fourier_folded_kernel, scale=1.0 / math.sqrt(n_rows * FT_GDIM)),
        grid=(n_rows // TK,),
        in_specs=[
            pl.BlockSpec((TK, half), lambda i: (i, 0)),
            pl.BlockSpec((TK, half), lambda i: (i, 0)),
            pl.BlockSpec((2, half, W), lambda i: (0, 0, 0), **single),
            pl.BlockSpec((2, half, W), lambda i: (0, 0, 0), **single),
            pl.BlockSpec((2, 2 * SUB, W), lambda i: (0, half // (2 * SUB), 0), **single),
        ],
        out_specs=pl.BlockSpec((TK, W), lambda i: (i, 0)),
        out_shape=jax.ShapeDtypeStruct((n_rows, W), BF16),
        scratch_shapes=[pltpu.VMEM((half, W), BF16), pltpu.VMEM((half, W), BF16)],
        compiler_params=_cparams(("arbitrary",)),
        name="fourier_mix_folded",
    )(ct_half, st_neg_half, f, up, f)


N_POST_IN = 13


def _post_kernel(*refs, n_lat_tiles):
    refs[N_POST_IN][0] = _post_compute(*refs[:N_POST_IN], n_lat_tiles=n_lat_tiles)


def _post_inproj_kernel(*refs, n_lat_tiles):
    n_in = N_POST_IN + 7
    x_new = _post_compute(*refs[:N_POST_IN], n_lat_tiles=n_lat_tiles)
    refs[n_in][0] = x_new
    _inproj_compute(x_new, *refs[N_POST_IN:n_in], *refs[n_in + 1:])


def _post_compute(x_ref, mod_ref, g_ref, of_ref, gate_ref, dal_ref, dac_ref, ftl_ref, ftc_ref, onorm_ref,
                  wo_ref, wfi_ref, wfo_ref, *, n_lat_tiles):
    x = x_ref[0]
    mod = mod_ref[0]
    is_lat = pl.program_id(1) < n_lat_tiles
    ft = jnp.where(is_lat, ftl_ref[...], ftc_ref[...])
    da_t = jnp.where(is_lat, dal_ref[0], dac_ref[0])
    o = of_ref[0, 0].astype(F32) + of_ref[1, 0].astype(F32)
    onorm = onorm_ref[...]
    heads = [_rms(o[:, h * HG_DIM:(h + 1) * HG_DIM]) * onorm for h in range(HG_HEADS)]
    hg = (jnp.concatenate(heads, axis=-1) * gate_ref[0].astype(F32)).astype(BF16)
    y = (jnp.dot(hg, wo_ref[0, 0:HG_WIDTH, :], preferred_element_type=F32)
         + lax.dot_general(da_t, wo_ref[0, HG_WIDTH:HG_WIDTH + DA_WIDTH, :], (((0,), (0,)), ((), ())),
                           preferred_element_type=F32)
         + jnp.dot(ft, wo_ref[0, HG_WIDTH + DA_WIDTH:, :], preferred_element_type=F32))
    x1 = x + _rms(y) * (mod[2:3, :] * g_ref[1:2, :])
    h2 = (_rms(x1) * (g_ref[2:3, :] * (1.0 + mod[4:5, :])) + mod[3:4, :]).astype(BF16)
    gu = jnp.dot(h2, wfi_ref[0], preferred_element_type=F32)
    act = (_silu(gu[:, :FF_HIDDEN]) * gu[:, FF_HIDDEN:]).astype(BF16)
    y2 = jnp.dot(act, wfo_ref[0], preferred_element_type=F32)
    return x1 + _rms(y2) * (mod[5:6, :] * g_ref[3:4, :])


def _post_in_specs(B, g_l, onorm_l, wo, wfi, wfo, layer, n_lat_tiles):
    const = lambda a: pl.BlockSpec(a.shape, lambda b, i: (0,) * a.ndim, pipeline_mode=pl.Buffered(1))
    weight = lambda a: pl.BlockSpec((1,) + a.shape[1:], lambda b, i: (layer, 0, 0), pipeline_mode=pl.Buffered(1))
    specs = [
        _tok(D_MODEL),
        _mod_spec(B, n_lat_tiles),
        const(g_l),
        pl.BlockSpec((2, 1, TM, HG_WIDTH), lambda b, i: (0, b, i, 0)),
        _tok(HG_WIDTH),
        pl.BlockSpec((1, DA_WIDTH, TM), lambda b, i: (b, 0, jnp.minimum(i, n_lat_tiles - 1))),
        pl.BlockSpec((1, DA_WIDTH, TM), lambda b, i: (b, 0, jnp.maximum(i - n_lat_tiles, 0))),
        pl.BlockSpec((TM, FT_WIDTH), lambda b, i: (jnp.minimum(i, n_lat_tiles - 1), b)),
        pl.BlockSpec((TM, FT_WIDTH), lambda b, i: (jnp.maximum(i - n_lat_tiles, 0), b)),
        const(onorm_l), weight(wo), weight(wfi), weight(wfo),
    ]
    assert len(specs) == N_POST_IN
    return specs


def _post(X, mod_l, g_l, of, gate, da_lat, da_ctx, ft_lat, ft_ctx, onorm_l, wo, wfi, wfo, layer, n_lat_tiles, n_tiles):
    B, NT, D = X.shape
    return pl.pallas_call(
        functools.partial(_post_kernel, n_lat_tiles=n_lat_tiles),
        grid=(B, n_tiles),
        in_specs=_post_in_specs(B, g_l, onorm_l, wo, wfi, wfo, layer, n_lat_tiles),
        out_specs=_tok(D),
        out_shape=jax.ShapeDtypeStruct((B, n_tiles * TM, D), F32),
        compiler_params=_cparams(("arbitrary", "arbitrary")),
        name="outproj_ffn",
    )(X, mod_l, g_l, of, gate, da_lat, da_ctx, ft_lat, ft_ctx, onorm_l, wo, wfi, wfo)


def _post_inproj(X, post_args, inproj_args, layer, n_lat_tiles):
    B, NT, D = X.shape
    mod_l, g_l, of, gate, da_lat, da_ctx, ft_lat, ft_ctx, onorm_l, wo, wfi, wfo = post_args
    mod_n, g_n, w_in, cos_t, sin_t, lb_n, cs_tab = inproj_args
    in_specs, out_specs, out_shapes = _inproj_io(B, NT, g_n, w_in, layer + 1, lb_n, cs_tab, n_lat_tiles, True)
    outs = pl.pallas_call(
        functools.partial(_post_inproj_kernel, n_lat_tiles=n_lat_tiles),
        grid=(B, NT // TM),
        in_specs=_post_in_specs(B, g_l, onorm_l, wo, wfi, wfo, layer, n_lat_tiles) + in_specs,
        out_specs=(_tok(D),) + tuple(out_specs),
        out_shape=(jax.ShapeDtypeStruct((B, NT, D), F32),) + tuple(out_shapes),
        compiler_params=_cparams(("arbitrary", "arbitrary")),
        name="outproj_ffn_inproj",
    )(X, *post_args, *inproj_args)
    return outs[0], outs[1:]


def _rope_tables(t_lat, t_ctx):
    pos = jnp.arange(t_lat)
    inv_freq = 1.0 / (ROPE_THETA ** (jnp.arange(0, ROPE_AXIS_DIM, 2, dtype=F32) / ROPE_AXIS_DIM))
    ang = jnp.stack([pos // GRID_W, pos % GRID_W], axis=-1).astype(F32)[:, :, None] * inv_freq
    cos, sin = jnp.cos(ang), jnp.sin(ang)
    cos32 = jnp.stack([cos, cos], axis=2).reshape(t_lat, DA_QK)
    sin32 = jnp.stack([-sin, sin], axis=2).reshape(t_lat, DA_QK)
    reps = DA_WIDTH // DA_QK
    cos_t = jnp.concatenate([jnp.tile(cos32, (1, reps)), jnp.ones((t_ctx, DA_WIDTH), F32)], axis=0)
    sin_t = jnp.concatenate([jnp.tile(sin32, (1, reps)), jnp.zeros((t_ctx, DA_WIDTH), F32)], axis=0)
    return cos_t, sin_t


def _channel_table():
    idx = np.arange(FT_GDIM)
    ang = 2.0 * np.pi * ((idx[:, None] * idx[None, :]) % FT_GDIM) / FT_GDIM
    eye = np.eye(FT_GROUPS)
    return jnp.asarray(np.concatenate([np.kron(eye, np.cos(ang)), np.kron(eye, np.sin(ang))], axis=1), BF16)


def _position_tables(n, cols):
    lo = 1
    while lo * lo < n:
        lo *= 2
    hi = n // lo
    k = jnp.arange(n)
    a1 = 2.0 * np.pi * ((k[:, None] * jnp.arange(cols // lo)[None, :]) % hi).astype(F32) / hi
    a2 = 2.0 * np.pi * ((k[:, None] * jnp.arange(lo)[None, :]) % n).astype(F32) / n
    c1, s1 = jnp.cos(a1)[:, :, None], jnp.sin(a1)[:, :, None]
    c2, s2 = jnp.cos(a2)[:, None, :], jnp.sin(a2)[:, None, :]
    scale = 1.0 / math.sqrt(n * FT_GDIM)
    ct = ((c1 * c2 - s1 * s2) * scale).reshape(n, cols).astype(BF16)
    st_neg = ((s1 * c2 + c1 * s2) * (-scale)).reshape(n, cols).astype(BF16)
    return ct, st_neg


def kernel(x, c, ctx, c_ctx, w_mod, b_mod, norm_g, w_in, w_out, hg_lb_logits, hg_onorm,
           da_lambda, da_subln, w_ffn_in, w_ffn_out):
    B, T, D = x.shape
    Tc = ctx.shape[1]
    depth = w_mod.shape[0]
    assert D == D_MODEL and B + 1 <= MOD_ROWS
    assert T % TM == 0 and Tc % TM == 0 and T % KC == 0 and Tc % KC == 0 and T % CH == 0 and Tc % CH == 0
    n_lat_tiles = T // TM
    NT = T + Tc

    lam_init = [0.8 - 0.6 * math.exp(-0.3 * l) for l in range(depth)]
    lb, lam = _prep(hg_lb_logits, da_lambda, lam_init)

    cond = jnp.concatenate([c.astype(F32), c_ctx.astype(F32)[None, :],
                            jnp.zeros((MOD_ROWS - B - 1, D), F32)], axis=0)
    mods = _modulation(cond, w_mod, b_mod).reshape(depth, MOD_ROWS, N_MOD, D)

    cos_t, sin_t = _rope_tables(T, Tc)
    cs_tab = _channel_table()
    ct_lat, st_lat = _position_tables(T, T // 2)
    ct_ctx, st_ctx = _position_tables(Tc, Tc)

    w_in_b, w_out_b = w_in.astype(BF16), w_out.astype(BF16)
    w_ffn_in_b, w_ffn_out_b = w_ffn_in.astype(BF16), w_ffn_out.astype(BF16)
    X = jnp.concatenate([x, ctx], axis=1).astype(F32)
    g_all = norm_g.astype(F32)
    projected = _inproj(X, mods[0], g_all[0], w_in_b, 0, cos_t, sin_t, lb[0], cs_tab, n_lat_tiles)
    for l in range(depth):
        q, vi, gate, lf, dq, dk, dv, f = projected
        of = _hgrn(q, vi, lf, T)
        with_ctx = l < depth - 1
        da_lat, da_ctx = _attention(dq, dk, dv, lam[l:l + 1], da_subln[l].astype(F32).reshape(DA_V, 1), T,
                                    lam_init[l], with_ctx)
        ft_lat = _fourier_folded(ct_lat, st_lat, f, T)
        ft_ctx = _fourier(ct_ctx, st_ctx, f, T // Tc, Tc) if with_ctx else ft_lat
        post_args = (mods[l], g_all[l], of, gate, da_lat, da_ctx, ft_lat, ft_ctx,
                     hg_onorm[l].astype(F32).reshape(1, HG_DIM), w_out_b, w_ffn_in_b, w_ffn_out_b)
        if with_ctx:
            next_args = (mods[l + 1], g_all[l + 1], w_in_b, cos_t, sin_t, lb[l + 1], cs_tab)
            X, projected = _post_inproj(X, post_args, next_args, l, n_lat_tiles)
        else:
            X = _post(X, *post_args, l, n_lat_tiles, n_lat_tiles)
    return X.astype(x.dtype)
```

```python
import functools
import math

import numpy as np
import jax
import jax.numpy as jnp
from jax import lax
from jax.experimental import pallas as pl
from jax.experimental.pallas import tpu as pltpu

F32 = jnp.float32
BF16 = jnp.bfloat16

D_MODEL = 1024
GRID_W = 64
HG_WIDTH = 512
HG_DIM = 128
HG_HEADS = 4
DA_WIDTH = 256
DA_HEADS = 4
DA_V = 64
DA_QK = 32
FT_WIDTH = 256
FT_GROUPS = 4
FT_GDIM = FT_WIDTH // FT_GROUPS
FF_HIDDEN = 2816
N_MOD = 6
EPS = 1e-6
ROPE_THETA = 10000.0
ROPE_AXIS_DIM = DA_QK // 2

_OFF_Q, _OFF_I, _OFF_G, _OFF_F, _OFF_DQ, _OFF_DK, _OFF_DV, _OFF_FT, IN_WIDTH = (
    0, 512, 1024, 1536, 2560, 2816, 3072, 3328, 3584)

TM = 256
TQ = 256
TQ_CTX = 256
ATT_TILES = 8
ATT_BUFFERS = 4
KC = 256
CH = 128
SUB = 8
HGRN_UNROLL = 17
TK = 256
VT_ROWS = 80
MOD_ROWS = 8
MOD_TN = 1536
VMEM_LIMIT = 52 * 1024 * 1024

Q_SCALE = (DA_QK ** -0.5) * math.log2(math.e)


def _silu(x):
    return x * jax.nn.sigmoid(x)


def _rms(x):
    return x * lax.rsqrt(jnp.mean(x * x, axis=-1, keepdims=True) + EPS)


def _cparams(sem):
    return pltpu.CompilerParams(dimension_semantics=sem, vmem_limit_bytes=VMEM_LIMIT)


def _mod_kernel(c_ref, w_ref, b_ref, o_ref):
    a = _silu(c_ref[...]).astype(BF16)
    w = w_ref[0].astype(BF16)
    o_ref[0] = jnp.dot(a, w, preferred_element_type=F32) + b_ref[0]


def _modulation(cond, w_mod, b_mod):
    depth, d, n = w_mod.shape
    return pl.pallas_call(
        _mod_kernel,
        grid=(depth, n // MOD_TN),
        in_specs=[
            pl.BlockSpec((MOD_ROWS, d), lambda l, j: (0, 0)),
            pl.BlockSpec((1, d, MOD_TN), lambda l, j: (l, 0, j)),
            pl.BlockSpec((1, 1, MOD_TN), lambda l, j: (l, 0, j)),
        ],
        out_specs=pl.BlockSpec((1, MOD_ROWS, MOD_TN), lambda l, j: (l, 0, j)),
        out_shape=jax.ShapeDtypeStruct((depth, MOD_ROWS, n), F32),
        compiler_params=_cparams(("arbitrary", "arbitrary")),
        name="modulation",
    )(cond, w_mod, b_mod.reshape(depth, 1, n))


def _prep_kernel(lb_ref, lam_ref, lam_init_ref, lbo_ref, lamo_ref, *, depth):
    rows = [lb_ref[l:l + 1, :] for l in range(depth)]
    m = rows[0]
    for r in rows[1:]:
        m = jnp.maximum(m, r)
    e = [jnp.exp(r - m) for r in rows]
    tot = e[0]
    for r in e[1:]:
        tot = tot + r
    p = [r / tot for r in e]
    acc = p[0]
    lbo_ref[0:1, :] = acc - p[0]
    for l in range(1, depth):
        acc = acc + p[l]
        lbo_ref[l:l + 1, :] = acc - p[0]
    x = lam_ref[...]
    a = jnp.sum(x[:, 0:DA_QK] * x[:, DA_QK:2 * DA_QK], axis=-1, keepdims=True)
    b = jnp.sum(x[:, 2 * DA_QK:3 * DA_QK] * x[:, 3 * DA_QK:4 * DA_QK], axis=-1, keepdims=True)
    lamo_ref[...] = jnp.exp(a) - jnp.exp(b) + lam_init_ref[...]


def _prep(hg_lb_logits, da_lambda, lam_init):
    depth = hg_lb_logits.shape[1]
    lb_in = jnp.transpose(hg_lb_logits.astype(F32), (1, 0, 2)).reshape(depth, 2 * HG_WIDTH)
    lam_in = da_lambda.astype(F32).reshape(depth, 4 * DA_QK)
    lam_init_arr = jnp.asarray(np.broadcast_to(np.asarray(lam_init, np.float32)[:, None], (depth, TQ)))
    lb, lam = pl.pallas_call(
        functools.partial(_prep_kernel, depth=depth),
        out_shape=(jax.ShapeDtypeStruct((depth, 2 * HG_WIDTH), F32),
                   jax.ShapeDtypeStruct((depth, TQ), F32)),
        name="param_prep",
    )(lb_in, lam_in, lam_init_arr)
    return lb.reshape(depth, 2, HG_WIDTH), lam


def _inproj_kernel(x_ref, *refs):
    _inproj_compute(x_ref[0], *refs)


def _inproj_compute(x, mod_ref, g_ref, w_ref, cos_ref, sin_ref, lb_ref, cs_ref,
                    q_ref, i_ref, gate_ref, lf_ref, dq_ref, dk_ref, dv_ref, f_ref):
    mod = mod_ref[0]
    h = _rms(x) * (g_ref[0:1, :] * (1.0 + mod[1:2, :])) + mod[0:1, :]
    hb = h.astype(BF16)

    def proj(a, b):
        return jnp.dot(hb, w_ref[0, :, a:b], preferred_element_type=F32)

    q_ref[0] = _silu(proj(_OFF_Q, _OFF_I)).astype(BF16)
    i_ref[0] = proj(_OFF_I, _OFF_G).astype(BF16)
    gate_ref[0] = _silu(proj(_OFF_G, _OFF_F)).astype(BF16)
    for d in range(2):
        z = proj(_OFF_F + d * HG_WIDTH, _OFF_F + (d + 1) * HG_WIDTH)
        lb = lb_ref[d:d + 1, :]
        lf_ref[d, 0] = jnp.log(lb + (1.0 - lb) * jax.nn.sigmoid(z))

    cos = cos_ref[...]
    sin = sin_ref[...]
    lane = lax.broadcasted_iota(jnp.int32, cos.shape, 1)
    upper_half = (lane & (ROPE_AXIS_DIM // 2)) != 0

    def rope(t):
        partner = jnp.where(upper_half,
                            pltpu.roll(t, ROPE_AXIS_DIM // 2, 1),
                            pltpu.roll(t, 2 * DA_HEADS * DA_QK - ROPE_AXIS_DIM // 2, 1))
        return t * cos + partner * sin

    dq_ref[0] = (rope(proj(_OFF_DQ, _OFF_DK)) * Q_SCALE).T.astype(BF16)
    dk_ref[0] = rope(proj(_OFF_DK, _OFF_DV)).astype(BF16)
    dv_ref[0] = proj(_OFF_DV, _OFF_FT).T.astype(BF16)
    ft = proj(_OFF_FT, IN_WIDTH).astype(BF16)
    cs = jnp.dot(ft, cs_ref[...], preferred_element_type=F32)
    f_ref[0] = cs[:, :FT_WIDTH].astype(BF16)
    f_ref[1] = cs[:, FT_WIDTH:].astype(BF16)


def _tok(w):
    return pl.BlockSpec((1, TM, w), lambda b, i: (b, i, 0))


def _mod_spec(B, n_lat_tiles):
    return pl.BlockSpec((1, N_MOD, D_MODEL), lambda b, i: (jnp.where(i < n_lat_tiles, b, B), 0, 0))


def _inproj_io(B, NT, g_l, w_in, layer, lb_l, cs_tab, n_lat_tiles, single):
    tok = _tok
    mode = dict(pipeline_mode=pl.Buffered(1)) if single else {}
    full = lambda a: pl.BlockSpec(a.shape, lambda b, i: (0,) * a.ndim, **mode)
    tok_t = pl.BlockSpec((1, DA_WIDTH, TM), lambda b, i: (b, 0, i))
    in_specs = [
        _mod_spec(B, n_lat_tiles),
        full(g_l),
        pl.BlockSpec((1,) + w_in.shape[1:], lambda b, i: (layer, 0, 0), **mode),
        pl.BlockSpec((TM, DA_WIDTH), lambda b, i: (i, 0)),
        pl.BlockSpec((TM, DA_WIDTH), lambda b, i: (i, 0)),
        full(lb_l), full(cs_tab),
    ]
    out_shapes = (
        jax.ShapeDtypeStruct((B, NT, HG_WIDTH), BF16),
        jax.ShapeDtypeStruct((B, NT, HG_WIDTH), BF16),
        jax.ShapeDtypeStruct((B, NT, HG_WIDTH), BF16),
        jax.ShapeDtypeStruct((2, B, NT, HG_WIDTH), F32),
        jax.ShapeDtypeStruct((B, DA_WIDTH, NT), BF16),
        jax.ShapeDtypeStruct((B, NT, DA_WIDTH), BF16),
        jax.ShapeDtypeStruct((B, DA_WIDTH, NT), BF16),
        jax.ShapeDtypeStruct((2, NT, B * FT_WIDTH), BF16),
    )
    out_specs = (
        tok(HG_WIDTH), tok(HG_WIDTH), tok(HG_WIDTH),
        pl.BlockSpec((2, 1, TM, HG_WIDTH), lambda b, i: (0, b, i, 0)),
        tok_t, tok(DA_WIDTH), tok_t,
        pl.BlockSpec((2, TM, FT_WIDTH), lambda b, i: (0, i, b)),
    )
    return in_specs, out_specs, out_shapes


def _inproj(X, mod_l, g_l, w_in, layer, cos_t, sin_t, lb_l, cs_tab, n_lat_tiles):
    B, NT, D = X.shape
    in_specs, out_specs, out_shapes = _inproj_io(B, NT, g_l, w_in, layer, lb_l, cs_tab, n_lat_tiles, False)
    return pl.pallas_call(
        _inproj_kernel,
        grid=(B, NT // TM),
        in_specs=[_tok(D)] + in_specs,
        out_specs=out_specs,
        out_shape=out_shapes,
        compiler_params=_cparams(("arbitrary", "arbitrary")),
        name="adaln_inproj",
    )(X, mod_l, g_l, w_in, cos_t, sin_t, lb_l, cs_tab)


def _hgrn_direction(q, f, v, g, st, cmat, codes, rev):
    C, K = q.shape
    k = 1.0 - f
    g_hi = g.astype(BF16)
    g_lo = (g - g_hi.astype(F32)).astype(BF16)
    cs = jnp.dot(cmat, jnp.concatenate([g_hi, g_lo], axis=1), preferred_element_type=F32)
    bc = cs[:, :K] + cs[:, K:]
    tot = bc[0:1, :] if rev else bc[C - 1:C, :]

    nt_dims = (((1,), (1,)), ((), ()))
    tn_dims = (((0,), (0,)), ((), ()))
    qd = (q * jnp.exp(bc)).astype(BF16)
    o = lax.dot_general(qd, st.astype(BF16), nt_dims, preferred_element_type=F32)
    kdec = (k * jnp.exp(tot - bc)).astype(BF16)
    st_new = st * jnp.exp(tot) + lax.dot_general(v, kdec, tn_dims, preferred_element_type=F32)

    a = jnp.zeros((C, C), F32)
    f3 = f.reshape(C // SUB, SUB, K)
    qd_d = q
    for d8 in range(SUB):
        f_shift = f if d8 == 0 else pltpu.roll(f3, (SUB - d8) if rev else d8, 1).reshape(C, K)
        qd_next = qd_d * f_shift
        red = jnp.sum(qd_d - qd_next, axis=-1, keepdims=True)
        a = jnp.where(codes == d8, red, a)
        qd_d = qd_next

    zeros8 = jnp.zeros((SUB, K), F32)
    b = SUB
    level = 0
    while b < C:
        ql, kl = [], []
        for r in range(0, C, SUB):
            base = (r // (2 * b)) * 2 * b
            upper = (r - base) >= b
            ref_row = base + b if rev else base + b - 1
            q_side = (not upper) if rev else upper
            bref = bc[ref_row:ref_row + 1, :]
            bg = bc[r:r + SUB, :]
            if q_side:
                ql.append(q[r:r + SUB, :] * jnp.exp(bg - bref))
                kl.append(zeros8)
            else:
                kl.append(k[r:r + SUB, :] * jnp.exp(bref - bg))
                ql.append(zeros8)
        qlb = jnp.concatenate(ql, axis=0).astype(BF16)
        klb = jnp.concatenate(kl, axis=0).astype(BF16)
        p = lax.dot_general(qlb, klb, nt_dims, preferred_element_type=F32)
        a = jnp.where(codes == SUB + level, p, a)
        b *= 2
        level += 1

    o = o + jnp.dot(a.astype(BF16), v, preferred_element_type=F32)
    return o, st_new


def _hgrn_tables(C):
    t = np.arange(C)[:, None]
    s = np.arange(C)[None, :]
    cmats, codes = [], []
    for rev in (False, True):
        d = (s - t) if rev else (t - s)
        cmats.append((d >= 0).astype(np.float32))
        code = np.full((C, C), -1, np.int32)
        inside = (d >= 0) & ((t // SUB) == (s // SUB))
        code[inside] = d[inside]
        b, level = SUB, 0
        while b < C:
            split = (d > 0) & ((t // (2 * b)) == (s // (2 * b))) & ((t // b) != (s // b))
            code[split] = SUB + level
            b *= 2
            level += 1
        codes.append(code)
    return jnp.asarray(np.stack(cmats), BF16), jnp.asarray(np.stack(codes), jnp.int32)


def _hgrn_kernel(q_ref, v_ref, lf_ref, cmat_ref, code_ref, o_ref, st_ref, *, n_lat, n_ctx):
    C = CH
    n_chunks = n_lat + n_ctx
    st_ref[...] = jnp.zeros(st_ref.shape, F32)

    def body(j, carry):
        for d, rev in enumerate((False, True)):
            c = (n_chunks - 1 - j) if rev else jnp.where(j < n_ctx, n_lat + j, j - n_ctx)
            r0 = pl.multiple_of(c * C, C)
            q = q_ref[0, pl.ds(r0, C), :].astype(F32)
            v = v_ref[0, pl.ds(r0, C), :]
            g = lf_ref[d, 0, pl.ds(r0, C), :]
            o, st_new = _hgrn_direction(q, jnp.exp(g), v, g, st_ref[d], cmat_ref[d], code_ref[d], rev)
            st_ref[d] = st_new
            o_ref[d, 0, pl.ds(r0, C), :] = o.astype(o_ref.dtype)
        return carry

    lax.fori_loop(0, n_chunks, body, 0, unroll=HGRN_UNROLL)


def _hgrn(q, v, lf, n_lat_rows):
    B, NT, _ = q.shape
    n_lat = n_lat_rows // CH
    n_ctx = (NT - n_lat_rows) // CH
    assert (n_lat + n_ctx) % HGRN_UNROLL == 0
    cmat, codes = _hgrn_tables(CH)
    head = pl.BlockSpec((1, NT, HG_DIM), lambda b, h: (b, 0, h))
    both = pl.BlockSpec((2, 1, NT, HG_DIM), lambda b, h: (0, b, 0, h))
    table = pl.BlockSpec((2, CH, CH), lambda b, h: (0, 0, 0))
    return pl.pallas_call(
        functools.partial(_hgrn_kernel, n_lat=n_lat, n_ctx=n_ctx),
        grid=(B, HG_HEADS),
        in_specs=[head, head, both, table, table],
        out_specs=both,
        out_shape=jax.ShapeDtypeStruct((2, B, NT, HG_WIDTH), BF16),
        scratch_shapes=[pltpu.VMEM((2, HG_DIM, HG_DIM), F32)],
        compiler_params=_cparams(("arbitrary", "arbitrary")),
        name="hgrn_scan",
    )(q, v, lf, cmat, codes)


def _attn_kernel(qT_ref, k_ref, vT_ref, lam_ref, sub_ref, o_ref, *s_refs, chunks, n_q, tq, out_scale):
    n = len(chunks)
    odd_head = (pl.program_id(1) % 2) == 1
    ones_row = lax.broadcasted_iota(jnp.int32, (VT_ROWS - DA_V, n * KC), 0) == 0
    vt = jnp.concatenate(
        [vT_ref[0, :, chunks[0] * KC:(chunks[-1] + 1) * KC],
         jnp.where(ones_row, 1.0, 0.0).astype(BF16)], axis=0)
    streams = [(t, mp) for t in range(n_q) for mp in range(2)]
    zeros = jnp.zeros((DA_QK, tq), BF16)

    def query_operand(t, mp):
        q = qT_ref[0, mp * DA_QK:(mp + 1) * DA_QK, t * tq:(t + 1) * tq]
        even = jnp.concatenate([q if j == mp else zeros for j in range(4)], axis=0)
        odd = jnp.concatenate([q if j == 2 + mp else zeros for j in range(4)], axis=0)
        return jnp.where(odd_head, odd, even)

    row0 = pl.multiple_of(jnp.minimum(pl.program_id(2), 0) * KC, KC)

    def pass1(slot):
        rhs = query_operand(*streams[slot])
        m8 = None
        for i, c in enumerate(chunks):
            s = jnp.dot(k_ref[0, c * KC:(c + 1) * KC, :], rhs, preferred_element_type=F32)
            s_refs[slot % len(s_refs)][i * KC:(i + 1) * KC, :] = s
            parts = [s[r:r + 8, :] for r in range(0, KC, 8)]
            while len(parts) > 1:
                parts = [jnp.maximum(parts[j], parts[j + 1]) for j in range(0, len(parts), 2)]
            m8 = parts[0] if m8 is None else jnp.maximum(m8, parts[0])
        return jnp.max(m8, axis=0, keepdims=True)

    def pass2(slot, m):
        p = jnp.exp2((s_refs[slot % len(s_refs)][pl.ds(row0, n * KC), :] - m).astype(BF16))
        return jnp.dot(vt, p, preferred_element_type=F32)

    accs, maxes = {}, {}
    for slot in range(len(streams) + 1):
        if slot < len(streams):
            maxes[slot] = pass1(slot)
        if slot >= 1:
            accs[streams[slot - 1]] = pass2(slot - 1, maxes[slot - 1])

    for t in range(n_q):
        acc0, acc1 = accs[(t, 0)], accs[(t, 1)]
        o = (acc0[0:DA_V] / acc0[DA_V:DA_V + 1]
             - lam_ref[...] * (acc1[0:DA_V] / acc1[DA_V:DA_V + 1]))
        y = o * lax.rsqrt(jnp.mean(o * o, axis=0, keepdims=True) + EPS)
        o_ref[0, :, t * tq:(t + 1) * tq] = (y * sub_ref[...] * out_scale).astype(BF16)


def _attention_call(dqT, dk, dvT, lam_l, subln_l, chunks, n_q, tq, q_block0, n_steps, lam_init):
    B, _, NT = dqT.shape
    w = n_q * tq
    kern = functools.partial(_attn_kernel, chunks=tuple(chunks), n_q=n_q, tq=tq, out_scale=1.0 - lam_init)
    return pl.pallas_call(
        kern,
        grid=(B, DA_HEADS, n_steps),
        in_specs=[
            pl.BlockSpec((1, 2 * DA_QK, w), lambda b, h, i: (b, h, q_block0 + i)),
            pl.BlockSpec((1, NT, 2 * DA_V), lambda b, h, i: (b, 0, h // 2)),
            pl.BlockSpec((1, DA_V, NT), lambda b, h, i: (b, h, 0)),
            pl.BlockSpec((1, tq), lambda b, h, i: (0, 0)),
            pl.BlockSpec((DA_V, 1), lambda b, h, i: (0, 0)),
        ],
        out_specs=pl.BlockSpec((1, DA_V, w), lambda b, h, i: (b, h, i)),
        out_shape=jax.ShapeDtypeStruct((B, DA_WIDTH, n_steps * w), BF16),
        scratch_shapes=[pltpu.VMEM((len(chunks) * KC, tq), F32) for _ in range(min(2 * n_q, ATT_BUFFERS))],
        compiler_params=_cparams(("arbitrary", "arbitrary", "arbitrary")),
        name="diff_attention",
    )(dqT, dk, dvT, lam_l, subln_l)


def _attention(dqT, dk, dvT, lam_l, subln_l, n_lat_rows, lam_init, with_ctx):
    NT = dqT.shape[2]
    n_chunks, n_lat_chunks = NT // KC, n_lat_rows // KC
    n_ctx_rows = NT - n_lat_rows
    lat = _attention_call(dqT, dk, dvT, lam_l, subln_l, range(n_chunks), ATT_TILES, TQ, 0,
                          n_lat_rows // (ATT_TILES * TQ), lam_init)
    if not with_ctx:
        return lat, lat
    ctx = _attention_call(dqT, dk, dvT, lam_l, subln_l, range(n_lat_chunks, n_chunks), 1, TQ_CTX,
                          n_lat_rows // TQ_CTX, n_ctx_rows // TQ_CTX, lam_init)
    return lat, ctx


def _fourier_kernel(ct_ref, st_ref, f_ref, o_ref):
    o_ref[...] = (jnp.dot(ct_ref[...], f_ref[0], preferred_element_type=F32)
                  + jnp.dot(st_ref[...], f_ref[1], preferred_element_type=F32)).astype(BF16)


def _fourier(ct, st_neg, f, row_block, n_rows):
    W = f.shape[2]
    tk = min(TK, n_rows)
    return pl.pallas_call(
        _fourier_kernel,
        grid=(n_rows // tk,),
        in_specs=[
            pl.BlockSpec((tk, n_rows), lambda i: (i, 0)),
            pl.BlockSpec((tk, n_rows), lambda i: (i, 0)),
            pl.BlockSpec((2, n_rows, W), lambda i: (0, row_block, 0)),
        ],
        out_specs=pl.BlockSpec((tk, W), lambda i: (i, 0)),
        out_shape=jax.ShapeDtypeStruct((n_rows, W), BF16),
        compiler_params=_cparams(("arbitrary",)),
        name="fourier_mix",
    )(ct, st_neg, f)


def _fourier_folded_kernel(ct_ref, st_ref, f_ref, o_ref, xs_ref, xa_ref, *, scale):
    tk = o_ref.shape[0]
    half = xs_ref.shape[0]

    @pl.when(pl.program_id(0) == 0)
    def _():
        n_all = f_ref.shape[1] // tk
        r = lax.broadcasted_iota(jnp.int32, (tk, tk), 0)
        c = lax.broadcasted_iota(jnp.int32, (tk, tk), 1)
        body = jnp.where(r + c == tk, 1.0, 0.0).astype(BF16)
        head = jnp.where((r == 0) & (c == 0), 1.0, 0.0).astype(BF16)
        first_row = lax.broadcasted_iota(jnp.int32, (tk, xs_ref.shape[1]), 0) == 0
        for j in range(half // tk):
            rows = slice(j * tk, (j + 1) * tk)
            lo_t = slice((n_all - 1 - j) * tk, (n_all - j) * tk)
            hi_t = slice(((n_all - j) % n_all) * tk, ((n_all - j) % n_all + 1) * tk)
            m0, m1 = [jnp.dot(body, f_ref[p, lo_t, :], preferred_element_type=F32)
                      + jnp.dot(head, f_ref[p, hi_t, :], preferred_element_type=F32) for p in range(2)]
            if j == 0:
                m0 = jnp.where(first_row, 0.0, m0)
            xs_ref[rows, :] = (f_ref[0, rows, :].astype(F32) + m0).astype(BF16)
            xa_ref[rows, :] = (f_ref[1, rows, :].astype(F32) - m1).astype(BF16)

    k = pl.program_id(0) * tk + lax.broadcasted_iota(jnp.int32, (tk, 1), 0)
    sign = (1 - 2 * (k & 1)).astype(F32)
    mid = f_ref[0, half:half + 1, :].astype(F32) * scale
    o_ref[...] = (jnp.dot(ct_ref[...], xs_ref[...], preferred_element_type=F32)
                  + jnp.dot(st_ref[...], xa_ref[...], preferred_element_type=F32)
                  + sign * mid).astype(BF16)


def _fourier_folded(ct_half, st_neg_half, f, n_rows):
    W = f.shape[2]
    half = n_rows // 2
    return pl.pallas_call(
        functools.partial(_fourier_folded_kernel, scale=1.0 / math.sqrt(n_rows * FT_GDIM)),
        grid=(n_rows // TK,),
        in_specs=[
            pl.BlockSpec((TK, half), lambda i: (i, 0)),
            pl.BlockSpec((TK, half), lambda i: (i, 0)),
            pl.BlockSpec((2, n_rows, W), lambda i: (0, 0, 0), pipeline_mode=pl.Buffered(1)),
        ],
        out_specs=pl.BlockSpec((TK, W), lambda i: (i, 0)),
        out_shape=jax.ShapeDtypeStruct((n_rows, W), BF16),
        scratch_shapes=[pltpu.VMEM((half, W), BF16), pltpu.VMEM((half, W), BF16)],
        compiler_params=_cparams(("arbitrary",)),
        name="fourier_mix_folded",
    )(ct_half, st_neg_half, f)


N_POST_IN = 13


def _post_kernel(*refs, n_lat_tiles):
    refs[N_POST_IN][0] = _post_compute(*refs[:N_POST_IN], n_lat_tiles=n_lat_tiles)


def _post_inproj_kernel(*refs, n_lat_tiles):
    n_in = N_POST_IN + 7
    x_new = _post_compute(*refs[:N_POST_IN], n_lat_tiles=n_lat_tiles)
    refs[n_in][0] = x_new
    _inproj_compute(x_new, *refs[N_POST_IN:n_in], *refs[n_in + 1:])


def _post_compute(x_ref, mod_ref, g_ref, of_ref, gate_ref, dal_ref, dac_ref, ftl_ref, ftc_ref, onorm_ref,
                  wo_ref, wfi_ref, wfo_ref, *, n_lat_tiles):
    x = x_ref[0]
    mod = mod_ref[0]
    is_lat = pl.program_id(1) < n_lat_tiles
    ft = jnp.where(is_lat, ftl_ref[...], ftc_ref[...])
    da_t = jnp.where(is_lat, dal_ref[0], dac_ref[0])
    o = of_ref[0, 0].astype(F32) + of_ref[1, 0].astype(F32)
    onorm = onorm_ref[...]
    heads = [_rms(o[:, h * HG_DIM:(h + 1) * HG_DIM]) * onorm for h in range(HG_HEADS)]
    hg = (jnp.concatenate(heads, axis=-1) * gate_ref[0].astype(F32)).astype(BF16)
    y = (jnp.dot(hg, wo_ref[0, 0:HG_WIDTH, :], preferred_element_type=F32)
         + lax.dot_general(da_t, wo_ref[0, HG_WIDTH:HG_WIDTH + DA_WIDTH, :], (((0,), (0,)), ((), ())),
                           preferred_element_type=F32)
         + jnp.dot(ft, wo_ref[0, HG_WIDTH + DA_WIDTH:, :], preferred_element_type=F32))
    x1 = x + _rms(y) * (mod[2:3, :] * g_ref[1:2, :])
    h2 = (_rms(x1) * (g_ref[2:3, :] * (1.0 + mod[4:5, :])) + mod[3:4, :]).astype(BF16)
    gu = jnp.dot(h2, wfi_ref[0], preferred_element_type=F32)
    act = (_silu(gu[:, :FF_HIDDEN]) * gu[:, FF_HIDDEN:]).astype(BF16)
    y2 = jnp.dot(act, wfo_ref[0], preferred_element_type=F32)
    return x1 + _rms(y2) * (mod[5:6, :] * g_ref[3:4, :])


def _post_in_specs(B, g_l, onorm_l, wo, wfi, wfo, layer, n_lat_tiles):
    const = lambda a: pl.BlockSpec(a.shape, lambda b, i: (0,) * a.ndim, pipeline_mode=pl.Buffered(1))
    weight = lambda a: pl.BlockSpec((1,) + a.shape[1:], lambda b, i: (layer, 0, 0), pipeline_mode=pl.Buffered(1))
    specs = [
        _tok(D_MODEL),
        _mod_spec(B, n_lat_tiles),
        const(g_l),
        pl.BlockSpec((2, 1, TM, HG_WIDTH), lambda b, i: (0, b, i, 0)),
        _tok(HG_WIDTH),
        pl.BlockSpec((1, DA_WIDTH, TM), lambda b, i: (b, 0, jnp.minimum(i, n_lat_tiles - 1))),
        pl.BlockSpec((1, DA_WIDTH, TM), lambda b, i: (b, 0, jnp.maximum(i - n_lat_tiles, 0))),
        pl.BlockSpec((TM, FT_WIDTH), lambda b, i: (jnp.minimum(i, n_lat_tiles - 1), b)),
        pl.BlockSpec((TM, FT_WIDTH), lambda b, i: (jnp.maximum(i - n_lat_tiles, 0), b)),
        const(onorm_l), weight(wo), weight(wfi), weight(wfo),
    ]
    assert len(specs) == N_POST_IN
    return specs


def _post(X, mod_l, g_l, of, gate, da_lat, da_ctx, ft_lat, ft_ctx, onorm_l, wo, wfi, wfo, layer, n_lat_tiles, n_tiles):
    B, NT, D = X.shape
    return pl.pallas_call(
        functools.partial(_post_kernel, n_lat_tiles=n_lat_tiles),
        grid=(B, n_tiles),
        in_specs=_post_in_specs(B, g_l, onorm_l, wo, wfi, wfo, layer, n_lat_tiles),
        out_specs=_tok(D),
        out_shape=jax.ShapeDtypeStruct((B, n_tiles * TM, D), F32),
        compiler_params=_cparams(("arbitrary", "arbitrary")),
        name="outproj_ffn",
    )(X, mod_l, g_l, of, gate, da_lat, da_ctx, ft_lat, ft_ctx, onorm_l, wo, wfi, wfo)


def _post_inproj(X, post_args, inproj_args, layer, n_lat_tiles):
    B, NT, D = X.shape
    mod_l, g_l, of, gate, da_lat, da_ctx, ft_lat, ft_ctx, onorm_l, wo, wfi, wfo = post_args
    mod_n, g_n, w_in, cos_t, sin_t, lb_n, cs_tab = inproj_args
    in_specs, out_specs, out_shapes = _inproj_io(B, NT, g_n, w_in, layer + 1, lb_n, cs_tab, n_lat_tiles, True)
    outs = pl.pallas_call(
        functools.partial(_post_inproj_kernel, n_lat_tiles=n_lat_tiles),
        grid=(B, NT // TM),
        in_specs=_post_in_specs(B, g_l, onorm_l, wo, wfi, wfo, layer, n_lat_tiles) + in_specs,
        out_specs=(_tok(D),) + tuple(out_specs),
        out_shape=(jax.ShapeDtypeStruct((B, NT, D), F32),) + tuple(out_shapes),
        compiler_params=_cparams(("arbitrary", "arbitrary")),
        name="outproj_ffn_inproj",
    )(X, *post_args, *inproj_args)
    return outs[0], outs[1:]


def _rope_tables(t_lat, t_ctx):
    pos = jnp.arange(t_lat)
    inv_freq = 1.0 / (ROPE_THETA ** (jnp.arange(0, ROPE_AXIS_DIM, 2, dtype=F32) / ROPE_AXIS_DIM))
    ang = jnp.stack([pos // GRID_W, pos % GRID_W], axis=-1).astype(F32)[:, :, None] * inv_freq
    cos, sin = jnp.cos(ang), jnp.sin(ang)
    cos32 = jnp.stack([cos, cos], axis=2).reshape(t_lat, DA_QK)
    sin32 = jnp.stack([-sin, sin], axis=2).reshape(t_lat, DA_QK)
    reps = DA_WIDTH // DA_QK
    cos_t = jnp.concatenate([jnp.tile(cos32, (1, reps)), jnp.ones((t_ctx, DA_WIDTH), F32)], axis=0)
    sin_t = jnp.concatenate([jnp.tile(sin32, (1, reps)), jnp.zeros((t_ctx, DA_WIDTH), F32)], axis=0)
    return cos_t, sin_t


def _channel_table():
    idx = np.arange(FT_GDIM)
    ang = 2.0 * np.pi * ((idx[:, None] * idx[None, :]) % FT_GDIM) / FT_GDIM
    eye = np.eye(FT_GROUPS)
    return jnp.asarray(np.concatenate([np.kron(eye, np.cos(ang)), np.kron(eye, np.sin(ang))], axis=1), BF16)


def _position_tables(n, cols):
    lo = 1
    while lo * lo < n:
        lo *= 2
    hi = n // lo
    k = jnp.arange(n)
    a1 = 2.0 * np.pi * ((k[:, None] * jnp.arange(cols // lo)[None, :]) % hi).astype(F32) / hi
    a2 = 2.0 * np.pi * ((k[:, None] * jnp.arange(lo)[None, :]) % n).astype(F32) / n
    c1, s1 = jnp.cos(a1)[:, :, None], jnp.sin(a1)[:, :, None]
    c2, s2 = jnp.cos(a2)[:, None, :], jnp.sin(a2)[:, None, :]
    scale = 1.0 / math.sqrt(n * FT_GDIM)
    ct = ((c1 * c2 - s1 * s2) * scale).reshape(n, cols).astype(BF16)
    st_neg = ((s1 * c2 + c1 * s2) * (-scale)).reshape(n, cols).astype(BF16)
    return ct, st_neg


def kernel(x, c, ctx, c_ctx, w_mod, b_mod, norm_g, w_in, w_out, hg_lb_logits, hg_onorm,
           da_lambda, da_subln, w_ffn_in, w_ffn_out):
    B, T, D = x.shape
    Tc = ctx.shape[1]
    depth = w_mod.shape[0]
    assert D == D_MODEL and B + 1 <= MOD_ROWS
    assert T % TM == 0 and Tc % TM == 0 and T % KC == 0 and Tc % KC == 0 and T % CH == 0 and Tc % CH == 0
    n_lat_tiles = T // TM
    NT = T + Tc

    lam_init = [0.8 - 0.6 * math.exp(-0.3 * l) for l in range(depth)]
    lb, lam = _prep(hg_lb_logits, da_lambda, lam_init)

    cond = jnp.concatenate([c.astype(F32), c_ctx.astype(F32)[None, :],
                            jnp.zeros((MOD_ROWS - B - 1, D), F32)], axis=0)
    mods = _modulation(cond, w_mod, b_mod).reshape(depth, MOD_ROWS, N_MOD, D)

    cos_t, sin_t = _rope_tables(T, Tc)
    cs_tab = _channel_table()
    ct_lat, st_lat = _position_tables(T, T // 2)
    ct_ctx, st_ctx = _position_tables(Tc, Tc)

    w_in_b, w_out_b = w_in.astype(BF16), w_out.astype(BF16)
    w_ffn_in_b, w_ffn_out_b = w_ffn_in.astype(BF16), w_ffn_out.astype(BF16)
    X = jnp.concatenate([x, ctx], axis=1).astype(F32)
    g_all = norm_g.astype(F32)
    projected = _inproj(X, mods[0], g_all[0], w_in_b, 0, cos_t, sin_t, lb[0], cs_tab, n_lat_tiles)
    for l in range(depth):
        q, vi, gate, lf, dq, dk, dv, f = projected
        of = _hgrn(q, vi, lf, T)
        with_ctx = l < depth - 1
        da_lat, da_ctx = _attention(dq, dk, dv, lam[l:l + 1], da_subln[l].astype(F32).reshape(DA_V, 1), T,
                                    lam_init[l], with_ctx)
        ft_lat = _fourier_folded(ct_lat, st_lat, f, T)
        ft_ctx = _fourier(ct_ctx, st_ctx, f, T // Tc, Tc) if with_ctx else ft_lat
        post_args = (mods[l], g_all[l], of, gate, da_lat, da_ctx, ft_lat, ft_ctx,
                     hg_onorm[l].astype(F32).reshape(1, HG_DIM), w_out_b, w_ffn_in_b, w_ffn_out_b)
        if with_ctx:
            next_args = (mods[l + 1], g_all[l + 1], w_in_b, cos_t, sin_t, lb[l + 1], cs_tab)
            X, projected = _post_inproj(X, post_args, next_args, l, n_lat_tiles)
        else:
            X = _post(X, *post_args, l, n_lat_tiles, n_lat_tiles)
    return X.astype(x.dtype)
```

```python
import functools
import math

import numpy as np
import jax
import jax.numpy as jnp
from jax import lax
from jax.experimental import pallas as pl
from jax.experimental.pallas import tpu as pltpu

F32 = jnp.float32
BF16 = jnp.bfloat16

D_MODEL = 1024
GRID_W = 64
HG_WIDTH = 512
HG_DIM = 128
HG_HEADS = 4
DA_WIDTH = 256
DA_HEADS = 4
DA_V = 64
DA_QK = 32
FT_WIDTH = 256
FT_GROUPS = 4
FT_GDIM = FT_WIDTH // FT_GROUPS
FF_HIDDEN = 2816
N_MOD = 6
EPS = 1e-6
ROPE_THETA = 10000.0
ROPE_AXIS_DIM = DA_QK // 2

_OFF_Q, _OFF_I, _OFF_G, _OFF_F, _OFF_DQ, _OFF_DK, _OFF_DV, _OFF_FT, IN_WIDTH = (
    0, 512, 1024, 1536, 2560, 2816, 3072, 3328, 3584)

TM = 256
TQ = 256
TQ_CTX = 256
ATT_TILES = 8
ATT_BUFFERS = 4
KC = 256
CH = 128
SUB = 8
HGRN_UNROLL = 34
TK = 256
VT_ROWS = 80
MOD_ROWS = 8
MOD_TN = 1536
VMEM_LIMIT = 52 * 1024 * 1024

Q_SCALE = (DA_QK ** -0.5) * math.log2(math.e)


def _silu(x):
    return x * jax.nn.sigmoid(x)


def _rms(x):
    return x * lax.rsqrt(jnp.mean(x * x, axis=-1, keepdims=True) + EPS)


def _cparams(sem):
    return pltpu.CompilerParams(dimension_semantics=sem, vmem_limit_bytes=VMEM_LIMIT)


def _mod_kernel(c_ref, w_ref, b_ref, o_ref):
    a = _silu(c_ref[...]).astype(BF16)
    w = w_ref[0].astype(BF16)
    o_ref[0] = jnp.dot(a, w, preferred_element_type=F32) + b_ref[0]


def _modulation(cond, w_mod, b_mod):
    depth, d, n = w_mod.shape
    return pl.pallas_call(
        _mod_kernel,
        grid=(depth, n // MOD_TN),
        in_specs=[
            pl.BlockSpec((MOD_ROWS, d), lambda l, j: (0, 0)),
            pl.BlockSpec((1, d, MOD_TN), lambda l, j: (l, 0, j)),
            pl.BlockSpec((1, 1, MOD_TN), lambda l, j: (l, 0, j)),
        ],
        out_specs=pl.BlockSpec((1, MOD_ROWS, MOD_TN), lambda l, j: (l, 0, j)),
        out_shape=jax.ShapeDtypeStruct((depth, MOD_ROWS, n), F32),
        compiler_params=_cparams(("arbitrary", "arbitrary")),
        name="modulation",
    )(cond, w_mod, b_mod.reshape(depth, 1, n))


def _prep_kernel(lb_ref, lam_ref, lam_init_ref, lbo_ref, lamo_ref, *, depth):
    rows = [lb_ref[l:l + 1, :] for l in range(depth)]
    m = rows[0]
    for r in rows[1:]:
        m = jnp.maximum(m, r)
    e = [jnp.exp(r - m) for r in rows]
    tot = e[0]
    for r in e[1:]:
        tot = tot + r
    p = [r / tot for r in e]
    acc = p[0]
    lbo_ref[0:1, :] = acc - p[0]
    for l in range(1, depth):
        acc = acc + p[l]
        lbo_ref[l:l + 1, :] = acc - p[0]
    x = lam_ref[...]
    a = jnp.sum(x[:, 0:DA_QK] * x[:, DA_QK:2 * DA_QK], axis=-1, keepdims=True)
    b = jnp.sum(x[:, 2 * DA_QK:3 * DA_QK] * x[:, 3 * DA_QK:4 * DA_QK], axis=-1, keepdims=True)
    lamo_ref[...] = jnp.exp(a) - jnp.exp(b) + lam_init_ref[...]


def _prep(hg_lb_logits, da_lambda, lam_init):
    depth = hg_lb_logits.shape[1]
    lb_in = jnp.transpose(hg_lb_logits.astype(F32), (1, 0, 2)).reshape(depth, 2 * HG_WIDTH)
    lam_in = da_lambda.astype(F32).reshape(depth, 4 * DA_QK)
    lam_init_arr = jnp.asarray(np.broadcast_to(np.asarray(lam_init, np.float32)[:, None], (depth, TQ)))
    lb, lam = pl.pallas_call(
        functools.partial(_prep_kernel, depth=depth),
        out_shape=(jax.ShapeDtypeStruct((depth, 2 * HG_WIDTH), F32),
                   jax.ShapeDtypeStruct((depth, TQ), F32)),
        name="param_prep",
    )(lb_in, lam_in, lam_init_arr)
    return lb.reshape(depth, 2, HG_WIDTH), lam


def _inproj_kernel(x_ref, *refs):
    _inproj_compute(x_ref[0], *refs)


def _inproj_compute(x, mod_ref, g_ref, w_ref, cos_ref, sin_ref, lb_ref, cs_ref,
                    q_ref, i_ref, gate_ref, lf_ref, dq_ref, dk_ref, dv_ref, f_ref):
    mod = mod_ref[0]
    h = _rms(x) * (g_ref[0:1, :] * (1.0 + mod[1:2, :])) + mod[0:1, :]
    hb = h.astype(BF16)

    def proj(a, b):
        return jnp.dot(hb, w_ref[0, :, a:b], preferred_element_type=F32)

    q_ref[0] = _silu(proj(_OFF_Q, _OFF_I)).astype(BF16)
    i_ref[0] = proj(_OFF_I, _OFF_G).astype(BF16)
    gate_ref[0] = _silu(proj(_OFF_G, _OFF_F)).astype(BF16)
    for d in range(2):
        z = proj(_OFF_F + d * HG_WIDTH, _OFF_F + (d + 1) * HG_WIDTH)
        lb = lb_ref[d:d + 1, :]
        lf_ref[d, 0] = jnp.log(lb + (1.0 - lb) * jax.nn.sigmoid(z))

    cos = cos_ref[...]
    sin = sin_ref[...]
    lane = lax.broadcasted_iota(jnp.int32, cos.shape, 1)
    upper_half = (lane & (ROPE_AXIS_DIM // 2)) != 0

    def rope(t):
        partner = jnp.where(upper_half,
                            pltpu.roll(t, ROPE_AXIS_DIM // 2, 1),
                            pltpu.roll(t, 2 * DA_HEADS * DA_QK - ROPE_AXIS_DIM // 2, 1))
        return t * cos + partner * sin

    dq_ref[0] = (rope(proj(_OFF_DQ, _OFF_DK)) * Q_SCALE).T.astype(BF16)
    dk_ref[0] = rope(proj(_OFF_DK, _OFF_DV)).astype(BF16)
    dv_ref[0] = proj(_OFF_DV, _OFF_FT).T.astype(BF16)
    ft = proj(_OFF_FT, IN_WIDTH).astype(BF16)
    cs = jnp.dot(ft, cs_ref[...], preferred_element_type=F32)
    f_ref[0] = cs[:, :FT_WIDTH].astype(BF16)
    f_ref[1] = cs[:, FT_WIDTH:].astype(BF16)


def _tok(w):
    return pl.BlockSpec((1, TM, w), lambda b, i: (b, i, 0))


def _mod_spec(B, n_lat_tiles):
    return pl.BlockSpec((1, N_MOD, D_MODEL), lambda b, i: (jnp.where(i < n_lat_tiles, b, B), 0, 0))


def _inproj_io(B, NT, g_l, w_in, layer, lb_l, cs_tab, n_lat_tiles, single):
    tok = _tok
    mode = dict(pipeline_mode=pl.Buffered(1)) if single else {}
    full = lambda a: pl.BlockSpec(a.shape, lambda b, i: (0,) * a.ndim, **mode)
    tok_t = pl.BlockSpec((1, DA_WIDTH, TM), lambda b, i: (b, 0, i))
    in_specs = [
        _mod_spec(B, n_lat_tiles),
        full(g_l),
        pl.BlockSpec((1,) + w_in.shape[1:], lambda b, i: (layer, 0, 0), **mode),
        pl.BlockSpec((TM, DA_WIDTH), lambda b, i: (i, 0)),
        pl.BlockSpec((TM, DA_WIDTH), lambda b, i: (i, 0)),
        full(lb_l), full(cs_tab),
    ]
    out_shapes = (
        jax.ShapeDtypeStruct((B, NT, HG_WIDTH), BF16),
        jax.ShapeDtypeStruct((B, NT, HG_WIDTH), BF16),
        jax.ShapeDtypeStruct((B, NT, HG_WIDTH), BF16),
        jax.ShapeDtypeStruct((2, B, NT, HG_WIDTH), F32),
        jax.ShapeDtypeStruct((B, DA_WIDTH, NT), BF16),
        jax.ShapeDtypeStruct((B, NT, DA_WIDTH), BF16),
        jax.ShapeDtypeStruct((B, DA_WIDTH, NT), BF16),
        jax.ShapeDtypeStruct((2, NT, B * FT_WIDTH), BF16),
    )
    out_specs = (
        tok(HG_WIDTH), tok(HG_WIDTH), tok(HG_WIDTH),
        pl.BlockSpec((2, 1, TM, HG_WIDTH), lambda b, i: (0, b, i, 0)),
        tok_t, tok(DA_WIDTH), tok_t,
        pl.BlockSpec((2, TM, FT_WIDTH), lambda b, i: (0, i, b)),
    )
    return in_specs, out_specs, out_shapes


def _inproj(X, mod_l, g_l, w_in, layer, cos_t, sin_t, lb_l, cs_tab, n_lat_tiles):
    B, NT, D = X.shape
    in_specs, out_specs, out_shapes = _inproj_io(B, NT, g_l, w_in, layer, lb_l, cs_tab, n_lat_tiles, False)
    return pl.pallas_call(
        _inproj_kernel,
        grid=(B, NT // TM),
        in_specs=[_tok(D)] + in_specs,
        out_specs=out_specs,
        out_shape=out_shapes,
        compiler_params=_cparams(("arbitrary", "arbitrary")),
        name="adaln_inproj",
    )(X, mod_l, g_l, w_in, cos_t, sin_t, lb_l, cs_tab)


def _hgrn_direction(q, f, v, g, st, cmat, codes, rev):
    C, K = q.shape
    k = 1.0 - f
    g_hi = g.astype(BF16)
    g_lo = (g - g_hi.astype(F32)).astype(BF16)
    cs = jnp.dot(cmat, jnp.concatenate([g_hi, g_lo], axis=1), preferred_element_type=F32)
    bc = cs[:, :K] + cs[:, K:]
    tot = bc[0:1, :] if rev else bc[C - 1:C, :]

    nt_dims = (((1,), (1,)), ((), ()))
    tn_dims = (((0,), (0,)), ((), ()))
    qd = (q * jnp.exp(bc)).astype(BF16)
    o = lax.dot_general(qd, st.astype(BF16), nt_dims, preferred_element_type=F32)
    kdec = (k * jnp.exp(tot - bc)).astype(BF16)
    st_new = st * jnp.exp(tot) + lax.dot_general(v, kdec, tn_dims, preferred_element_type=F32)

    a = jnp.zeros((C, C), F32)
    f3 = f.reshape(C // SUB, SUB, K)
    qd_d = q
    for d8 in range(SUB):
        f_shift = f if d8 == 0 else pltpu.roll(f3, (SUB - d8) if rev else d8, 1).reshape(C, K)
        qd_next = qd_d * f_shift
        red = jnp.sum(qd_d - qd_next, axis=-1, keepdims=True)
        a = jnp.where(codes == d8, red, a)
        qd_d = qd_next

    zeros8 = jnp.zeros((SUB, K), F32)
    b = SUB
    level = 0
    while b < C:
        ql, kl = [], []
        for r in range(0, C, SUB):
            base = (r // (2 * b)) * 2 * b
            upper = (r - base) >= b
            ref_row = base + b if rev else base + b - 1
            q_side = (not upper) if rev else upper
            bref = bc[ref_row:ref_row + 1, :]
            bg = bc[r:r + SUB, :]
            if q_side:
                ql.append(q[r:r + SUB, :] * jnp.exp(bg - bref))
                kl.append(zeros8)
            else:
                kl.append(k[r:r + SUB, :] * jnp.exp(bref - bg))
                ql.append(zeros8)
        qlb = jnp.concatenate(ql, axis=0).astype(BF16)
        klb = jnp.concatenate(kl, axis=0).astype(BF16)
        p = lax.dot_general(qlb, klb, nt_dims, preferred_element_type=F32)
        a = jnp.where(codes == SUB + level, p, a)
        b *= 2
        level += 1

    o = o + jnp.dot(a.astype(BF16), v, preferred_element_type=F32)
    return o, st_new


def _hgrn_tables(C):
    t = np.arange(C)[:, None]
    s = np.arange(C)[None, :]
    cmats, codes = [], []
    for rev in (False, True):
        d = (s - t) if rev else (t - s)
        cmats.append((d >= 0).astype(np.float32))
        code = np.full((C, C), -1, np.int32)
        inside = (d >= 0) & ((t // SUB) == (s // SUB))
        code[inside] = d[inside]
        b, level = SUB, 0
        while b < C:
            split = (d > 0) & ((t // (2 * b)) == (s // (2 * b))) & ((t // b) != (s // b))
            code[split] = SUB + level
            b *= 2
            level += 1
        codes.append(code)
    return jnp.asarray(np.stack(cmats), BF16), jnp.asarray(np.stack(codes), jnp.int32)


def _hgrn_kernel(q_ref, v_ref, lf_ref, cmat_ref, code_ref, o_ref, st_ref, *, n_lat, n_ctx):
    C = CH
    n_chunks = n_lat + n_ctx
    st_ref[...] = jnp.zeros(st_ref.shape, F32)

    def body(j, carry):
        for d, rev in enumerate((False, True)):
            c = (n_chunks - 1 - j) if rev else jnp.where(j < n_ctx, n_lat + j, j - n_ctx)
            r0 = pl.multiple_of(c * C, C)
            q = q_ref[0, pl.ds(r0, C), :].astype(F32)
            v = v_ref[0, pl.ds(r0, C), :]
            g = lf_ref[d, 0, pl.ds(r0, C), :]
            o, st_new = _hgrn_direction(q, jnp.exp(g), v, g, st_ref[d], cmat_ref[d], code_ref[d], rev)
            st_ref[d] = st_new
            o_ref[d, 0, pl.ds(r0, C), :] = o.astype(o_ref.dtype)
        return carry

    lax.fori_loop(0, n_chunks, body, 0, unroll=HGRN_UNROLL)


def _hgrn(q, v, lf, n_lat_rows):
    B, NT, _ = q.shape
    n_lat = n_lat_rows // CH
    n_ctx = (NT - n_lat_rows) // CH
    assert (n_lat + n_ctx) % HGRN_UNROLL == 0
    cmat, codes = _hgrn_tables(CH)
    head = pl.BlockSpec((1, NT, HG_DIM), lambda b, h: (b, 0, h))
    both = pl.BlockSpec((2, 1, NT, HG_DIM), lambda b, h: (0, b, 0, h))
    table = pl.BlockSpec((2, CH, CH), lambda b, h: (0, 0, 0))
    return pl.pallas_call(
        functools.partial(_hgrn_kernel, n_lat=n_lat, n_ctx=n_ctx),
        grid=(B, HG_HEADS),
        in_specs=[head, head, both, table, table],
        out_specs=both,
        out_shape=jax.ShapeDtypeStruct((2, B, NT, HG_WIDTH), BF16),
        scratch_shapes=[pltpu.VMEM((2, HG_DIM, HG_DIM), F32)],
        compiler_params=_cparams(("arbitrary", "arbitrary")),
        name="hgrn_scan",
    )(q, v, lf, cmat, codes)


def _attn_kernel(qT_ref, k_ref, vT_ref, lam_ref, sub_ref, o_ref, *s_refs, chunks, n_q, tq, out_scale):
    n = len(chunks)
    odd_head = (pl.program_id(1) % 2) == 1
    ones_row = lax.broadcasted_iota(jnp.int32, (VT_ROWS - DA_V, n * KC), 0) == 0
    vt = jnp.concatenate(
        [vT_ref[0, :, chunks[0] * KC:(chunks[-1] + 1) * KC],
         jnp.where(ones_row, 1.0, 0.0).astype(BF16)], axis=0)
    streams = [(t, mp) for t in range(n_q) for mp in range(2)]
    zeros = jnp.zeros((DA_QK, tq), BF16)

    def query_operand(t, mp):
        q = qT_ref[0, mp * DA_QK:(mp + 1) * DA_QK, t * tq:(t + 1) * tq]
        even = jnp.concatenate([q if j == mp else zeros for j in range(4)], axis=0)
        odd = jnp.concatenate([q if j == 2 + mp else zeros for j in range(4)], axis=0)
        return jnp.where(odd_head, odd, even)

    row0 = pl.multiple_of(jnp.minimum(pl.program_id(2), 0) * KC, KC)

    def pass1(slot):
        rhs = query_operand(*streams[slot])
        m8 = None
        for i, c in enumerate(chunks):
            s = jnp.dot(k_ref[0, c * KC:(c + 1) * KC, :], rhs, preferred_element_type=F32)
            s_refs[slot % len(s_refs)][i * KC:(i + 1) * KC, :] = s
            parts = [s[r:r + 8, :] for r in range(0, KC, 8)]
            while len(parts) > 1:
                parts = [jnp.maximum(parts[j], parts[j + 1]) for j in range(0, len(parts), 2)]
            m8 = parts[0] if m8 is None else jnp.maximum(m8, parts[0])
        return jnp.max(m8, axis=0, keepdims=True)

    def pass2(slot, m):
        p = jnp.exp2((s_refs[slot % len(s_refs)][pl.ds(row0, n * KC), :] - m).astype(BF16))
        return jnp.dot(vt, p, preferred_element_type=F32)

    accs, maxes = {}, {}
    for slot in range(len(streams) + 1):
        if slot < len(streams):
            maxes[slot] = pass1(slot)
        if slot >= 1:
            accs[streams[slot - 1]] = pass2(slot - 1, maxes[slot - 1])

    for t in range(n_q):
        acc0, acc1 = accs[(t, 0)], accs[(t, 1)]
        o = (acc0[0:DA_V] / acc0[DA_V:DA_V + 1]
             - lam_ref[...] * (acc1[0:DA_V] / acc1[DA_V:DA_V + 1]))
        y = o * lax.rsqrt(jnp.mean(o * o, axis=0, keepdims=True) + EPS)
        o_ref[0, :, t * tq:(t + 1) * tq] = (y * sub_ref[...] * out_scale).astype(BF16)


def _attention_call(dqT, dk, dvT, lam_l, subln_l, chunks, n_q, tq, q_block0, n_steps, lam_init):
    B, _, NT = dqT.shape
    w = n_q * tq
    kern = functools.partial(_attn_kernel, chunks=tuple(chunks), n_q=n_q, tq=tq, out_scale=1.0 - lam_init)
    return pl.pallas_call(
        kern,
        grid=(B, DA_HEADS, n_steps),
        in_specs=[
            pl.BlockSpec((1, 2 * DA_QK, w), lambda b, h, i: (b, h, q_block0 + i)),
            pl.BlockSpec((1, NT, 2 * DA_V), lambda b, h, i: (b, 0, h // 2)),
            pl.BlockSpec((1, DA_V, NT), lambda b, h, i: (b, h, 0)),
            pl.BlockSpec((1, tq), lambda b, h, i: (0, 0)),
            pl.BlockSpec((DA_V, 1), lambda b, h, i: (0, 0)),
        ],
        out_specs=pl.BlockSpec((1, DA_V, w), lambda b, h, i: (b, h, i)),
        out_shape=jax.ShapeDtypeStruct((B, DA_WIDTH, n_steps * w), BF16),
        scratch_shapes=[pltpu.VMEM((len(chunks) * KC, tq), F32) for _ in range(min(2 * n_q, ATT_BUFFERS))],
        compiler_params=_cparams(("arbitrary", "arbitrary", "arbitrary")),
        name="diff_attention",
    )(dqT, dk, dvT, lam_l, subln_l)


def _attention(dqT, dk, dvT, lam_l, subln_l, n_lat_rows, lam_init, with_ctx):
    NT = dqT.shape[2]
    n_chunks, n_lat_chunks = NT // KC, n_lat_rows // KC
    n_ctx_rows = NT - n_lat_rows
    lat = _attention_call(dqT, dk, dvT, lam_l, subln_l, range(n_chunks), ATT_TILES, TQ, 0,
                          n_lat_rows // (ATT_TILES * TQ), lam_init)
    if not with_ctx:
        return lat, lat
    ctx = _attention_call(dqT, dk, dvT, lam_l, subln_l, range(n_lat_chunks, n_chunks), 1, TQ_CTX,
                          n_lat_rows // TQ_CTX, n_ctx_rows // TQ_CTX, lam_init)
    return lat, ctx


def _fourier_kernel(ct_ref, st_ref, f_ref, o_ref):
    o_ref[...] = (jnp.dot(ct_ref[...], f_ref[0], preferred_element_type=F32)
                  + jnp.dot(st_ref[...], f_ref[1], preferred_element_type=F32)).astype(BF16)


def _fourier(ct, st_neg, f, row_block, n_rows):
    W = f.shape[2]
    tk = min(TK, n_rows)
    return pl.pallas_call(
        _fourier_kernel,
        grid=(n_rows // tk,),
        in_specs=[
            pl.BlockSpec((tk, n_rows), lambda i: (i, 0)),
            pl.BlockSpec((tk, n_rows), lambda i: (i, 0)),
            pl.BlockSpec((2, n_rows, W), lambda i: (0, row_block, 0)),
        ],
        out_specs=pl.BlockSpec((tk, W), lambda i: (i, 0)),
        out_shape=jax.ShapeDtypeStruct((n_rows, W), BF16),
        compiler_params=_cparams(("arbitrary",)),
        name="fourier_mix",
    )(ct, st_neg, f)


def _fourier_folded_kernel(ct_ref, st_ref, f_ref, o_ref, xs_ref, xa_ref, *, scale):
    tk = o_ref.shape[0]
    half = xs_ref.shape[0]

    @pl.when(pl.program_id(0) == 0)
    def _():
        n_all = f_ref.shape[1] // tk
        r = lax.broadcasted_iota(jnp.int32, (tk, tk), 0)
        c = lax.broadcasted_iota(jnp.int32, (tk, tk), 1)
        body = jnp.where(r + c == tk, 1.0, 0.0).astype(BF16)
        head = jnp.where((r == 0) & (c == 0), 1.0, 0.0).astype(BF16)
        first_row = lax.broadcasted_iota(jnp.int32, (tk, xs_ref.shape[1]), 0) == 0
        for j in range(half // tk):
            rows = slice(j * tk, (j + 1) * tk)
            lo_t = slice((n_all - 1 - j) * tk, (n_all - j) * tk)
            hi_t = slice(((n_all - j) % n_all) * tk, ((n_all - j) % n_all + 1) * tk)
            m0, m1 = [jnp.dot(body, f_ref[p, lo_t, :], preferred_element_type=F32)
                      + jnp.dot(head, f_ref[p, hi_t, :], preferred_element_type=F32) for p in range(2)]
            if j == 0:
                m0 = jnp.where(first_row, 0.0, m0)
            xs_ref[rows, :] = (f_ref[0, rows, :].astype(F32) + m0).astype(BF16)
            xa_ref[rows, :] = (f_ref[1, rows, :].astype(F32) - m1).astype(BF16)

    k = pl.program_id(0) * tk + lax.broadcasted_iota(jnp.int32, (tk, 1), 0)
    sign = (1 - 2 * (k & 1)).astype(F32)
    mid = f_ref[0, half:half + 1, :].astype(F32) * scale
    o_ref[...] = (jnp.dot(ct_ref[...], xs_ref[...], preferred_element_type=F32)
                  + jnp.dot(st_ref[...], xa_ref[...], preferred_element_type=F32)
                  + sign * mid).astype(BF16)


def _fourier_folded(ct_half, st_neg_half, f, n_rows):
    W = f.shape[2]
    half = n_rows // 2
    return pl.pallas_call(
        functools.partial(_fourier_folded_kernel, scale=1.0 / math.sqrt(n_rows * FT_GDIM)),
        grid=(n_rows // TK,),
        in_specs=[
            pl.BlockSpec((TK, half), lambda i: (i, 0)),
            pl.BlockSpec((TK, half), lambda i: (i, 0)),
            pl.BlockSpec((2, n_rows, W), lambda i: (0, 0, 0), pipeline_mode=pl.Buffered(1)),
        ],
        out_specs=pl.BlockSpec((TK, W), lambda i: (i, 0)),
        out_shape=jax.ShapeDtypeStruct((n_rows, W), BF16),
        scratch_shapes=[pltpu.VMEM((half, W), BF16), pltpu.VMEM((half, W), BF16)],
        compiler_params=_cparams(("arbitrary",)),
        name="fourier_mix_folded",
    )(ct_half, st_neg_half, f)


N_POST_IN = 13


def _post_kernel(*refs, n_lat_tiles):
    refs[N_POST_IN][0] = _post_compute(*refs[:N_POST_IN], n_lat_tiles=n_lat_tiles)


def _post_inproj_kernel(*refs, n_lat_tiles):
    n_in = N_POST_IN + 7
    x_new = _post_compute(*refs[:N_POST_IN], n_lat_tiles=n_lat_tiles)
    refs[n_in][0] = x_new
    _inproj_compute(x_new, *refs[N_POST_IN:n_in], *refs[n_in + 1:])


def _post_compute(x_ref, mod_ref, g_ref, of_ref, gate_ref, dal_ref, dac_ref, ftl_ref, ftc_ref, onorm_ref,
                  wo_ref, wfi_ref, wfo_ref, *, n_lat_tiles):
    x = x_ref[0]
    mod = mod_ref[0]
    is_lat = pl.program_id(1) < n_lat_tiles
    ft = jnp.where(is_lat, ftl_ref[...], ftc_ref[...])
    da_t = jnp.where(is_lat, dal_ref[0], dac_ref[0])
    o = of_ref[0, 0].astype(F32) + of_ref[1, 0].astype(F32)
    onorm = onorm_ref[...]
    heads = [_rms(o[:, h * HG_DIM:(h + 1) * HG_DIM]) * onorm for h in range(HG_HEADS)]
    hg = (jnp.concatenate(heads, axis=-1) * gate_ref[0].astype(F32)).astype(BF16)
    y = (jnp.dot(hg, wo_ref[0, 0:HG_WIDTH, :], preferred_element_type=F32)
         + lax.dot_general(da_t, wo_ref[0, HG_WIDTH:HG_WIDTH + DA_WIDTH, :], (((0,), (0,)), ((), ())),
                           preferred_element_type=F32)
         + jnp.dot(ft, wo_ref[0, HG_WIDTH + DA_WIDTH:, :], preferred_element_type=F32))
    x1 = x + _rms(y) * (mod[2:3, :] * g_ref[1:2, :])
    h2 = (_rms(x1) * (g_ref[2:3, :] * (1.0 + mod[4:5, :])) + mod[3:4, :]).astype(BF16)
    gu = jnp.dot(h2, wfi_ref[0], preferred_element_type=F32)
    act = (_silu(gu[:, :FF_HIDDEN]) * gu[:, FF_HIDDEN:]).astype(BF16)
    y2 = jnp.dot(act, wfo_ref[0], preferred_element_type=F32)
    return x1 + _rms(y2) * (mod[5:6, :] * g_ref[3:4, :])


def _post_in_specs(B, g_l, onorm_l, wo, wfi, wfo, layer, n_lat_tiles):
    const = lambda a: pl.BlockSpec(a.shape, lambda b, i: (0,) * a.ndim, pipeline_mode=pl.Buffered(1))
    weight = lambda a: pl.BlockSpec((1,) + a.shape[1:], lambda b, i: (layer, 0, 0), pipeline_mode=pl.Buffered(1))
    specs = [
        _tok(D_MODEL),
        _mod_spec(B, n_lat_tiles),
        const(g_l),
        pl.BlockSpec((2, 1, TM, HG_WIDTH), lambda b, i: (0, b, i, 0)),
        _tok(HG_WIDTH),
        pl.BlockSpec((1, DA_WIDTH, TM), lambda b, i: (b, 0, jnp.minimum(i, n_lat_tiles - 1))),
        pl.BlockSpec((1, DA_WIDTH, TM), lambda b, i: (b, 0, jnp.maximum(i - n_lat_tiles, 0))),
        pl.BlockSpec((TM, FT_WIDTH), lambda b, i: (jnp.minimum(i, n_lat_tiles - 1), b)),
        pl.BlockSpec((TM, FT_WIDTH), lambda b, i: (jnp.maximum(i - n_lat_tiles, 0), b)),
        const(onorm_l), weight(wo), weight(wfi), weight(wfo),
    ]
    assert len(specs) == N_POST_IN
    return specs


def _post(X, mod_l, g_l, of, gate, da_lat, da_ctx, ft_lat, ft_ctx, onorm_l, wo, wfi, wfo, layer, n_lat_tiles, n_tiles):
    B, NT, D = X.shape
    return pl.pallas_call(
        functools.partial(_post_kernel, n_lat_tiles=n_lat_tiles),
        grid=(B, n_tiles),
        in_specs=_post_in_specs(B, g_l, onorm_l, wo, wfi, wfo, layer, n_lat_tiles),
        out_specs=_tok(D),
        out_shape=jax.ShapeDtypeStruct((B, n_tiles * TM, D), F32),
        compiler_params=_cparams(("arbitrary", "arbitrary")),
        name="outproj_ffn",
    )(X, mod_l, g_l, of, gate, da_lat, da_ctx, ft_lat, ft_ctx, onorm_l, wo, wfi, wfo)


def _post_inproj(X, post_args, inproj_args, layer, n_lat_tiles):
    B, NT, D = X.shape
    mod_l, g_l, of, gate, da_lat, da_ctx, ft_lat, ft_ctx, onorm_l, wo, wfi, wfo = post_args
    mod_n, g_n, w_in, cos_t, sin_t, lb_n, cs_tab = inproj_args
    in_specs, out_specs, out_shapes = _inproj_io(B, NT, g_n, w_in, layer + 1, lb_n, cs_tab, n_lat_tiles, True)
    outs = pl.pallas_call(
        functools.partial(_post_inproj_kernel, n_lat_tiles=n_lat_tiles),
        grid=(B, NT // TM),
        in_specs=_post_in_specs(B, g_l, onorm_l, wo, wfi, wfo, layer, n_lat_tiles) + in_specs,
        out_specs=(_tok(D),) + tuple(out_specs),
        out_shape=(jax.ShapeDtypeStruct((B, NT, D), F32),) + tuple(out_shapes),
        compiler_params=_cparams(("arbitrary", "arbitrary")),
        name="outproj_ffn_inproj",
    )(X, *post_args, *inproj_args)
    return outs[0], outs[1:]


def _rope_tables(t_lat, t_ctx):
    pos = jnp.arange(t_lat)
    inv_freq = 1.0 / (ROPE_THETA ** (jnp.arange(0, ROPE_AXIS_DIM, 2, dtype=F32) / ROPE_AXIS_DIM))
    ang = jnp.stack([pos // GRID_W, pos % GRID_W], axis=-1).astype(F32)[:, :, None] * inv_freq
    cos, sin = jnp.cos(ang), jnp.sin(ang)
    cos32 = jnp.stack([cos, cos], axis=2).reshape(t_lat, DA_QK)
    sin32 = jnp.stack([-sin, sin], axis=2).reshape(t_lat, DA_QK)
    reps = DA_WIDTH // DA_QK
    cos_t = jnp.concatenate([jnp.tile(cos32, (1, reps)), jnp.ones((t_ctx, DA_WIDTH), F32)], axis=0)
    sin_t = jnp.concatenate([jnp.tile(sin32, (1, reps)), jnp.zeros((t_ctx, DA_WIDTH), F32)], axis=0)
    return cos_t, sin_t


def _channel_table():
    idx = np.arange(FT_GDIM)
    ang = 2.0 * np.pi * ((idx[:, None] * idx[None, :]) % FT_GDIM) / FT_GDIM
    eye = np.eye(FT_GROUPS)
    return jnp.asarray(np.concatenate([np.kron(eye, np.cos(ang)), np.kron(eye, np.sin(ang))], axis=1), BF16)


def _position_tables(n, cols):
    lo = 1
    while lo * lo < n:
        lo *= 2
    hi = n // lo
    k = jnp.arange(n)
    a1 = 2.0 * np.pi * ((k[:, None] * jnp.arange(cols // lo)[None, :]) % hi).astype(F32) / hi
    a2 = 2.0 * np.pi * ((k[:, None] * jnp.arange(lo)[None, :]) % n).astype(F32) / n
    c1, s1 = jnp.cos(a1)[:, :, None], jnp.sin(a1)[:, :, None]
    c2, s2 = jnp.cos(a2)[:, None, :], jnp.sin(a2)[:, None, :]
    scale = 1.0 / math.sqrt(n * FT_GDIM)
    ct = ((c1 * c2 - s1 * s2) * scale).reshape(n, cols).astype(BF16)
    st_neg = ((s1 * c2 + c1 * s2) * (-scale)).reshape(n, cols).astype(BF16)
    return ct, st_neg


def kernel(x, c, ctx, c_ctx, w_mod, b_mod, norm_g, w_in, w_out, hg_lb_logits, hg_onorm,
           da_lambda, da_subln, w_ffn_in, w_ffn_out):
    B, T, D = x.shape
    Tc = ctx.shape[1]
    depth = w_mod.shape[0]
    assert D == D_MODEL and B + 1 <= MOD_ROWS
    assert T % TM == 0 and Tc % TM == 0 and T % KC == 0 and Tc % KC == 0 and T % CH == 0 and Tc % CH == 0
    n_lat_tiles = T // TM
    NT = T + Tc

    lam_init = [0.8 - 0.6 * math.exp(-0.3 * l) for l in range(depth)]
    lb, lam = _prep(hg_lb_logits, da_lambda, lam_init)

    cond = jnp.concatenate([c.astype(F32), c_ctx.astype(F32)[None, :],
                            jnp.zeros((MOD_ROWS - B - 1, D), F32)], axis=0)
    mods = _modulation(cond, w_mod, b_mod).reshape(depth, MOD_ROWS, N_MOD, D)

    cos_t, sin_t = _rope_tables(T, Tc)
    cs_tab = _channel_table()
    ct_lat, st_lat = _position_tables(T, T // 2)
    ct_ctx, st_ctx = _position_tables(Tc, Tc)

    w_in_b, w_out_b = w_in.astype(BF16), w_out.astype(BF16)
    w_ffn_in_b, w_ffn_out_b = w_ffn_in.astype(BF16), w_ffn_out.astype(BF16)
    X = jnp.concatenate([x, ctx], axis=1).astype(F32)
    g_all = norm_g.astype(F32)
    projected = _inproj(X, mods[0], g_all[0], w_in_b, 0, cos_t, sin_t, lb[0], cs_tab, n_lat_tiles)
    for l in range(depth):
        q, vi, gate, lf, dq, dk, dv, f = projected
        of = _hgrn(q, vi, lf, T)
        with_ctx = l < depth - 1
        da_lat, da_ctx = _attention(dq, dk, dv, lam[l:l + 1], da_subln[l].astype(F32).reshape(DA_V, 1), T,
                                    lam_init[l], with_ctx)
        ft_lat = _fourier_folded(ct_lat, st_lat, f, T)
        ft_ctx = _fourier(ct_ctx, st_ctx, f, T // Tc, Tc) if with_ctx else ft_lat
        post_args = (mods[l], g_all[l], of, gate, da_lat, da_ctx, ft_lat, ft_ctx,
                     hg_onorm[l].astype(F32).reshape(1, HG_DIM), w_out_b, w_ffn_in_b, w_ffn_out_b)
        if with_ctx:
            next_args = (mods[l + 1], g_all[l + 1], w_in_b, cos_t, sin_t, lb[l + 1], cs_tab)
            X, projected = _post_inproj(X, post_args, next_args, l, n_lat_tiles)
        else:
            X = _post(X, *post_args, l, n_lat_tiles, n_lat_tiles)
    return X.astype(x.dtype)
```

```python
import functools
import math

import numpy as np
import jax
import jax.numpy as jnp
from jax import lax
from jax.experimental import pallas as pl
from jax.experimental.pallas import tpu as pltpu

F32 = jnp.float32
BF16 = jnp.bfloat16

D_MODEL = 1024
GRID_W = 64
HG_WIDTH = 512
HG_DIM = 128
HG_HEADS = 4
DA_WIDTH = 256
DA_HEADS = 4
DA_V = 64
DA_QK = 32
FT_WIDTH = 256
FT_GROUPS = 4
FT_GDIM = FT_WIDTH // FT_GROUPS
FF_HIDDEN = 2816
N_MOD = 6
EPS = 1e-6
ROPE_THETA = 10000.0
ROPE_AXIS_DIM = DA_QK // 2

_OFF_Q, _OFF_I, _OFF_G, _OFF_F, _OFF_DQ, _OFF_DK, _OFF_DV, _OFF_FT, IN_WIDTH = (
    0, 512, 1024, 1536, 2560, 2816, 3072, 3328, 3584)

TM = 256
TQ = 256
TQ_CTX = 256
ATT_TILES = 16
ATT_BUFFERS = 4
KC = 256
CH = 128
SUB = 8
HGRN_UNROLL = 34
TK = 256
VT_ROWS = 80
MOD_ROWS = 8
MOD_TN = 1536
VMEM_LIMIT = 52 * 1024 * 1024

Q_SCALE = (DA_QK ** -0.5) * math.log2(math.e)


def _silu(x):
    return x * jax.nn.sigmoid(x)


def _rms(x):
    return x * lax.rsqrt(jnp.mean(x * x, axis=-1, keepdims=True) + EPS)


def _cparams(sem):
    return pltpu.CompilerParams(dimension_semantics=sem, vmem_limit_bytes=VMEM_LIMIT)


def _mod_kernel(c_ref, w_ref, b_ref, o_ref):
    a = _silu(c_ref[...]).astype(BF16)
    w = w_ref[0].astype(BF16)
    o_ref[0] = jnp.dot(a, w, preferred_element_type=F32) + b_ref[0]


def _modulation(cond, w_mod, b_mod):
    depth, d, n = w_mod.shape
    return pl.pallas_call(
        _mod_kernel,
        grid=(depth, n // MOD_TN),
        in_specs=[
            pl.BlockSpec((MOD_ROWS, d), lambda l, j: (0, 0)),
            pl.BlockSpec((1, d, MOD_TN), lambda l, j: (l, 0, j)),
            pl.BlockSpec((1, 1, MOD_TN), lambda l, j: (l, 0, j)),
        ],
        out_specs=pl.BlockSpec((1, MOD_ROWS, MOD_TN), lambda l, j: (l, 0, j)),
        out_shape=jax.ShapeDtypeStruct((depth, MOD_ROWS, n), F32),
        compiler_params=_cparams(("arbitrary", "arbitrary")),
        name="modulation",
    )(cond, w_mod, b_mod.reshape(depth, 1, n))


def _prep_kernel(lb_ref, lam_ref, lam_init_ref, lbo_ref, lamo_ref, *, depth):
    rows = [lb_ref[l:l + 1, :] for l in range(depth)]
    m = rows[0]
    for r in rows[1:]:
        m = jnp.maximum(m, r)
    e = [jnp.exp(r - m) for r in rows]
    tot = e[0]
    for r in e[1:]:
        tot = tot + r
    p = [r / tot for r in e]
    acc = p[0]
    lbo_ref[0:1, :] = acc - p[0]
    for l in range(1, depth):
        acc = acc + p[l]
        lbo_ref[l:l + 1, :] = acc - p[0]
    x = lam_ref[...]
    a = jnp.sum(x[:, 0:DA_QK] * x[:, DA_QK:2 * DA_QK], axis=-1, keepdims=True)
    b = jnp.sum(x[:, 2 * DA_QK:3 * DA_QK] * x[:, 3 * DA_QK:4 * DA_QK], axis=-1, keepdims=True)
    lamo_ref[...] = jnp.exp(a) - jnp.exp(b) + lam_init_ref[...]


def _prep(hg_lb_logits, da_lambda, lam_init):
    depth = hg_lb_logits.shape[1]
    lb_in = jnp.transpose(hg_lb_logits.astype(F32), (1, 0, 2)).reshape(depth, 2 * HG_WIDTH)
    lam_in = da_lambda.astype(F32).reshape(depth, 4 * DA_QK)
    lam_init_arr = jnp.asarray(np.broadcast_to(np.asarray(lam_init, np.float32)[:, None], (depth, TQ)))
    lb, lam = pl.pallas_call(
        functools.partial(_prep_kernel, depth=depth),
        out_shape=(jax.ShapeDtypeStruct((depth, 2 * HG_WIDTH), F32),
                   jax.ShapeDtypeStruct((depth, TQ), F32)),
        name="param_prep",
    )(lb_in, lam_in, lam_init_arr)
    return lb.reshape(depth, 2, HG_WIDTH), lam


def _inproj_kernel(x_ref, *refs):
    _inproj_compute(x_ref[0], *refs)


def _inproj_compute(x, mod_ref, g_ref, w_ref, cos_ref, sin_ref, lb_ref, cs_ref,
                    q_ref, i_ref, gate_ref, lf_ref, dq_ref, dk_ref, dv_ref, f_ref):
    mod = mod_ref[0]
    h = _rms(x) * (g_ref[0:1, :] * (1.0 + mod[1:2, :])) + mod[0:1, :]
    hb = h.astype(BF16)

    def proj(a, b):
        return jnp.dot(hb, w_ref[0, :, a:b], preferred_element_type=F32)

    q_ref[0] = _silu(proj(_OFF_Q, _OFF_I)).astype(BF16)
    i_ref[0] = proj(_OFF_I, _OFF_G).astype(BF16)
    gate_ref[0] = _silu(proj(_OFF_G, _OFF_F)).astype(BF16)
    for d in range(2):
        z = proj(_OFF_F + d * HG_WIDTH, _OFF_F + (d + 1) * HG_WIDTH)
        lb = lb_ref[d:d + 1, :]
        lf_ref[d, 0] = jnp.log(lb + (1.0 - lb) * jax.nn.sigmoid(z))

    cos = cos_ref[...]
    sin = sin_ref[...]
    lane = lax.broadcasted_iota(jnp.int32, cos.shape, 1)
    upper_half = (lane & (ROPE_AXIS_DIM // 2)) != 0

    def rope(t):
        partner = jnp.where(upper_half,
                            pltpu.roll(t, ROPE_AXIS_DIM // 2, 1),
                            pltpu.roll(t, 2 * DA_HEADS * DA_QK - ROPE_AXIS_DIM // 2, 1))
        return t * cos + partner * sin

    dq_ref[0] = (rope(proj(_OFF_DQ, _OFF_DK)) * Q_SCALE).T.astype(BF16)
    dk_ref[0] = rope(proj(_OFF_DK, _OFF_DV)).astype(BF16)
    dv_ref[0] = proj(_OFF_DV, _OFF_FT).T.astype(BF16)
    ft = proj(_OFF_FT, IN_WIDTH).astype(BF16)
    cs = jnp.dot(ft, cs_ref[...], preferred_element_type=F32)
    f_ref[0] = cs[:, :FT_WIDTH].astype(BF16)
    f_ref[1] = cs[:, FT_WIDTH:].astype(BF16)


def _tok(w):
    return pl.BlockSpec((1, TM, w), lambda b, i: (b, i, 0))


def _mod_spec(B, n_lat_tiles):
    return pl.BlockSpec((1, N_MOD, D_MODEL), lambda b, i: (jnp.where(i < n_lat_tiles, b, B), 0, 0))


def _inproj_io(B, NT, g_l, w_in, layer, lb_l, cs_tab, n_lat_tiles, single):
    tok = _tok
    mode = dict(pipeline_mode=pl.Buffered(1)) if single else {}
    full = lambda a: pl.BlockSpec(a.shape, lambda b, i: (0,) * a.ndim, **mode)
    tok_t = pl.BlockSpec((1, DA_WIDTH, TM), lambda b, i: (b, 0, i))
    in_specs = [
        _mod_spec(B, n_lat_tiles),
        full(g_l),
        pl.BlockSpec((1,) + w_in.shape[1:], lambda b, i: (layer, 0, 0), **mode),
        pl.BlockSpec((TM, DA_WIDTH), lambda b, i: (i, 0)),
        pl.BlockSpec((TM, DA_WIDTH), lambda b, i: (i, 0)),
        full(lb_l), full(cs_tab),
    ]
    out_shapes = (
        jax.ShapeDtypeStruct((B, NT, HG_WIDTH), BF16),
        jax.ShapeDtypeStruct((B, NT, HG_WIDTH), BF16),
        jax.ShapeDtypeStruct((B, NT, HG_WIDTH), BF16),
        jax.ShapeDtypeStruct((2, B, NT, HG_WIDTH), F32),
        jax.ShapeDtypeStruct((B, DA_WIDTH, NT), BF16),
        jax.ShapeDtypeStruct((B, NT, DA_WIDTH), BF16),
        jax.ShapeDtypeStruct((B, DA_WIDTH, NT), BF16),
        jax.ShapeDtypeStruct((2, NT, B * FT_WIDTH), BF16),
    )
    out_specs = (
        tok(HG_WIDTH), tok(HG_WIDTH), tok(HG_WIDTH),
        pl.BlockSpec((2, 1, TM, HG_WIDTH), lambda b, i: (0, b, i, 0)),
        tok_t, tok(DA_WIDTH), tok_t,
        pl.BlockSpec((2, TM, FT_WIDTH), lambda b, i: (0, i, b)),
    )
    return in_specs, out_specs, out_shapes


def _inproj(X, mod_l, g_l, w_in, layer, cos_t, sin_t, lb_l, cs_tab, n_lat_tiles):
    B, NT, D = X.shape
    in_specs, out_specs, out_shapes = _inproj_io(B, NT, g_l, w_in, layer, lb_l, cs_tab, n_lat_tiles, False)
    return pl.pallas_call(
        _inproj_kernel,
        grid=(B, NT // TM),
        in_specs=[_tok(D)] + in_specs,
        out_specs=out_specs,
        out_shape=out_shapes,
        compiler_params=_cparams(("arbitrary", "arbitrary")),
        name="adaln_inproj",
    )(X, mod_l, g_l, w_in, cos_t, sin_t, lb_l, cs_tab)


def _hgrn_direction(q, f, v, g, st, cmat, codes, rev):
    C, K = q.shape
    k = 1.0 - f
    g_hi = g.astype(BF16)
    g_lo = (g - g_hi.astype(F32)).astype(BF16)
    cs = jnp.dot(cmat, jnp.concatenate([g_hi, g_lo], axis=1), preferred_element_type=F32)
    bc = cs[:, :K] + cs[:, K:]
    tot = bc[0:1, :] if rev else bc[C - 1:C, :]

    nt_dims = (((1,), (1,)), ((), ()))
    tn_dims = (((0,), (0,)), ((), ()))
    qd = (q * jnp.exp(bc)).astype(BF16)
    o = lax.dot_general(qd, st.astype(BF16), nt_dims, preferred_element_type=F32)
    kdec = (k * jnp.exp(tot - bc)).astype(BF16)
    st_new = st * jnp.exp(tot) + lax.dot_general(v, kdec, tn_dims, preferred_element_type=F32)

    a = jnp.zeros((C, C), F32)
    f3 = f.reshape(C // SUB, SUB, K)
    qd_d = q
    for d8 in range(SUB):
        f_shift = f if d8 == 0 else pltpu.roll(f3, (SUB - d8) if rev else d8, 1).reshape(C, K)
        qd_next = qd_d * f_shift
        red = jnp.sum(qd_d - qd_next, axis=-1, keepdims=True)
        a = jnp.where(codes == d8, red, a)
        qd_d = qd_next

    zeros8 = jnp.zeros((SUB, K), F32)
    b = SUB
    level = 0
    while b < C:
        ql, kl = [], []
        for r in range(0, C, SUB):
            base = (r // (2 * b)) * 2 * b
            upper = (r - base) >= b
            ref_row = base + b if rev else base + b - 1
            q_side = (not upper) if rev else upper
            bref = bc[ref_row:ref_row + 1, :]
            bg = bc[r:r + SUB, :]
            if q_side:
                ql.append(q[r:r + SUB, :] * jnp.exp(bg - bref))
                kl.append(zeros8)
            else:
                kl.append(k[r:r + SUB, :] * jnp.exp(bref - bg))
                ql.append(zeros8)
        qlb = jnp.concatenate(ql, axis=0).astype(BF16)
        klb = jnp.concatenate(kl, axis=0).astype(BF16)
        p = lax.dot_general(qlb, klb, nt_dims, preferred_element_type=F32)
        a = jnp.where(codes == SUB + level, p, a)
        b *= 2
        level += 1

    o = o + jnp.dot(a.astype(BF16), v, preferred_element_type=F32)
    return o, st_new


def _hgrn_tables(C):
    t = np.arange(C)[:, None]
    s = np.arange(C)[None, :]
    cmats, codes = [], []
    for rev in (False, True):
        d = (s - t) if rev else (t - s)
        cmats.append((d >= 0).astype(np.float32))
        code = np.full((C, C), -1, np.int32)
        inside = (d >= 0) & ((t // SUB) == (s // SUB))
        code[inside] = d[inside]
        b, level = SUB, 0
        while b < C:
            split = (d > 0) & ((t // (2 * b)) == (s // (2 * b))) & ((t // b) != (s // b))
            code[split] = SUB + level
            b *= 2
            level += 1
        codes.append(code)
    return jnp.asarray(np.stack(cmats), BF16), jnp.asarray(np.stack(codes), jnp.int32)


def _hgrn_kernel(q_ref, v_ref, lf_ref, cmat_ref, code_ref, o_ref, st_ref, *, n_lat, n_ctx):
    C = CH
    n_chunks = n_lat + n_ctx
    st_ref[...] = jnp.zeros(st_ref.shape, F32)

    def body(j, carry):
        for d, rev in enumerate((False, True)):
            c = (n_chunks - 1 - j) if rev else jnp.where(j < n_ctx, n_lat + j, j - n_ctx)
            r0 = pl.multiple_of(c * C, C)
            q = q_ref[0, pl.ds(r0, C), :].astype(F32)
            v = v_ref[0, pl.ds(r0, C), :]
            g = lf_ref[d, 0, pl.ds(r0, C), :]
            o, st_new = _hgrn_direction(q, jnp.exp(g), v, g, st_ref[d], cmat_ref[d], code_ref[d], rev)
            st_ref[d] = st_new
            o_ref[d, 0, pl.ds(r0, C), :] = o.astype(o_ref.dtype)
        return carry

    lax.fori_loop(0, n_chunks, body, 0, unroll=HGRN_UNROLL)


def _hgrn(q, v, lf, n_lat_rows):
    B, NT, _ = q.shape
    n_lat = n_lat_rows // CH
    n_ctx = (NT - n_lat_rows) // CH
    assert (n_lat + n_ctx) % HGRN_UNROLL == 0
    cmat, codes = _hgrn_tables(CH)
    head = pl.BlockSpec((1, NT, HG_DIM), lambda b, h: (b, 0, h))
    both = pl.BlockSpec((2, 1, NT, HG_DIM), lambda b, h: (0, b, 0, h))
    table = pl.BlockSpec((2, CH, CH), lambda b, h: (0, 0, 0))
    return pl.pallas_call(
        functools.partial(_hgrn_kernel, n_lat=n_lat, n_ctx=n_ctx),
        grid=(B, HG_HEADS),
        in_specs=[head, head, both, table, table],
        out_specs=both,
        out_shape=jax.ShapeDtypeStruct((2, B, NT, HG_WIDTH), BF16),
        scratch_shapes=[pltpu.VMEM((2, HG_DIM, HG_DIM), F32)],
        compiler_params=_cparams(("arbitrary", "arbitrary")),
        name="hgrn_scan",
    )(q, v, lf, cmat, codes)


def _attn_kernel(qT_ref, k_ref, vT_ref, lam_ref, sub_ref, o_ref, *s_refs, chunks, n_q, tq, out_scale):
    n = len(chunks)
    odd_head = (pl.program_id(1) % 2) == 1
    ones_row = lax.broadcasted_iota(jnp.int32, (VT_ROWS - DA_V, n * KC), 0) == 0
    vt = jnp.concatenate(
        [vT_ref[0, :, chunks[0] * KC:(chunks[-1] + 1) * KC],
         jnp.where(ones_row, 1.0, 0.0).astype(BF16)], axis=0)
    streams = [(t, mp) for t in range(n_q) for mp in range(2)]
    zeros = jnp.zeros((DA_QK, tq), BF16)

    def query_operand(t, mp):
        q = qT_ref[0, mp * DA_QK:(mp + 1) * DA_QK, t * tq:(t + 1) * tq]
        even = jnp.concatenate([q if j == mp else zeros for j in range(4)], axis=0)
        odd = jnp.concatenate([q if j == 2 + mp else zeros for j in range(4)], axis=0)
        return jnp.where(odd_head, odd, even)

    row0 = pl.multiple_of(jnp.minimum(pl.program_id(2), 0) * KC, KC)

    def pass1(slot):
        rhs = query_operand(*streams[slot])
        m8 = None
        for i, c in enumerate(chunks):
            s = jnp.dot(k_ref[0, c * KC:(c + 1) * KC, :], rhs, preferred_element_type=F32)
            s_refs[slot % len(s_refs)][i * KC:(i + 1) * KC, :] = s
            parts = [s[r:r + 8, :] for r in range(0, KC, 8)]
            while len(parts) > 1:
                parts = [jnp.maximum(parts[j], parts[j + 1]) for j in range(0, len(parts), 2)]
            m8 = parts[0] if m8 is None else jnp.maximum(m8, parts[0])
        return jnp.max(m8, axis=0, keepdims=True)

    def pass2(slot, m):
        p = jnp.exp2((s_refs[slot % len(s_refs)][pl.ds(row0, n * KC), :] - m).astype(BF16))
        return jnp.dot(vt, p, preferred_element_type=F32)

    accs, maxes = {}, {}
    for slot in range(len(streams) + 1):
        if slot < len(streams):
            maxes[slot] = pass1(slot)
        if slot >= 1:
            accs[streams[slot - 1]] = pass2(slot - 1, maxes[slot - 1])

    for t in range(n_q):
        acc0, acc1 = accs[(t, 0)], accs[(t, 1)]
        o = (acc0[0:DA_V] / acc0[DA_V:DA_V + 1]
             - lam_ref[...] * (acc1[0:DA_V] / acc1[DA_V:DA_V + 1]))
        y = o * lax.rsqrt(jnp.mean(o * o, axis=0, keepdims=True) + EPS)
        o_ref[0, :, t * tq:(t + 1) * tq] = (y * sub_ref[...] * out_scale).astype(BF16)


def _attention_call(dqT, dk, dvT, lam_l, subln_l, chunks, n_q, tq, q_block0, n_steps, lam_init):
    B, _, NT = dqT.shape
    w = n_q * tq
    kern = functools.partial(_attn_kernel, chunks=tuple(chunks), n_q=n_q, tq=tq, out_scale=1.0 - lam_init)
    return pl.pallas_call(
        kern,
        grid=(B, DA_HEADS, n_steps),
        in_specs=[
            pl.BlockSpec((1, 2 * DA_QK, w), lambda b, h, i: (b, h, q_block0 + i)),
            pl.BlockSpec((1, NT, 2 * DA_V), lambda b, h, i: (b, 0, h // 2)),
            pl.BlockSpec((1, DA_V, NT), lambda b, h, i: (b, h, 0)),
            pl.BlockSpec((1, tq), lambda b, h, i: (0, 0)),
            pl.BlockSpec((DA_V, 1), lambda b, h, i: (0, 0)),
        ],
        out_specs=pl.BlockSpec((1, DA_V, w), lambda b, h, i: (b, h, i)),
        out_shape=jax.ShapeDtypeStruct((B, DA_WIDTH, n_steps * w), BF16),
        scratch_shapes=[pltpu.VMEM((len(chunks) * KC, tq), F32) for _ in range(min(2 * n_q, ATT_BUFFERS))],
        compiler_params=_cparams(("arbitrary", "arbitrary", "arbitrary")),
        name="diff_attention",
    )(dqT, dk, dvT, lam_l, subln_l)


def _attention(dqT, dk, dvT, lam_l, subln_l, n_lat_rows, lam_init, with_ctx):
    NT = dqT.shape[2]
    n_chunks, n_lat_chunks = NT // KC, n_lat_rows // KC
    n_ctx_rows = NT - n_lat_rows
    lat = _attention_call(dqT, dk, dvT, lam_l, subln_l, range(n_chunks), ATT_TILES, TQ, 0,
                          n_lat_rows // (ATT_TILES * TQ), lam_init)
    if not with_ctx:
        return lat, lat
    ctx = _attention_call(dqT, dk, dvT, lam_l, subln_l, range(n_lat_chunks, n_chunks), 1, TQ_CTX,
                          n_lat_rows // TQ_CTX, n_ctx_rows // TQ_CTX, lam_init)
    return lat, ctx


def _fourier_kernel(ct_ref, st_ref, f_ref, o_ref):
    o_ref[...] = (jnp.dot(ct_ref[...], f_ref[0], preferred_element_type=F32)
                  + jnp.dot(st_ref[...], f_ref[1], preferred_element_type=F32)).astype(BF16)


def _fourier(ct, st_neg, f, row_block, n_rows):
    W = f.shape[2]
    tk = min(TK, n_rows)
    return pl.pallas_call(
        _fourier_kernel,
        grid=(n_rows // tk,),
        in_specs=[
            pl.BlockSpec((tk, n_rows), lambda i: (i, 0)),
            pl.BlockSpec((tk, n_rows), lambda i: (i, 0)),
            pl.BlockSpec((2, n_rows, W), lambda i: (0, row_block, 0)),
        ],
        out_specs=pl.BlockSpec((tk, W), lambda i: (i, 0)),
        out_shape=jax.ShapeDtypeStruct((n_rows, W), BF16),
        compiler_params=_cparams(("arbitrary",)),
        name="fourier_mix",
    )(ct, st_neg, f)


def _fourier_folded_kernel(ct_ref, st_ref, f_ref, o_ref, xs_ref, xa_ref, *, scale):
    tk = o_ref.shape[0]
    half = xs_ref.shape[0]

    @pl.when(pl.program_id(0) == 0)
    def _():
        n_all = f_ref.shape[1] // tk
        r = lax.broadcasted_iota(jnp.int32, (tk, tk), 0)
        c = lax.broadcasted_iota(jnp.int32, (tk, tk), 1)
        body = jnp.where(r + c == tk, 1.0, 0.0).astype(BF16)
        head = jnp.where((r == 0) & (c == 0), 1.0, 0.0).astype(BF16)
        first_row = lax.broadcasted_iota(jnp.int32, (tk, xs_ref.shape[1]), 0) == 0
        for j in range(half // tk):
            rows = slice(j * tk, (j + 1) * tk)
            lo_t = slice((n_all - 1 - j) * tk, (n_all - j) * tk)
            hi_t = slice(((n_all - j) % n_all) * tk, ((n_all - j) % n_all + 1) * tk)
            m0, m1 = [jnp.dot(body, f_ref[p, lo_t, :], preferred_element_type=F32)
                      + jnp.dot(head, f_ref[p, hi_t, :], preferred_element_type=F32) for p in range(2)]
            if j == 0:
                m0 = jnp.where(first_row, 0.0, m0)
            xs_ref[rows, :] = (f_ref[0, rows, :].astype(F32) + m0).astype(BF16)
            xa_ref[rows, :] = (f_ref[1, rows, :].astype(F32) - m1).astype(BF16)

    k = pl.program_id(0) * tk + lax.broadcasted_iota(jnp.int32, (tk, 1), 0)
    sign = (1 - 2 * (k & 1)).astype(F32)
    mid = f_ref[0, half:half + 1, :].astype(F32) * scale
    o_ref[...] = (jnp.dot(ct_ref[...], xs_ref[...], preferred_element_type=F32)
                  + jnp.dot(st_ref[...], xa_ref[...], preferred_element_type=F32)
                  + sign * mid).astype(BF16)


def _fourier_folded(ct_half, st_neg_half, f, n_rows):
    W = f.shape[2]
    half = n_rows // 2
    return pl.pallas_call(
        functools.partial(_fourier_folded_kernel, scale=1.0 / math.sqrt(n_rows * FT_GDIM)),
        grid=(n_rows // TK,),
        in_specs=[
            pl.BlockSpec((TK, half), lambda i: (i, 0)),
            pl.BlockSpec((TK, half), lambda i: (i, 0)),
            pl.BlockSpec((2, n_rows, W), lambda i: (0, 0, 0), pipeline_mode=pl.Buffered(1)),
        ],
        out_specs=pl.BlockSpec((TK, W), lambda i: (i, 0)),
        out_shape=jax.ShapeDtypeStruct((n_rows, W), BF16),
        scratch_shapes=[pltpu.VMEM((half, W), BF16), pltpu.VMEM((half, W), BF16)],
        compiler_params=_cparams(("arbitrary",)),
        name="fourier_mix_folded",
    )(ct_half, st_neg_half, f)


N_POST_IN = 13


def _post_kernel(*refs, n_lat_tiles):
    refs[N_POST_IN][0] = _post_compute(*refs[:N_POST_IN], n_lat_tiles=n_lat_tiles)


def _post_inproj_kernel(*refs, n_lat_tiles):
    n_in = N_POST_IN + 7
    x_new = _post_compute(*refs[:N_POST_IN], n_lat_tiles=n_lat_tiles)
    refs[n_in][0] = x_new
    _inproj_compute(x_new, *refs[N_POST_IN:n_in], *refs[n_in + 1:])


def _post_compute(x_ref, mod_ref, g_ref, of_ref, gate_ref, dal_ref, dac_ref, ftl_ref, ftc_ref, onorm_ref,
                  wo_ref, wfi_ref, wfo_ref, *, n_lat_tiles):
    x = x_ref[0]
    mod = mod_ref[0]
    is_lat = pl.program_id(1) < n_lat_tiles
    ft = jnp.where(is_lat, ftl_ref[...], ftc_ref[...])
    da_t = jnp.where(is_lat, dal_ref[0], dac_ref[0])
    o = of_ref[0, 0].astype(F32) + of_ref[1, 0].astype(F32)
    onorm = onorm_ref[...]
    heads = [_rms(o[:, h * HG_DIM:(h + 1) * HG_DIM]) * onorm for h in range(HG_HEADS)]
    hg = (jnp.concatenate(heads, axis=-1) * gate_ref[0].astype(F32)).astype(BF16)
    y = (jnp.dot(hg, wo_ref[0, 0:HG_WIDTH, :], preferred_element_type=F32)
         + lax.dot_general(da_t, wo_ref[0, HG_WIDTH:HG_WIDTH + DA_WIDTH, :], (((0,), (0,)), ((), ())),
                           preferred_element_type=F32)
         + jnp.dot(ft, wo_ref[0, HG_WIDTH + DA_WIDTH:, :], preferred_element_type=F32))
    x1 = x + _rms(y) * (mod[2:3, :] * g_ref[1:2, :])
    h2 = (_rms(x1) * (g_ref[2:3, :] * (1.0 + mod[4:5, :])) + mod[3:4, :]).astype(BF16)
    gu = jnp.dot(h2, wfi_ref[0], preferred_element_type=F32)
    act = (_silu(gu[:, :FF_HIDDEN]) * gu[:, FF_HIDDEN:]).astype(BF16)
    y2 = jnp.dot(act, wfo_ref[0], preferred_element_type=F32)
    return x1 + _rms(y2) * (mod[5:6, :] * g_ref[3:4, :])


def _post_in_specs(B, g_l, onorm_l, wo, wfi, wfo, layer, n_lat_tiles):
    const = lambda a: pl.BlockSpec(a.shape, lambda b, i: (0,) * a.ndim, pipeline_mode=pl.Buffered(1))
    weight = lambda a: pl.BlockSpec((1,) + a.shape[1:], lambda b, i: (layer, 0, 0), pipeline_mode=pl.Buffered(1))
    specs = [
        _tok(D_MODEL),
        _mod_spec(B, n_lat_tiles),
        const(g_l),
        pl.BlockSpec((2, 1, TM, HG_WIDTH), lambda b, i: (0, b, i, 0)),
        _tok(HG_WIDTH),
        pl.BlockSpec((1, DA_WIDTH, TM), lambda b, i: (b, 0, jnp.minimum(i, n_lat_tiles - 1))),
        pl.BlockSpec((1, DA_WIDTH, TM), lambda b, i: (b, 0, jnp.maximum(i - n_lat_tiles, 0))),
        pl.BlockSpec((TM, FT_WIDTH), lambda b, i: (jnp.minimum(i, n_lat_tiles - 1), b)),
        pl.BlockSpec((TM, FT_WIDTH), lambda b, i: (jnp.maximum(i - n_lat_tiles, 0), b)),
        const(onorm_l), weight(wo), weight(wfi), weight(wfo),
    ]
    assert len(specs) == N_POST_IN
    return specs


def _post(X, mod_l, g_l, of, gate, da_lat, da_ctx, ft_lat, ft_ctx, onorm_l, wo, wfi, wfo, layer, n_lat_tiles, n_tiles):
    B, NT, D = X.shape
    return pl.pallas_call(
        functools.partial(_post_kernel, n_lat_tiles=n_lat_tiles),
        grid=(B, n_tiles),
        in_specs=_post_in_specs(B, g_l, onorm_l, wo, wfi, wfo, layer, n_lat_tiles),
        out_specs=_tok(D),
        out_shape=jax.ShapeDtypeStruct((B, n_tiles * TM, D), F32),
        compiler_params=_cparams(("arbitrary", "arbitrary")),
        name="outproj_ffn",
    )(X, mod_l, g_l, of, gate, da_lat, da_ctx, ft_lat, ft_ctx, onorm_l, wo, wfi, wfo)


def _post_inproj(X, post_args, inproj_args, layer, n_lat_tiles):
    B, NT, D = X.shape
    mod_l, g_l, of, gate, da_lat, da_ctx, ft_lat, ft_ctx, onorm_l, wo, wfi, wfo = post_args
    mod_n, g_n, w_in, cos_t, sin_t, lb_n, cs_tab = inproj_args
    in_specs, out_specs, out_shapes = _inproj_io(B, NT, g_n, w_in, layer + 1, lb_n, cs_tab, n_lat_tiles, True)
    outs = pl.pallas_call(
        functools.partial(_post_inproj_kernel, n_lat_tiles=n_lat_tiles),
        grid=(B, NT // TM),
        in_specs=_post_in_specs(B, g_l, onorm_l, wo, wfi, wfo, layer, n_lat_tiles) + in_specs,
        out_specs=(_tok(D),) + tuple(out_specs),
        out_shape=(jax.ShapeDtypeStruct((B, NT, D), F32),) + tuple(out_shapes),
        compiler_params=_cparams(("arbitrary", "arbitrary")),
        name="outproj_ffn_inproj",
    )(X, *post_args, *inproj_args)
    return outs[0], outs[1:]


def _rope_tables(t_lat, t_ctx):
    pos = jnp.arange(t_lat)
    inv_freq = 1.0 / (ROPE_THETA ** (jnp.arange(0, ROPE_AXIS_DIM, 2, dtype=F32) / ROPE_AXIS_DIM))
    ang = jnp.stack([pos // GRID_W, pos % GRID_W], axis=-1).astype(F32)[:, :, None] * inv_freq
    cos, sin = jnp.cos(ang), jnp.sin(ang)
    cos32 = jnp.stack([cos, cos], axis=2).reshape(t_lat, DA_QK)
    sin32 = jnp.stack([-sin, sin], axis=2).reshape(t_lat, DA_QK)
    reps = DA_WIDTH // DA_QK
    cos_t = jnp.concatenate([jnp.tile(cos32, (1, reps)), jnp.ones((t_ctx, DA_WIDTH), F32)], axis=0)
    sin_t = jnp.concatenate([jnp.tile(sin32, (1, reps)), jnp.zeros((t_ctx, DA_WIDTH), F32)], axis=0)
    return cos_t, sin_t


def _channel_table():
    idx = np.arange(FT_GDIM)
    ang = 2.0 * np.pi * ((idx[:, None] * idx[None, :]) % FT_GDIM) / FT_GDIM
    eye = np.eye(FT_GROUPS)
    return jnp.asarray(np.concatenate([np.kron(eye, np.cos(ang)), np.kron(eye, np.sin(ang))], axis=1), BF16)


def _position_tables(n, cols):
    lo = 1
    while lo * lo < n:
        lo *= 2
    hi = n // lo
    k = jnp.arange(n)
    a1 = 2.0 * np.pi * ((k[:, None] * jnp.arange(cols // lo)[None, :]) % hi).astype(F32) / hi
    a2 = 2.0 * np.pi * ((k[:, None] * jnp.arange(lo)[None, :]) % n).astype(F32) / n
    c1, s1 = jnp.cos(a1)[:, :, None], jnp.sin(a1)[:, :, None]
    c2, s2 = jnp.cos(a2)[:, None, :], jnp.sin(a2)[:, None, :]
    scale = 1.0 / math.sqrt(n * FT_GDIM)
    ct = ((c1 * c2 - s1 * s2) * scale).reshape(n, cols).astype(BF16)
    st_neg = ((s1 * c2 + c1 * s2) * (-scale)).reshape(n, cols).astype(BF16)
    return ct, st_neg


def kernel(x, c, ctx, c_ctx, w_mod, b_mod, norm_g, w_in, w_out, hg_lb_logits, hg_onorm,
           da_lambda, da_subln, w_ffn_in, w_ffn_out):
    B, T, D = x.shape
    Tc = ctx.shape[1]
    depth = w_mod.shape[0]
    assert D == D_MODEL and B + 1 <= MOD_ROWS
    assert T % TM == 0 and Tc % TM == 0 and T % KC == 0 and Tc % KC == 0 and T % CH == 0 and Tc % CH == 0
    n_lat_tiles = T // TM
    NT = T + Tc

    lam_init = [0.8 - 0.6 * math.exp(-0.3 * l) for l in range(depth)]
    lb, lam = _prep(hg_lb_logits, da_lambda, lam_init)

    cond = jnp.concatenate([c.astype(F32), c_ctx.astype(F32)[None, :],
                            jnp.zeros((MOD_ROWS - B - 1, D), F32)], axis=0)
    mods = _modulation(cond, w_mod, b_mod).reshape(depth, MOD_ROWS, N_MOD, D)

    cos_t, sin_t = _rope_tables(T, Tc)
    cs_tab = _channel_table()
    ct_lat, st_lat = _position_tables(T, T // 2)
    ct_ctx, st_ctx = _position_tables(Tc, Tc)

    w_in_b, w_out_b = w_in.astype(BF16), w_out.astype(BF16)
    w_ffn_in_b, w_ffn_out_b = w_ffn_in.astype(BF16), w_ffn_out.astype(BF16)
    X = jnp.concatenate([x, ctx], axis=1).astype(F32)
    g_all = norm_g.astype(F32)
    projected = _inproj(X, mods[0], g_all[0], w_in_b, 0, cos_t, sin_t, lb[0], cs_tab, n_lat_tiles)
    for l in range(depth):
        q, vi, gate, lf, dq, dk, dv, f = projected
        of = _hgrn(q, vi, lf, T)
        with_ctx = l < depth - 1
        da_lat, da_ctx = _attention(dq, dk, dv, lam[l:l + 1], da_subln[l].astype(F32).reshape(DA_V, 1), T,
                                    lam_init[l], with_ctx)
        ft_lat = _fourier_folded(ct_lat, st_lat, f, T)
        ft_ctx = _fourier(ct_ctx, st_ctx, f, T // Tc, Tc) if with_ctx else ft_lat
        post_args = (mods[l], g_all[l], of, gate, da_lat, da_ctx, ft_lat, ft_ctx,
                     hg_onorm[l].astype(F32).reshape(1, HG_DIM), w_out_b, w_ffn_in_b, w_ffn_out_b)
        if with_ctx:
            next_args = (mods[l + 1], g_all[l + 1], w_in_b, cos_t, sin_t, lb[l + 1], cs_tab)
            X, projected = _post_inproj(X, post_args, next_args, l, n_lat_tiles)
        else:
            X = _post(X, *post_args, l, n_lat_tiles, n_lat_tiles)
    return X.astype(x.dtype)
```
